```python
import jax, jax.numpy as jnp
from jax import lax
import numpy as np

D_MODEL = 2048
BATCH = 2
SEQ = 4096
DEPTH = 1

MLA_HEADS = 8
MLA_Q_RANK = 512
MLA_KV_RANK = 256
MLA_NOPE_DIM = 128
MLA_ROPE_DIM = 64
MLA_V_DIM = 128
MLA_QK_DIM = MLA_NOPE_DIM + MLA_ROPE_DIM
MLA_Q_BLOCK = 128
MOBA_HEADS = 8
MOBA_HEAD_DIM = 128
MOBA_BLOCK = 256
MOBA_TOPK = 3
MOBA_Q_CHUNK = 32
MLA_WIDTH = MLA_HEADS * MLA_V_DIM
MOBA_WIDTH = MOBA_HEADS * MOBA_HEAD_DIM
MIX_WIDTH = MLA_WIDTH + MOBA_WIDTH
IN_SPLITS = (MLA_Q_RANK, MLA_KV_RANK, MLA_ROPE_DIM, MOBA_WIDTH, MOBA_WIDTH, MOBA_WIDTH)
IN_PROJ_DIM = sum(IN_SPLITS)
FFN_DIM = 5632
ROPE_THETA = 10000.0
NORM_EPS = 1e-6
NEG_INF = -1e30

kernel_name = "hymba_mla_moba_macaron_block"


def rms_norm(x, g):
    xf = x.astype(jnp.float32)
    y = xf * lax.rsqrt(jnp.mean(xf * xf, axis=-1, keepdims=True) + NORM_EPS)
    return (y * g.astype(jnp.float32)).astype(x.dtype)


def rope_tables(seq, dim):
    inv = 1.0 / (ROPE_THETA ** (jnp.arange(0, dim, 2, dtype=jnp.float32) / dim))
    ang = jnp.arange(seq, dtype=jnp.float32)[:, None] * inv[None, :]
    return jnp.cos(ang), jnp.sin(ang)


def apply_rope(x, cos, sin):
    half = x.shape[-1] // 2
    x1, x2 = x[..., :half], x[..., half:]
    c = cos.astype(x.dtype)
    s = sin.astype(x.dtype)
    return jnp.concatenate([x1 * c - x2 * s, x2 * c + x1 * s], axis=-1)


def swiglu(x, w_gate, w_up, w_down):
    return (jax.nn.silu(x @ w_gate) * (x @ w_up)) @ w_down


def mla_attention(c_q, c_kv, k_rope_raw, g_q, g_kv, w_uq, w_ukv):
    B, S, _ = c_q.shape
    H = MLA_HEADS
    q = (rms_norm(c_q, g_q) @ w_uq).reshape(B, S, H, MLA_QK_DIM).transpose(0, 2, 1, 3)
    kv = (rms_norm(c_kv, g_kv) @ w_ukv).reshape(B, S, H, MLA_NOPE_DIM + MLA_V_DIM).transpose(0, 2, 1, 3)
    q_nope, q_pe = q[..., :MLA_NOPE_DIM], q[..., MLA_NOPE_DIM:]
    k_nope, v = kv[..., :MLA_NOPE_DIM], kv[..., MLA_NOPE_DIM:]
    cos, sin = rope_tables(S, MLA_ROPE_DIM)
    q_pe = apply_rope(q_pe, cos, sin)
    k_pe = apply_rope(k_rope_raw[:, None], cos, sin)
    q = jnp.concatenate([q_nope, q_pe], axis=-1)
    k = jnp.concatenate([k_nope, jnp.broadcast_to(k_pe, (B, H, S, MLA_ROPE_DIM))], axis=-1)
    scale = MLA_QK_DIM ** -0.5
    nq = S // MLA_Q_BLOCK
    qb = q.reshape(B, H, nq, MLA_Q_BLOCK, MLA_QK_DIM).transpose(2, 0, 1, 3, 4)
    key_pos = jnp.arange(S)

    def block(args):
        qi, i = args
        s = jnp.einsum('bhqd,bhkd->bhqk', qi, k).astype(jnp.float32) * scale
        qpos = i * MLA_Q_BLOCK + jnp.arange(MLA_Q_BLOCK)
        s = jnp.where(key_pos[None, :] <= qpos[:, None], s, NEG_INF)
        p = jax.nn.softmax(s, axis=-1).astype(v.dtype)
        return jnp.einsum('bhqk,bhkd->bhqd', p, v)

    out = lax.map(block, (qb, jnp.arange(nq)))
    return out.transpose(1, 0, 3, 2, 4).reshape(B, S, H * MLA_V_DIM)


def moba_attention(q, k, v):
    B, S, _ = q.shape
    H, D, L = MOBA_HEADS, MOBA_HEAD_DIM, MOBA_BLOCK
    q = q.reshape(B, S, H, D).transpose(0, 2, 1, 3)
    k = k.reshape(B, S, H, D).transpose(0, 2, 1, 3)
    v = v.reshape(B, S, H, D).transpose(0, 2, 1, 3)
    cos, sin = rope_tables(S, D)
    q = apply_rope(q, cos, sin)
    k = apply_rope(k, cos, sin)
    nb = -(-S // L)
    pad = nb * L - S
    kp = jnp.pad(k, ((0, 0), (0, 0), (0, pad), (0, 0)))
    vp = jnp.pad(v, ((0, 0), (0, 0), (0, pad), (0, 0)))
    kb = kp.reshape(B, H, nb, L, D)
    vb = vp.reshape(B, H, nb, L, D)
    k_mean = jnp.mean(kb.astype(jnp.float32), axis=3)
    gate = jnp.einsum('bhsd,bhnd->bhsn', q.astype(jnp.float32), k_mean)
    q_blk = jnp.arange(S) // L
    past = jnp.arange(nb)[None, :] < q_blk[:, None]
    gate = jnp.where(past, gate, NEG_INF)
    n_sel = max(1, min(MOBA_TOPK, nb - 1))
    _, sel = lax.top_k(gate, n_sel)
    sel_valid = jnp.arange(n_sel)[None, :] < q_blk[:, None]
    gather = jax.vmap(jax.vmap(lambda blocks, idx: blocks[idx]))
    scale = D ** -0.5
    nc = S // MOBA_Q_CHUNK

    def chunk(c):
        t0 = c * MOBA_Q_CHUNK
        qc = lax.dynamic_slice_in_dim(q, t0, MOBA_Q_CHUNK, axis=2)
        idx = lax.dynamic_slice_in_dim(sel, t0, MOBA_Q_CHUNK, axis=2)
        valid = lax.dynamic_slice_in_dim(sel_valid, t0, MOBA_Q_CHUNK, axis=0)
        kg = gather(kb, idx)
        vg = gather(vb, idx)
        own0 = (t0 // L) * L
        k_own = lax.dynamic_slice_in_dim(kp, own0, L, axis=2)
        v_own = lax.dynamic_slice_in_dim(vp, own0, L, axis=2)
        s_sel = jnp.einsum('bhqd,bhqkld->bhqkl', qc, kg).astype(jnp.float32) * scale
        s_sel = jnp.where(valid[None, None, :, :, None], s_sel, NEG_INF)
        s_sel = s_sel.reshape(B, H, MOBA_Q_CHUNK, n_sel * L)
        s_own = jnp.einsum('bhqd,bhld->bhql', qc, k_own).astype(jnp.float32) * scale
        qpos = t0 + jnp.arange(MOBA_Q_CHUNK)
        kpos = own0 + jnp.arange(L)
        s_own = jnp.where(kpos[None, :] <= qpos[:, None], s_own, NEG_INF)
        p = jax.nn.softmax(jnp.concatenate([s_sel, s_own], axis=-1), axis=-1).astype(v.dtype)
        p_sel = p[..., :n_sel * L].reshape(B, H, MOBA_Q_CHUNK, n_sel, L)
        p_own = p[..., n_sel * L:]
        return (jnp.einsum('bhqkl,bhqkld->bhqd', p_sel, vg)
                + jnp.einsum('bhql,bhld->bhqd', p_own, v_own))

    out = lax.map(chunk, jnp.arange(nc))
    return out.transpose(1, 0, 3, 2, 4).reshape(B, S, H * D)


def setup_inputs(seed: int = 0) -> dict:
    key = jax.random.key(seed)
    ks = jax.random.split(key, 24)

    def w(k, fan_in, fan_out):
        return jax.random.normal(k, (DEPTH, fan_in, fan_out), jnp.float32) * fan_in ** -0.5

    def g(k, n):
        return 1.0 + 0.02 * jax.random.normal(k, (DEPTH, n), jnp.float32)

    return {
        "x": jax.random.normal(ks[0], (BATCH, SEQ, D_MODEL), jnp.float32),
        "ffn1_pre_g": g(ks[1], D_MODEL),
        "ffn1_w_gate": w(ks[2], D_MODEL, FFN_DIM),
        "ffn1_w_up": w(ks[3], D_MODEL, FFN_DIM),
        "ffn1_w_down": w(ks[4], FFN_DIM, D_MODEL),
        "ffn1_post_g": g(ks[5], D_MODEL),
        "mix_pre_g": g(ks[6], D_MODEL),
        "w_in": w(ks[7], D_MODEL, IN_PROJ_DIM),
        "mla_q_norm_g": g(ks[8], MLA_Q_RANK),
        "mla_kv_norm_g": g(ks[9], MLA_KV_RANK),
        "mla_w_uq": w(ks[10], MLA_Q_RANK, MLA_HEADS * MLA_QK_DIM),
        "mla_w_ukv": w(ks[11], MLA_KV_RANK, MLA_HEADS * (MLA_NOPE_DIM + MLA_V_DIM)),
        "mla_out_g": g(ks[12], MLA_WIDTH),
        "moba_out_g": g(ks[13], MOBA_WIDTH),
        "w_out": w(ks[14], MIX_WIDTH, D_MODEL),
        "mix_post_g": g(ks[15], D_MODEL),
        "ffn2_pre_g": g(ks[16], D_MODEL),
        "ffn2_w_gate": w(ks[17], D_MODEL, FFN_DIM),
        "ffn2_w_up": w(ks[18], D_MODEL, FFN_DIM),
        "ffn2_w_down": w(ks[19], FFN_DIM, D_MODEL),
        "ffn2_post_g": g(ks[20], D_MODEL),
    }


def reference(x, ffn1_pre_g, ffn1_w_gate, ffn1_w_up, ffn1_w_down, ffn1_post_g,
              mix_pre_g, w_in, mla_q_norm_g, mla_kv_norm_g, mla_w_uq, mla_w_ukv,
              mla_out_g, moba_out_g, w_out, mix_post_g,
              ffn2_pre_g, ffn2_w_gate, ffn2_w_up, ffn2_w_down, ffn2_post_g):
    cuts = list(np.cumsum(IN_SPLITS)[:-1])
    for l in range(DEPTH):
        h = swiglu(rms_norm(x, ffn1_pre_g[l]), ffn1_w_gate[l], ffn1_w_up[l], ffn1_w_down[l])
        x = x + 0.5 * rms_norm(h, ffn1_post_g[l])
        u = rms_norm(x, mix_pre_g[l])
        c_q, c_kv, k_rope, q_m, k_m, v_m = jnp.split(u @ w_in[l], cuts, axis=-1)
        a = mla_attention(c_q, c_kv, k_rope, mla_q_norm_g[l], mla_kv_norm_g[l],
                          mla_w_uq[l], mla_w_ukv[l])
        m = moba_attention(q_m, k_m, v_m)
        y = jnp.concatenate([rms_norm(a, mla_out_g[l]), rms_norm(m, moba_out_g[l])], axis=-1) @ w_out[l]
        x = x + rms_norm(y, mix_post_g[l])
        h = swiglu(rms_norm(x, ffn2_pre_g[l]), ffn2_w_gate[l], ffn2_w_up[l], ffn2_w_down[l])
        x = x + 0.5 * rms_norm(h, ffn2_post_g[l])
    return x
```

```python
import functools

import jax
import jax.numpy as jnp
from jax import lax
from jax.experimental import pallas as pl
from jax.experimental.pallas import tpu as pltpu

D_MODEL = 2048
FFN_DIM = 5632
MLA_HEADS = 8
MLA_Q_RANK = 512
MLA_KV_RANK = 256
MLA_NOPE_DIM = 128
MLA_ROPE_DIM = 64
MLA_V_DIM = 128
MLA_QK_DIM = MLA_NOPE_DIM + MLA_ROPE_DIM
MOBA_HEADS = 8
MOBA_HEAD_DIM = 128
MOBA_BLOCK = 256
MOBA_TOPK = 3
MLA_WIDTH = MLA_HEADS * MLA_V_DIM
MOBA_WIDTH = MOBA_HEADS * MOBA_HEAD_DIM
ROPE_THETA = 10000.0
NORM_EPS = 1e-6
NEG_INF = -1e30

LANES = 128
MLA_QK_PAD = 2 * LANES
V7X_VMEM_BYTES = 64 * 2 ** 20
MIB = 2 ** 20

BF16 = jnp.bfloat16
F32 = jnp.float32


def _rms(x, g):
    return x * lax.rsqrt(jnp.mean(x * x, axis=-1, keepdims=True) + NORM_EPS) * g


def _dot(a, b):
    return jnp.dot(a, b, preferred_element_type=F32)


def _dot_nt(a, b):
    return lax.dot_general(a, b, (((1,), (1,)), ((), ())), preferred_element_type=F32)


def _const_spec(shape):
    return pl.BlockSpec(shape, lambda *_: (0,) * len(shape), pipeline_mode=pl.Buffered(1))


def _ffn_kernel(x_ref, pre_g_ref, wg_ref, wu_ref, wd_ref, post_g_ref, o_ref, xn_ref, acc_ref):
    f = pl.program_id(1)

    @pl.when(f == 0)
    def _():
        xn_ref[...] = _rms(x_ref[...], pre_g_ref[...]).astype(BF16)
        acc_ref[...] = jnp.zeros_like(acc_ref)

    xn = xn_ref[...]
    h = _dot(xn, wg_ref[...])
    u = _dot(xn, wu_ref[...])
    a = (h * jax.nn.sigmoid(h)) * u
    acc_ref[...] += _dot(a.astype(BF16), wd_ref[...])

    @pl.when(f == pl.num_programs(1) - 1)
    def _():
        o_ref[...] = x_ref[...] + 0.5 * _rms(acc_ref[...], post_g_ref[...])


def _ffn(x, pre_g, wg, wu, wd, post_g, *, tm=512, tf=512):
    T, D = x.shape
    F = wg.shape[1]
    vmem = (2 * 2 * tm * D * 4
            + tm * D * 2 + tm * D * 4
            + 2 * 3 * D * tf * 2
            + 4 * tm * tf * 4
            + 8 * MIB)
    return pl.pallas_call(
        _ffn_kernel,
        grid=(T // tm, F // tf),
        in_specs=[
            pl.BlockSpec((tm, D), lambda i, f: (i, 0)),
            pl.BlockSpec((1, D), lambda i, f: (0, 0)),
            pl.BlockSpec((D, tf), lambda i, f: (0, f)),
            pl.BlockSpec((D, tf), lambda i, f: (0, f)),
            pl.BlockSpec((tf, D), lambda i, f: (f, 0)),
            pl.BlockSpec((1, D), lambda i, f: (0, 0)),
        ],
        out_specs=pl.BlockSpec((tm, D), lambda i, f: (i, 0)),
        out_shape=jax.ShapeDtypeStruct((T, D), F32),
        scratch_shapes=[pltpu.VMEM((tm, D), BF16), pltpu.VMEM((tm, D), F32)],
        compiler_params=pltpu.CompilerParams(
            dimension_semantics=("parallel", "arbitrary"), vmem_limit_bytes=vmem),
        name="ffn",
    )(x, pre_g, wg, wu, wd, post_g)


def _rope(x, c, s):
    return x * c + pltpu.roll(x, LANES // 2, axis=1) * s


def _proj_kernel(x_ref, g_ref, w_in_ref, gq_ref, gkv_ref, w_uq_ref, w_uk_ref, w_uv_ref,
                 ca_ref, sa_ref, cm_ref, sm_ref,
                 qa_ref, ka_ref, va_ref, qm_ref, km_ref, vm_ref, kmean_ref):
    un = _rms(x_ref[...], g_ref[...]).astype(BF16)
    ca, sa, cm, sm = ca_ref[...], sa_ref[...], cm_ref[...], sm_ref[...]
    o_ckv = MLA_Q_RANK
    o_kpe = o_ckv + MLA_KV_RANK
    o_qm = o_kpe + LANES
    o_km = o_qm + MOBA_WIDTH
    o_vm = o_km + MOBA_WIDTH

    qn = _rms(_dot(un, w_in_ref[:, :o_ckv]), gq_ref[...]).astype(BF16)
    q = _dot(qn, w_uq_ref[...])
    for h in range(MLA_HEADS):
        lo = h * MLA_QK_PAD
        qa_ref[:, lo:lo + LANES] = q[:, lo:lo + LANES].astype(BF16)
        qa_ref[:, lo + LANES:lo + MLA_QK_PAD] = _rope(q[:, lo + LANES:lo + MLA_QK_PAD], ca, sa).astype(BF16)

    kvn = _rms(_dot(un, w_in_ref[:, o_ckv:o_kpe]), gkv_ref[...]).astype(BF16)
    k_nope = _dot(kvn, w_uk_ref[...])
    va_ref[...] = _dot(kvn, w_uv_ref[...]).astype(BF16)
    k_pe = _rope(_dot(un, w_in_ref[:, o_kpe:o_qm]), ca, sa).astype(BF16)
    for h in range(MLA_HEADS):
        lo = h * MLA_QK_PAD
        ka_ref[:, lo:lo + LANES] = k_nope[:, h * LANES:(h + 1) * LANES].astype(BF16)
        ka_ref[:, lo + LANES:lo + MLA_QK_PAD] = k_pe

    qm = _dot(un, w_in_ref[:, o_qm:o_km])
    km = _dot(un, w_in_ref[:, o_km:o_vm])
    vm_ref[...] = _dot(un, w_in_ref[:, o_vm:]).astype(BF16)
    for h in range(MOBA_HEADS):
        sl = slice(h * LANES, (h + 1) * LANES)
        qm_ref[:, sl] = _rope(qm[:, sl], cm, sm).astype(BF16)
        kh = _rope(km[:, sl], cm, sm)
        km_ref[:, sl] = kh.astype(BF16)
        kmean_ref[:, sl] = jnp.mean(kh, axis=0, keepdims=True)


def _proj(x, g, w_in, gq, gkv, w_uq, w_uk, w_uv, ca, sa, cm, sm, *, seq):
    T, D = x.shape
    tm = MOBA_BLOCK
    n_in = w_in.shape[1]
    pos_blocks = seq // tm
    tok = lambda w: pl.BlockSpec((tm, w), lambda i: (i, 0))
    tab = pl.BlockSpec((tm, LANES), lambda i: (i % pos_blocks, 0))
    wa, wm = MLA_HEADS * MLA_QK_PAD, MOBA_WIDTH
    vmem = (2 * tm * D * 4
            + (D * n_in + MLA_Q_RANK * wa + 2 * MLA_KV_RANK * MLA_WIDTH) * 2
            + 2 * tm * (2 * wa + MLA_WIDTH + 3 * wm) * 2
            + 6 * tm * wa * 4
            + 8 * MIB)
    return pl.pallas_call(
        _proj_kernel,
        grid=(T // tm,),
        in_specs=[
            tok(D), _const_spec((1, D)), _const_spec((D, n_in)),
            _const_spec((1, MLA_Q_RANK)), _const_spec((1, MLA_KV_RANK)),
            _const_spec((MLA_Q_RANK, wa)), _const_spec((MLA_KV_RANK, MLA_WIDTH)),
            _const_spec((MLA_KV_RANK, MLA_WIDTH)),
            tab, tab, tab, tab,
        ],
        out_specs=[tok(wa), tok(wa), tok(MLA_WIDTH), tok(wm), tok(wm), tok(wm),
                   pl.BlockSpec((None, 1, wm), lambda i: (i, 0, 0))],
        out_shape=[
            jax.ShapeDtypeStruct((T, wa), BF16), jax.ShapeDtypeStruct((T, wa), BF16),
            jax.ShapeDtypeStruct((T, MLA_WIDTH), BF16),
            jax.ShapeDtypeStruct((T, wm), BF16), jax.ShapeDtypeStruct((T, wm), BF16),
            jax.ShapeDtypeStruct((T, wm), BF16),
            jax.ShapeDtypeStruct((T // tm, 1, wm), F32),
        ],
        compiler_params=pltpu.CompilerParams(
            dimension_semantics=("parallel",), vmem_limit_bytes=vmem),
        name="proj",
    )(x, g, w_in, gq, gkv, w_uq, w_uk, w_uv, ca, sa, cm, sm)


def _softmax_init(s, v, m_ref, l_ref, acc_ref):
    m = jnp.max(s, axis=1, keepdims=True)
    p = jnp.exp(s - m)
    m_ref[...] = m
    l_ref[...] = jnp.sum(p, axis=1, keepdims=True)
    acc_ref[...] = _dot(p.astype(BF16), v)


def _softmax_step(s, v, m_ref, l_ref, acc_ref):
    m_prev = m_ref[...]
    m = jnp.maximum(m_prev, jnp.max(s, axis=1, keepdims=True))
    alpha = jnp.exp(m_prev - m)
    p = jnp.exp(s - m)
    m_ref[...] = m
    l_ref[...] = alpha * l_ref[...] + jnp.sum(p, axis=1, keepdims=True)
    acc_ref[...] = alpha * acc_ref[...] + _dot(p.astype(BF16), v)


def _causal(s):
    row = lax.broadcasted_iota(jnp.int32, s.shape, 0)
    col = lax.broadcasted_iota(jnp.int32, s.shape, 1)
    return jnp.where(col <= row, s, NEG_INF)


def _mla_kernel(q_ref, k_ref, v_ref, o_ref, m_ref, l_ref, acc_ref, *, tq, scale):
    i = pl.program_id(2)
    q = q_ref[...]

    def kv(j):
        rows = pl.ds(pl.multiple_of(j * tq, tq), tq)
        return k_ref[rows, :], v_ref[rows, :]

    k, v = kv(i)
    _softmax_init(_causal(_dot_nt(q, k) * scale), v, m_ref, l_ref, acc_ref)

    def body(j, carry):
        k, v = kv(j)
        _softmax_step(_dot_nt(q, k) * scale, v, m_ref, l_ref, acc_ref)
        return carry

    lax.fori_loop(0, i, body, 0)
    o_ref[...] = acc_ref[...] / l_ref[...]


def _mla_attention(q, k, v, *, tq=256):
    B, S, _ = q.shape
    kern = functools.partial(_mla_kernel, tq=tq, scale=MLA_QK_DIM ** -0.5)
    vmem = (2 * S * (MLA_QK_PAD + MLA_V_DIM) * 2 + 2 * tq * MLA_QK_PAD * 2
            + 2 * tq * MLA_V_DIM * 4 + 3 * tq * LANES * 4 + 8 * tq * tq * 4 + 8 * MIB)
    return pl.pallas_call(
        kern,
        grid=(B, MLA_HEADS, S // tq),
        in_specs=[
            pl.BlockSpec((None, tq, MLA_QK_PAD), lambda b, h, i: (b, i, h)),
            pl.BlockSpec((None, S, MLA_QK_PAD), lambda b, h, i: (b, 0, h)),
            pl.BlockSpec((None, S, MLA_V_DIM), lambda b, h, i: (b, 0, h)),
        ],
        out_specs=pl.BlockSpec((None, tq, MLA_V_DIM), lambda b, h, i: (b, i, h)),
        out_shape=jax.ShapeDtypeStruct((B, S, MLA_WIDTH), F32),
        scratch_shapes=[pltpu.VMEM((tq, 1), F32), pltpu.VMEM((tq, 1), F32),
                        pltpu.VMEM((tq, MLA_V_DIM), F32)],
        compiler_params=pltpu.CompilerParams(
            dimension_semantics=("parallel", "parallel", "arbitrary"), vmem_limit_bytes=vmem),
        name="mla_attention",
    )(q, k, v)


def _moba_kernel(q_ref, k_ref, v_ref, kmean_ref, o_ref, sel_ref, m_ref, l_ref, acc_ref, *, scale):
    L = MOBA_BLOCK
    i = pl.program_id(2)
    q = q_ref[...]
    nb = k_ref.shape[0] // L

    gate = _dot_nt(q, kmean_ref[...].astype(BF16))
    lane = lax.broadcasted_iota(jnp.int32, gate.shape, 1)
    cand = jnp.where(lane < i, gate, -jnp.inf)
    sel = jnp.zeros(gate.shape, F32)
    for _ in range(min(MOBA_TOPK, nb - 1)):
        best = jnp.max(cand, axis=1, keepdims=True)
        first = jnp.min(jnp.where(cand == best, lane, LANES), axis=1, keepdims=True)
        pick = (lane == first) & (best > -jnp.inf)
        sel = jnp.where(pick, 1.0, sel)
        cand = jnp.where(pick, -jnp.inf, cand)
    sel_ref[...] = sel

    def kv(j):
        rows = pl.ds(j * L, L)
        return k_ref[rows, :], v_ref[rows, :]

    k, v = kv(i)
    _softmax_init(_causal(_dot_nt(q, k) * scale), v, m_ref, l_ref, acc_ref)

    for n in range(nb - 1):
        @pl.when(n < i)
        def _():
            k, v = k_ref[n * L:(n + 1) * L, :], v_ref[n * L:(n + 1) * L, :]
            s = jnp.where(sel_ref[:, n:n + 1] > 0.0, _dot_nt(q, k) * scale, NEG_INF)
            _softmax_step(s, v, m_ref, l_ref, acc_ref)

    o_ref[...] = acc_ref[...] / l_ref[...]


def _moba_attention(q, k, v, kmean):
    B, S, _ = q.shape
    L, Dh = MOBA_BLOCK, MOBA_HEAD_DIM
    kern = functools.partial(_moba_kernel, scale=Dh ** -0.5)
    vmem = (2 * 2 * S * Dh * 2 + 2 * L * Dh * 2 + 2 * L * Dh * 4 + 2 * LANES * Dh * 4
            + 4 * L * LANES * 4 + 8 * L * L * 4 + 8 * MIB)
    return pl.pallas_call(
        kern,
        grid=(B, MOBA_HEADS, S // L),
        in_specs=[
            pl.BlockSpec((None, L, Dh), lambda b, h, i: (b, i, h)),
            pl.BlockSpec((None, S, Dh), lambda b, h, i: (b, 0, h)),
            pl.BlockSpec((None, S, Dh), lambda b, h, i: (b, 0, h)),
            pl.BlockSpec((None, LANES, Dh), lambda b, h, i: (b, 0, h)),
        ],
        out_specs=pl.BlockSpec((None, L, Dh), lambda b, h, i: (b, i, h)),
        out_shape=jax.ShapeDtypeStruct((B, S, MOBA_WIDTH), F32),
        scratch_shapes=[pltpu.VMEM((L, LANES), F32), pltpu.VMEM((L, 1), F32),
                        pltpu.VMEM((L, 1), F32), pltpu.VMEM((L, Dh), F32)],
        compiler_params=pltpu.CompilerParams(
            dimension_semantics=("parallel", "parallel", "arbitrary"), vmem_limit_bytes=vmem),
        name="moba_attention",
    )(q, k, v, kmean)


def _out_kernel(x_ref, a_ref, m_ref, ga_ref, gm_ref, w_ref, gp_ref, o_ref):
    an = _rms(a_ref[...], ga_ref[...]).astype(BF16)
    mn = _rms(m_ref[...], gm_ref[...]).astype(BF16)
    y = _dot(an, w_ref[:MLA_WIDTH, :]) + _dot(mn, w_ref[MLA_WIDTH:, :])
    o_ref[...] = x_ref[...] + _rms(y, gp_ref[...])


def _out_proj(x, a, m, ga, gm, w, gp, *, tm=512):
    T, D = x.shape
    tok = lambda wd: pl.BlockSpec((tm, wd), lambda i: (i, 0))
    vmem = (2 * 2 * tm * D * 4 + 2 * tm * (MLA_WIDTH + MOBA_WIDTH) * 4
            + (MLA_WIDTH + MOBA_WIDTH) * D * 2 + 4 * tm * D * 4 + 8 * MIB)
    return pl.pallas_call(
        _out_kernel,
        grid=(T // tm,),
        in_specs=[tok(D), tok(MLA_WIDTH), tok(MOBA_WIDTH),
                  _const_spec((1, MLA_WIDTH)), _const_spec((1, MOBA_WIDTH)),
                  _const_spec((MLA_WIDTH + MOBA_WIDTH, D)), _const_spec((1, D))],
        out_specs=tok(D),
        out_shape=jax.ShapeDtypeStruct((T, D), F32),
        compiler_params=pltpu.CompilerParams(
            dimension_semantics=("parallel",), vmem_limit_bytes=vmem),
        name="out_proj",
    )(x, a, m, ga, gm, w, gp)


def _spread_rope_cols(w):
    half = MLA_ROPE_DIM // 2
    z = jnp.zeros(w.shape[:-1] + (LANES // 2 - half,), w.dtype)
    return jnp.concatenate([w[..., :half], z, w[..., half:], z], axis=-1)


def _rope_tables(seq):
    pos = jnp.arange(seq, dtype=F32)[:, None]

    def cos_sin(dim):
        inv = 1.0 / (ROPE_THETA ** (jnp.arange(0, dim, 2, dtype=F32) / dim))
        ang = pos * inv[None, :]
        return jnp.cos(ang), jnp.sin(ang)

    c, s = cos_sin(MOBA_HEAD_DIM)
    cm = jnp.concatenate([c, c], axis=-1)
    sm = jnp.concatenate([-s, s], axis=-1)
    c, s = cos_sin(MLA_ROPE_DIM)
    z = jnp.zeros_like(c)
    ca = jnp.concatenate([c, z, c, z], axis=-1)
    sa = jnp.concatenate([-s, z, s, z], axis=-1)
    return ca, sa, cm, sm


def kernel(x, ffn1_pre_g, ffn1_w_gate, ffn1_w_up, ffn1_w_down, ffn1_post_g, mix_pre_g, w_in, mla_q_norm_g, mla_kv_norm_g, mla_w_uq, mla_w_ukv, mla_out_g, moba_out_g, w_out, mix_post_g, ffn2_pre_g, ffn2_w_gate, ffn2_w_up, ffn2_w_down, ffn2_post_g):
    B, S, D = x.shape
    depth = w_in.shape[0]
    H = MLA_HEADS
    ca, sa, cm, sm = _rope_tables(S)
    xt = x.reshape(B * S, D)
    for l in range(depth):
        o_kpe = MLA_Q_RANK + MLA_KV_RANK
        w_in_l = jnp.concatenate(
            [w_in[l][:, :o_kpe], _spread_rope_cols(w_in[l][:, o_kpe:o_kpe + MLA_ROPE_DIM]),
             w_in[l][:, o_kpe + MLA_ROPE_DIM:]], axis=1).astype(BF16)
        uq = mla_w_uq[l].reshape(MLA_Q_RANK, H, MLA_QK_DIM)
        w_uq = jnp.concatenate(
            [uq[..., :MLA_NOPE_DIM], _spread_rope_cols(uq[..., MLA_NOPE_DIM:])], axis=-1
        ).reshape(MLA_Q_RANK, H * MLA_QK_PAD).astype(BF16)
        ukv = mla_w_ukv[l].reshape(MLA_KV_RANK, H, MLA_NOPE_DIM + MLA_V_DIM)
        w_uk = ukv[..., :MLA_NOPE_DIM].reshape(MLA_KV_RANK, MLA_WIDTH).astype(BF16)
        w_uv = ukv[..., MLA_NOPE_DIM:].reshape(MLA_KV_RANK, MLA_WIDTH).astype(BF16)

        xt = _ffn(xt, ffn1_pre_g[l][None], ffn1_w_gate[l].astype(BF16), ffn1_w_up[l].astype(BF16),
                  ffn1_w_down[l].astype(BF16), ffn1_post_g[l][None])
        qa, ka, va, qm, km, vm, kmean = _proj(
            xt, mix_pre_g[l][None], w_in_l, mla_q_norm_g[l][None], mla_kv_norm_g[l][None],
            w_uq, w_uk, w_uv, ca, sa, cm, sm, seq=S)
        a = _mla_attention(qa.reshape(B, S, -1), ka.reshape(B, S, -1), va.reshape(B, S, -1))
        nb = S // MOBA_BLOCK
        kmean = jnp.pad(kmean.reshape(B, nb, MOBA_WIDTH), ((0, 0), (0, LANES - nb), (0, 0)))
        m = _moba_attention(qm.reshape(B, S, -1), km.reshape(B, S, -1), vm.reshape(B, S, -1), kmean)
        xt = _out_proj(xt, a.reshape(B * S, -1), m.reshape(B * S, -1), mla_out_g[l][None],
                       moba_out_g[l][None], w_out[l].astype(BF16), mix_post_g[l][None])
        xt = _ffn(xt, ffn2_pre_g[l][None], ffn2_w_gate[l].astype(BF16), ffn2_w_up[l].astype(BF16),
                  ffn2_w_down[l].astype(BF16), ffn2_post_g[l][None])
    return xt.reshape(B, S, D)
```

```python
import functools

import jax
import jax.numpy as jnp
from jax import lax
from jax.experimental import pallas as pl
from jax.experimental.pallas import tpu as pltpu

D_MODEL = 2048
FFN_DIM = 5632
MLA_HEADS = 8
MLA_Q_RANK = 512
MLA_KV_RANK = 256
MLA_NOPE_DIM = 128
MLA_ROPE_DIM = 64
MLA_V_DIM = 128
MLA_QK_DIM = MLA_NOPE_DIM + MLA_ROPE_DIM
MOBA_HEADS = 8
MOBA_HEAD_DIM = 128
MOBA_BLOCK = 256
MOBA_TOPK = 3
MLA_WIDTH = MLA_HEADS * MLA_V_DIM
MOBA_WIDTH = MOBA_HEADS * MOBA_HEAD_DIM
ROPE_THETA = 10000.0
NORM_EPS = 1e-6
NEG_INF = -1e30
MASK_BIAS = -(2.0 ** 100)

LANES = 128
SUBLANES_BF16 = 16
HEADS = MLA_HEADS + MOBA_HEADS
QK_PAD = 2 * LANES
V_DIM = 128
MIB = 2 ** 20

BF16 = jnp.bfloat16
F32 = jnp.float32


def _rms(x, g):
    return x * lax.rsqrt(jnp.mean(x * x, axis=-1, keepdims=True) + NORM_EPS) * g


def _dot(a, b):
    return jnp.dot(a, b, preferred_element_type=F32)


def _dot_nt(a, b):
    return lax.dot_general(a, b, (((1,), (1,)), ((), ())), preferred_element_type=F32)


def _const_spec(shape):
    return pl.BlockSpec(shape, lambda *_: (0,) * len(shape), pipeline_mode=pl.Buffered(1))


def _ffn_kernel(x_ref, pre_g_ref, wg_ref, wu_ref, wd_ref, post_g_ref, o_ref, xn_ref, acc_ref):
    f = pl.program_id(1)

    @pl.when(f == 0)
    def _():
        xn_ref[...] = _rms(x_ref[...], pre_g_ref[...]).astype(BF16)
        acc_ref[...] = jnp.zeros_like(acc_ref)

    xn = xn_ref[...]
    h = _dot(xn, wg_ref[...])
    u = _dot(xn, wu_ref[...])
    a = (h * jax.nn.sigmoid(h)) * u
    acc_ref[...] += _dot(a.astype(BF16), wd_ref[...])

    @pl.when(f == pl.num_programs(1) - 1)
    def _():
        o_ref[...] = x_ref[...] + 0.5 * _rms(acc_ref[...], post_g_ref[...])


def _ffn(x, pre_g, wg, wu, wd, post_g, *, tm=512, tf=512):
    T, D = x.shape
    F = wg.shape[1]
    vmem = (2 * 2 * tm * D * 4
            + tm * D * 2 + tm * D * 4
            + 2 * 3 * D * tf * 2
            + 4 * tm * tf * 4
            + 8 * MIB)
    return pl.pallas_call(
        _ffn_kernel,
        grid=(T // tm, F // tf),
        in_specs=[
            pl.BlockSpec((tm, D), lambda i, f: (i, 0)),
            pl.BlockSpec((1, D), lambda i, f: (0, 0)),
            pl.BlockSpec((D, tf), lambda i, f: (0, f)),
            pl.BlockSpec((D, tf), lambda i, f: (0, f)),
            pl.BlockSpec((tf, D), lambda i, f: (f, 0)),
            pl.BlockSpec((1, D), lambda i, f: (0, 0)),
        ],
        out_specs=pl.BlockSpec((tm, D), lambda i, f: (i, 0)),
        out_shape=jax.ShapeDtypeStruct((T, D), F32),
        scratch_shapes=[pltpu.VMEM((tm, D), BF16), pltpu.VMEM((tm, D), F32)],
        compiler_params=pltpu.CompilerParams(
            dimension_semantics=("parallel", "arbitrary"), vmem_limit_bytes=vmem),
        name="ffn",
    )(x, pre_g, wg, wu, wd, post_g)


def _rope(x, c, s):
    return x * c + pltpu.roll(x, LANES // 2, axis=1) * s


def _moba_bias(kmean, qh, blk):
    nb = kmean.shape[0]
    gate = _dot_nt(kmean.astype(BF16), qh)
    n = lax.broadcasted_iota(jnp.int32, gate.shape, 0)
    cand = jnp.where(n < blk, gate, -jnp.inf)
    allowed = n == blk
    for _ in range(min(MOBA_TOPK, nb - 1)):
        best = jnp.max(cand, axis=0, keepdims=True)
        first = jnp.min(jnp.where(cand == best, n, nb), axis=0, keepdims=True)
        pick = (n == first) & (best > -jnp.inf)
        allowed = allowed | pick
        cand = jnp.where(pick, -jnp.inf, cand)
    bias = jnp.where(allowed, 0.0, MASK_BIAS)
    bias = jnp.concatenate([bias, jnp.zeros((LANES - nb, bias.shape[1]), F32)], axis=0)
    return bias.T


def _proj_kernel(x_ref, g_ref, w_in_ref, gq_ref, gkv_ref, w_uq_ref, w_uk_ref, w_uv_ref,
                 ca_ref, sa_ref, cm_ref, sm_ref, q_ref, k_ref, v_ref, kmean_ref):
    nb = kmean_ref.shape[0]
    blk = pl.program_id(0) % nb

    @pl.when(pl.program_id(0) == 0)
    def _():
        kmean_ref[...] = jnp.zeros_like(kmean_ref)

    un = _rms(x_ref[...], g_ref[...]).astype(BF16)
    ca, sa, cm, sm = ca_ref[...], sa_ref[...], cm_ref[...], sm_ref[...]
    o_ckv = MLA_Q_RANK
    o_kpe = o_ckv + MLA_KV_RANK
    o_qm = o_kpe + LANES
    o_km = o_qm + MOBA_WIDTH
    o_vm = o_km + MOBA_WIDTH

    qn = _rms(_dot(un, w_in_ref[:, :o_ckv]), gq_ref[...]).astype(BF16)
    q = _dot(qn, w_uq_ref[...])
    for h in range(MLA_HEADS):
        lo = h * QK_PAD
        q_ref[:, lo:lo + LANES] = q[:, lo:lo + LANES].astype(BF16)
        q_ref[:, lo + LANES:lo + QK_PAD] = _rope(q[:, lo + LANES:lo + QK_PAD], ca, sa).astype(BF16)

    kvn = _rms(_dot(un, w_in_ref[:, o_ckv:o_kpe]), gkv_ref[...]).astype(BF16)
    k_nope = _dot(kvn, w_uk_ref[...])
    v_ref[:, :MLA_WIDTH] = _dot(kvn, w_uv_ref[...]).astype(BF16)
    k_pe = _rope(_dot(un, w_in_ref[:, o_kpe:o_qm]), ca, sa).astype(BF16)
    for h in range(MLA_HEADS):
        lo = h * QK_PAD
        k_ref[:, lo:lo + LANES] = k_nope[:, h * LANES:(h + 1) * LANES].astype(BF16)
        k_ref[:, lo + LANES:lo + QK_PAD] = k_pe

    qm = _dot(un, w_in_ref[:, o_qm:o_km])
    km = _dot(un, w_in_ref[:, o_km:o_vm])
    v_ref[:, MLA_WIDTH:] = _dot(un, w_in_ref[:, o_vm:]).astype(BF16)
    lane = lax.broadcasted_iota(jnp.int32, (x_ref.shape[0], LANES), 1)
    one_hot = jnp.where(lane == blk, 1.0, 0.0).astype(BF16)
    blk_row = lax.broadcasted_iota(jnp.int32, (nb, LANES), 0)
    for h in range(MOBA_HEADS):
        sl = slice(h * LANES, (h + 1) * LANES)
        lo = (MLA_HEADS + h) * QK_PAD
        qh = _rope(qm[:, sl], cm, sm).astype(BF16)
        kh = _rope(km[:, sl], cm, sm)
        kmean = jnp.where(blk_row == blk, jnp.mean(kh, axis=0, keepdims=True), kmean_ref[:, sl])
        kmean_ref[:, sl] = kmean
        q_ref[:, lo:lo + LANES] = qh
        q_ref[:, lo + LANES:lo + QK_PAD] = _moba_bias(kmean, qh, blk).astype(BF16)
        k_ref[:, lo:lo + LANES] = kh.astype(BF16)
        k_ref[:, lo + LANES:lo + QK_PAD] = one_hot


def _proj(x, g, w_in, gq, gkv, w_uq, w_uk, w_uv, ca, sa, cm, sm, *, seq):
    T, D = x.shape
    tm = MOBA_BLOCK
    n_in = w_in.shape[1]
    nb = seq // tm
    assert seq % tm == 0 and nb % SUBLANES_BF16 == 0 and nb <= LANES
    tok = lambda w: pl.BlockSpec((tm, w), lambda i: (i, 0))
    tab = pl.BlockSpec((tm, LANES), lambda i: (i % nb, 0))
    wa = MLA_HEADS * QK_PAD
    vmem = (2 * tm * D * 4
            + (D * n_in + MLA_Q_RANK * wa + 2 * MLA_KV_RANK * MLA_WIDTH) * 2
            + 2 * tm * (2 * HEADS * QK_PAD + HEADS * V_DIM) * 2
            + 6 * tm * wa * 4
            + 8 * MIB)
    return pl.pallas_call(
        _proj_kernel,
        grid=(T // tm,),
        in_specs=[
            tok(D), _const_spec((1, D)), _const_spec((D, n_in)),
            _const_spec((1, MLA_Q_RANK)), _const_spec((1, MLA_KV_RANK)),
            _const_spec((MLA_Q_RANK, wa)), _const_spec((MLA_KV_RANK, MLA_WIDTH)),
            _const_spec((MLA_KV_RANK, MLA_WIDTH)),
            tab, tab, tab, tab,
        ],
        out_specs=[tok(HEADS * QK_PAD), tok(HEADS * QK_PAD), tok(HEADS * V_DIM)],
        out_shape=[
            jax.ShapeDtypeStruct((T, HEADS * QK_PAD), BF16),
            jax.ShapeDtypeStruct((T, HEADS * QK_PAD), BF16),
            jax.ShapeDtypeStruct((T, HEADS * V_DIM), BF16),
        ],
        scratch_shapes=[pltpu.VMEM((nb, MOBA_WIDTH), F32)],
        compiler_params=pltpu.CompilerParams(
            dimension_semantics=("arbitrary",), vmem_limit_bytes=vmem),
        name="proj",
    )(x, g, w_in, gq, gkv, w_uq, w_uk, w_uv, ca, sa, cm, sm)


def _attn_kernel(q_ref, k_ref, v_ref, o_ref, m_ref, acc_ref, *, t):
    h = pl.program_id(1)
    i = pl.program_id(2)
    scale = jnp.where(h < MLA_HEADS, MLA_QK_DIM ** -0.5, MOBA_HEAD_DIM ** -0.5).astype(F32)
    q = q_ref[...]
    ones = jnp.ones((t, LANES), BF16)

    def block(j, diagonal):
        rows = pl.ds(pl.multiple_of(j * t, t), t)
        s = _dot_nt(q, k_ref[rows, :]) * scale
        if diagonal:
            row = lax.broadcasted_iota(jnp.int32, s.shape, 0)
            col = lax.broadcasted_iota(jnp.int32, s.shape, 1)
            s = jnp.where(col <= row, s, NEG_INF)
        chunks = [s[:, c * LANES:(c + 1) * LANES] for c in range(t // LANES)]
        m_cur = jnp.max(functools.reduce(jnp.maximum, chunks), axis=1, keepdims=True)
        m_cur = jnp.broadcast_to(m_cur, (t, LANES))
        if diagonal:
            m_new = m_cur
        else:
            m_prev = m_ref[...]
            m_new = jnp.maximum(m_prev, m_cur)
            alpha = jnp.exp(m_prev - m_new)
        p = jnp.concatenate([jnp.exp(c - m_new).astype(BF16) for c in chunks], axis=1)
        pv = _dot(p, jnp.concatenate([v_ref[rows, :], ones], axis=1))
        if diagonal:
            acc_ref[...] = pv
        else:
            acc_ref[...] = jnp.concatenate([alpha, alpha], axis=1) * acc_ref[...] + pv
        m_ref[...] = m_new

    block(i, True)

    def body(j, carry):
        block(j, False)
        return carry

    lax.fori_loop(0, i, body, 0)
    acc = acc_ref[...]
    o_ref[...] = acc[:, :V_DIM] / acc[:, V_DIM:]


def _attention(q, k, v, *, t=512):
    B, S, _ = q.shape
    assert S % t == 0 and t % MOBA_BLOCK == 0
    kern = functools.partial(_attn_kernel, t=t)
    vmem = (2 * S * (QK_PAD + V_DIM) * 2 + 2 * t * QK_PAD * 2 + 2 * t * V_DIM * 4
            + t * (LANES + 2 * V_DIM) * 4 + 6 * t * t * 4 + 8 * MIB)
    return pl.pallas_call(
        kern,
        grid=(B, HEADS, S // t),
        in_specs=[
            pl.BlockSpec((None, t, QK_PAD), lambda b, h, i: (b, i, h)),
            pl.BlockSpec((None, S, QK_PAD), lambda b, h, i: (b, 0, h)),
            pl.BlockSpec((None, S, V_DIM), lambda b, h, i: (b, 0, h)),
        ],
        out_specs=pl.BlockSpec((None, t, V_DIM), lambda b, h, i: (b, i, h)),
        out_shape=jax.ShapeDtypeStruct((B, S, HEADS * V_DIM), F32),
        scratch_shapes=[pltpu.VMEM((t, LANES), F32), pltpu.VMEM((t, 2 * V_DIM), F32)],
        compiler_params=pltpu.CompilerParams(
            dimension_semantics=("parallel", "parallel", "arbitrary"), vmem_limit_bytes=vmem),
        name="attention",
    )(q, k, v)


def _out_kernel(x_ref, am_ref, ga_ref, gm_ref, w_ref, gp_ref, o_ref):
    an = _rms(am_ref[:, :MLA_WIDTH], ga_ref[...]).astype(BF16)
    mn = _rms(am_ref[:, MLA_WIDTH:], gm_ref[...]).astype(BF16)
    y = _dot(jnp.concatenate([an, mn], axis=1), w_ref[...])
    o_ref[...] = x_ref[...] + _rms(y, gp_ref[...])


def _out_proj(x, am, ga, gm, w, gp, *, tm=512):
    T, D = x.shape
    W = MLA_WIDTH + MOBA_WIDTH
    tok = lambda wd: pl.BlockSpec((tm, wd), lambda i: (i, 0))
    vmem = 2 * 2 * tm * D * 4 + 2 * tm * W * 4 + W * D * 2 + 4 * tm * D * 4 + 8 * MIB
    return pl.pallas_call(
        _out_kernel,
        grid=(T // tm,),
        in_specs=[tok(D), tok(W), _const_spec((1, MLA_WIDTH)), _const_spec((1, MOBA_WIDTH)),
                  _const_spec((W, D)), _const_spec((1, D))],
        out_specs=tok(D),
        out_shape=jax.ShapeDtypeStruct((T, D), F32),
        compiler_params=pltpu.CompilerParams(
            dimension_semantics=("parallel",), vmem_limit_bytes=vmem),
        name="out_proj",
    )(x, am, ga, gm, w, gp)


def _spread_rope_cols(w):
    half = MLA_ROPE_DIM // 2
    z = jnp.zeros(w.shape[:-1] + (LANES // 2 - half,), w.dtype)
    return jnp.concatenate([w[..., :half], z, w[..., half:], z], axis=-1)


def _rope_tables(seq):
    pos = jnp.arange(seq, dtype=F32)[:, None]

    def cos_sin(dim):
        inv = 1.0 / (ROPE_THETA ** (jnp.arange(0, dim, 2, dtype=F32) / dim))
        ang = pos * inv[None, :]
        return jnp.cos(ang), jnp.sin(ang)

    c, s = cos_sin(MOBA_HEAD_DIM)
    cm = jnp.concatenate([c, c], axis=-1)
    sm = jnp.concatenate([-s, s], axis=-1)
    c, s = cos_sin(MLA_ROPE_DIM)
    z = jnp.zeros_like(c)
    ca = jnp.concatenate([c, z, c, z], axis=-1)
    sa = jnp.concatenate([-s, z, s, z], axis=-1)
    return ca, sa, cm, sm


def kernel(x, ffn1_pre_g, ffn1_w_gate, ffn1_w_up, ffn1_w_down, ffn1_post_g, mix_pre_g, w_in, mla_q_norm_g, mla_kv_norm_g, mla_w_uq, mla_w_ukv, mla_out_g, moba_out_g, w_out, mix_post_g, ffn2_pre_g, ffn2_w_gate, ffn2_w_up, ffn2_w_down, ffn2_post_g):
    B, S, D = x.shape
    depth = w_in.shape[0]
    H = MLA_HEADS
    ca, sa, cm, sm = _rope_tables(S)
    xt = x.reshape(B * S, D)
    for l in range(depth):
        o_kpe = MLA_Q_RANK + MLA_KV_RANK
        w_in_l = jnp.concatenate(
            [w_in[l][:, :o_kpe], _spread_rope_cols(w_in[l][:, o_kpe:o_kpe + MLA_ROPE_DIM]),
             w_in[l][:, o_kpe + MLA_ROPE_DIM:]], axis=1).astype(BF16)
        uq = mla_w_uq[l].reshape(MLA_Q_RANK, H, MLA_QK_DIM)
        w_uq = jnp.concatenate(
            [uq[..., :MLA_NOPE_DIM], _spread_rope_cols(uq[..., MLA_NOPE_DIM:])], axis=-1
        ).reshape(MLA_Q_RANK, H * QK_PAD).astype(BF16)
        ukv = mla_w_ukv[l].reshape(MLA_KV_RANK, H, MLA_NOPE_DIM + MLA_V_DIM)
        w_uk = ukv[..., :MLA_NOPE_DIM].reshape(MLA_KV_RANK, MLA_WIDTH).astype(BF16)
        w_uv = ukv[..., MLA_NOPE_DIM:].reshape(MLA_KV_RANK, MLA_WIDTH).astype(BF16)

        xt = _ffn(xt, ffn1_pre_g[l][None], ffn1_w_gate[l].astype(BF16), ffn1_w_up[l].astype(BF16),
                  ffn1_w_down[l].astype(BF16), ffn1_post_g[l][None])
        q, k, v = _proj(xt, mix_pre_g[l][None], w_in_l, mla_q_norm_g[l][None],
                        mla_kv_norm_g[l][None], w_uq, w_uk, w_uv, ca, sa, cm, sm, seq=S)
        am = _attention(q.reshape(B, S, -1), k.reshape(B, S, -1), v.reshape(B, S, -1))
        xt = _out_proj(xt, am.reshape(B * S, -1), mla_out_g[l][None], moba_out_g[l][None],
                       w_out[l].astype(BF16), mix_post_g[l][None])
        xt = _ffn(xt, ffn2_pre_g[l][None], ffn2_w_gate[l].astype(BF16), ffn2_w_up[l].astype(BF16),
                  ffn2_w_down[l].astype(BF16), ffn2_post_g[l][None])
    return xt.reshape(B, S, D)
```

```python
import functools

import jax
import jax.numpy as jnp
from jax import lax
from jax.experimental import pallas as pl
from jax.experimental.pallas import tpu as pltpu

D_MODEL = 2048
FFN_DIM = 5632
MLA_HEADS = 8
MLA_Q_RANK = 512
MLA_KV_RANK = 256
MLA_NOPE_DIM = 128
MLA_ROPE_DIM = 64
MLA_V_DIM = 128
MLA_QK_DIM = MLA_NOPE_DIM + MLA_ROPE_DIM
MOBA_HEADS = 8
MOBA_HEAD_DIM = 128
MOBA_BLOCK = 256
MOBA_TOPK = 3
MLA_WIDTH = MLA_HEADS * MLA_V_DIM
MOBA_WIDTH = MOBA_HEADS * MOBA_HEAD_DIM
ROPE_THETA = 10000.0
NORM_EPS = 1e-6
NEG_INF = -1e30
MASK_BIAS = -(2.0 ** 100)

LANES = 128
SUBLANES_BF16 = 16
HEADS = MLA_HEADS + MOBA_HEADS
QK_PAD = 2 * LANES
V_DIM = 128
MIB = 2 ** 20

BF16 = jnp.bfloat16
F32 = jnp.float32


def _rms(x, g):
    return x * lax.rsqrt(jnp.mean(x * x, axis=-1, keepdims=True) + NORM_EPS) * g


def _dot(a, b):
    return jnp.dot(a, b, preferred_element_type=F32)


def _dot_nt(a, b):
    return lax.dot_general(a, b, (((1,), (1,)), ((), ())), preferred_element_type=F32)


def _const_spec(shape):
    return pl.BlockSpec(shape, lambda *_: (0,) * len(shape), pipeline_mode=pl.Buffered(1))


def _ffn_kernel(x_ref, pre_g_ref, wg_ref, wu_ref, wd_ref, post_g_ref, o_ref, xn_ref, acc_ref):
    f = pl.program_id(1)

    @pl.when(f == 0)
    def _():
        xn_ref[...] = _rms(x_ref[...], pre_g_ref[...]).astype(BF16)
        acc_ref[...] = jnp.zeros_like(acc_ref)

    xn = xn_ref[...]
    h = _dot(xn, wg_ref[...])
    u = _dot(xn, wu_ref[...])
    a = (h * jax.nn.sigmoid(h)) * u
    acc_ref[...] += _dot(a.astype(BF16), wd_ref[...])

    @pl.when(f == pl.num_programs(1) - 1)
    def _():
        o_ref[...] = x_ref[...] + 0.5 * _rms(acc_ref[...], post_g_ref[...])


def _ffn(x, pre_g, wg, wu, wd, post_g, *, tm=512, tf=512):
    T, D = x.shape
    F = wg.shape[1]
    vmem = (2 * 2 * tm * D * 4
            + tm * D * 2 + tm * D * 4
            + 2 * 3 * D * tf * 2
            + 4 * tm * tf * 4
            + 8 * MIB)
    return pl.pallas_call(
        _ffn_kernel,
        grid=(T // tm, F // tf),
        in_specs=[
            pl.BlockSpec((tm, D), lambda i, f: (i, 0)),
            pl.BlockSpec((1, D), lambda i, f: (0, 0)),
            pl.BlockSpec((D, tf), lambda i, f: (0, f)),
            pl.BlockSpec((D, tf), lambda i, f: (0, f)),
            pl.BlockSpec((tf, D), lambda i, f: (f, 0)),
            pl.BlockSpec((1, D), lambda i, f: (0, 0)),
        ],
        out_specs=pl.BlockSpec((tm, D), lambda i, f: (i, 0)),
        out_shape=jax.ShapeDtypeStruct((T, D), F32),
        scratch_shapes=[pltpu.VMEM((tm, D), BF16), pltpu.VMEM((tm, D), F32)],
        compiler_params=pltpu.CompilerParams(
            dimension_semantics=("parallel", "arbitrary"), vmem_limit_bytes=vmem),
        name="ffn",
    )(x, pre_g, wg, wu, wd, post_g)


def _rope(x, c, s):
    return x * c + pltpu.roll(x, LANES // 2, axis=1) * s


def _moba_bias(kmean, qh, blk):
    nb = kmean.shape[0]
    gate = _dot_nt(kmean.astype(BF16), qh)
    n = lax.broadcasted_iota(jnp.int32, gate.shape, 0)
    cand = jnp.where(n < blk, gate, -jnp.inf)
    allowed = n == blk
    for _ in range(min(MOBA_TOPK, nb - 1)):
        best = jnp.max(cand, axis=0, keepdims=True)
        first = jnp.min(jnp.where(cand == best, n, nb), axis=0, keepdims=True)
        pick = (n == first) & (best > -jnp.inf)
        allowed = allowed | pick
        cand = jnp.where(pick, -jnp.inf, cand)
    bias = jnp.where(allowed, 0.0, MASK_BIAS)
    bias = jnp.concatenate([bias, jnp.zeros((LANES - nb, bias.shape[1]), F32)], axis=0)
    return bias.T


def _proj_kernel(x_ref, g_ref, w_in_ref, gq_ref, gkv_ref, w_uq_ref, w_uk_ref, w_uv_ref,
                 ca_ref, sa_ref, cm_ref, sm_ref, q_ref, k_ref, v_ref, kmean_ref):
    nb = kmean_ref.shape[0]
    blk = pl.program_id(0) % nb

    @pl.when(pl.program_id(0) == 0)
    def _():
        kmean_ref[...] = jnp.zeros_like(kmean_ref)

    un = _rms(x_ref[...], g_ref[...]).astype(BF16)
    ca, sa, cm, sm = ca_ref[...], sa_ref[...], cm_ref[...], sm_ref[...]
    o_ckv = MLA_Q_RANK
    o_kpe = o_ckv + MLA_KV_RANK
    o_qm = o_kpe + LANES
    o_km = o_qm + MOBA_WIDTH
    o_vm = o_km + MOBA_WIDTH

    qn = _rms(_dot(un, w_in_ref[:, :o_ckv]), gq_ref[...]).astype(BF16)
    q = _dot(qn, w_uq_ref[...])
    for h in range(MLA_HEADS):
        lo = h * QK_PAD
        q_ref[:, lo:lo + LANES] = q[:, lo:lo + LANES].astype(BF16)
        q_ref[:, lo + LANES:lo + QK_PAD] = _rope(q[:, lo + LANES:lo + QK_PAD], ca, sa).astype(BF16)

    kvn = _rms(_dot(un, w_in_ref[:, o_ckv:o_kpe]), gkv_ref[...]).astype(BF16)
    k_nope = _dot(kvn, w_uk_ref[...])
    v_ref[:, :MLA_WIDTH] = _dot(kvn, w_uv_ref[...]).astype(BF16)
    k_pe = _rope(_dot(un, w_in_ref[:, o_kpe:o_qm]), ca, sa).astype(BF16)
    for h in range(MLA_HEADS):
        lo = h * QK_PAD
        k_ref[:, lo:lo + LANES] = k_nope[:, h * LANES:(h + 1) * LANES].astype(BF16)
        k_ref[:, lo + LANES:lo + QK_PAD] = k_pe

    qm = _dot(un, w_in_ref[:, o_qm:o_km])
    km = _dot(un, w_in_ref[:, o_km:o_vm])
    v_ref[:, MLA_WIDTH:] = _dot(un, w_in_ref[:, o_vm:]).astype(BF16)
    lane = lax.broadcasted_iota(jnp.int32, (x_ref.shape[0], LANES), 1)
    one_hot = jnp.where(lane == blk, 1.0, 0.0).astype(BF16)
    blk_row = lax.broadcasted_iota(jnp.int32, (nb, LANES), 0)
    for h in range(MOBA_HEADS):
        sl = slice(h * LANES, (h + 1) * LANES)
        lo = (MLA_HEADS + h) * QK_PAD
        qh = _rope(qm[:, sl], cm, sm).astype(BF16)
        kh = _rope(km[:, sl], cm, sm)
        kmean = jnp.where(blk_row == blk, jnp.mean(kh, axis=0, keepdims=True), kmean_ref[:, sl])
        kmean_ref[:, sl] = kmean
        q_ref[:, lo:lo + LANES] = qh
        q_ref[:, lo + LANES:lo + QK_PAD] = _moba_bias(kmean, qh, blk).astype(BF16)
        k_ref[:, lo:lo + LANES] = kh.astype(BF16)
        k_ref[:, lo + LANES:lo + QK_PAD] = one_hot


def _proj(x, g, w_in, gq, gkv, w_uq, w_uk, w_uv, ca, sa, cm, sm, *, seq):
    T, D = x.shape
    tm = MOBA_BLOCK
    n_in = w_in.shape[1]
    nb = seq // tm
    assert seq % tm == 0 and nb % SUBLANES_BF16 == 0 and nb <= LANES
    tok = lambda w: pl.BlockSpec((tm, w), lambda i: (i, 0))
    tab = pl.BlockSpec((tm, LANES), lambda i: (i % nb, 0))
    wa = MLA_HEADS * QK_PAD
    vmem = (2 * tm * D * 4
            + (D * n_in + MLA_Q_RANK * wa + 2 * MLA_KV_RANK * MLA_WIDTH) * 2
            + 2 * tm * (2 * HEADS * QK_PAD + HEADS * V_DIM) * 2
            + 6 * tm * wa * 4
            + 8 * MIB)
    return pl.pallas_call(
        _proj_kernel,
        grid=(T // tm,),
        in_specs=[
            tok(D), _const_spec((1, D)), _const_spec((D, n_in)),
            _const_spec((1, MLA_Q_RANK)), _const_spec((1, MLA_KV_RANK)),
            _const_spec((MLA_Q_RANK, wa)), _const_spec((MLA_KV_RANK, MLA_WIDTH)),
            _const_spec((MLA_KV_RANK, MLA_WIDTH)),
            tab, tab, tab, tab,
        ],
        out_specs=[tok(HEADS * QK_PAD), tok(HEADS * QK_PAD), tok(HEADS * V_DIM)],
        out_shape=[
            jax.ShapeDtypeStruct((T, HEADS * QK_PAD), BF16),
            jax.ShapeDtypeStruct((T, HEADS * QK_PAD), BF16),
            jax.ShapeDtypeStruct((T, HEADS * V_DIM), BF16),
        ],
        scratch_shapes=[pltpu.VMEM((nb, MOBA_WIDTH), F32)],
        compiler_params=pltpu.CompilerParams(
            dimension_semantics=("arbitrary",), vmem_limit_bytes=vmem),
        name="proj",
    )(x, g, w_in, gq, gkv, w_uq, w_uk, w_uv, ca, sa, cm, sm)


LOG2_E = 1.4426950408889634


def _attn_kernel(q_ref, k_ref, v_ref, o_ref, m_ref, acc_ref, *, t):
    h = pl.program_id(1)
    i = pl.program_id(2)
    scale = jnp.where(h < MLA_HEADS, MLA_QK_DIM ** -0.5, MOBA_HEAD_DIM ** -0.5)
    c2 = (scale * LOG2_E).astype(F32)
    ones = jnp.ones((t, LANES), BF16)
    m_ref[...] = jnp.full(m_ref.shape, -jnp.inf, F32)
    acc_ref[...] = jnp.zeros_like(acc_ref)

    def block(half, j, diagonal):
        rows = pl.ds(pl.multiple_of(j * t, t), t)
        s = _dot_nt(q_ref[half * t:(half + 1) * t, :], k_ref[rows, :])
        if diagonal:
            row = lax.broadcasted_iota(jnp.int32, s.shape, 0)
            col = lax.broadcasted_iota(jnp.int32, s.shape, 1)
            s = jnp.where(col <= row, s, NEG_INF)
        chunks = [s[:, c * LANES:(c + 1) * LANES] for c in range(t // LANES)]
        m_cur = jnp.max(functools.reduce(jnp.maximum, chunks), axis=1, keepdims=True)
        m_prev = m_ref[half]
        m_new = jnp.maximum(m_prev, jnp.broadcast_to(m_cur, (t, LANES)) * c2)
        alpha = jnp.exp2(m_prev - m_new)
        p = jnp.concatenate([jnp.exp2(c * c2 - m_new).astype(BF16) for c in chunks], axis=1)
        pv = _dot(p, jnp.concatenate([v_ref[rows, :], ones], axis=1))
        acc_ref[half] = jnp.concatenate([alpha, alpha], axis=1) * acc_ref[half] + pv
        m_ref[half] = m_new

    def body(j, carry):
        block(0, j, False)
        block(1, j, False)
        return carry

    lax.fori_loop(0, 2 * i, body, 0)
    block(0, 2 * i, True)
    block(1, 2 * i, False)
    block(1, 2 * i + 1, True)
    for half in range(2):
        acc = acc_ref[half]
        o_ref[half * t:(half + 1) * t, :] = acc[:, :V_DIM] / acc[:, V_DIM:]


def _attention(q, k, v, *, t=512):
    B, S, _ = q.shape
    assert S % (2 * t) == 0 and t % MOBA_BLOCK == 0
    kern = functools.partial(_attn_kernel, t=t)
    vmem = (2 * S * (QK_PAD + V_DIM) * 2 + 2 * 2 * t * QK_PAD * 2 + 2 * 2 * t * V_DIM * 4
            + 2 * t * (LANES + 2 * V_DIM) * 4 + 2 * 6 * t * t * 4 + 8 * MIB)
    return pl.pallas_call(
        kern,
        grid=(B, HEADS, S // (2 * t)),
        in_specs=[
            pl.BlockSpec((None, 2 * t, QK_PAD), lambda b, h, i: (b, i, h)),
            pl.BlockSpec((None, S, QK_PAD), lambda b, h, i: (b, 0, h)),
            pl.BlockSpec((None, S, V_DIM), lambda b, h, i: (b, 0, h)),
        ],
        out_specs=pl.BlockSpec((None, 2 * t, V_DIM), lambda b, h, i: (b, i, h)),
        out_shape=jax.ShapeDtypeStruct((B, S, HEADS * V_DIM), F32),
        scratch_shapes=[pltpu.VMEM((2, t, LANES), F32), pltpu.VMEM((2, t, 2 * V_DIM), F32)],
        compiler_params=pltpu.CompilerParams(
            dimension_semantics=("parallel", "parallel", "arbitrary"), vmem_limit_bytes=vmem),
        name="attention",
    )(q, k, v)


def _out_kernel(x_ref, am_ref, ga_ref, gm_ref, w_ref, gp_ref, o_ref):
    an = _rms(am_ref[:, :MLA_WIDTH], ga_ref[...]).astype(BF16)
    mn = _rms(am_ref[:, MLA_WIDTH:], gm_ref[...]).astype(BF16)
    y = _dot(jnp.concatenate([an, mn], axis=1), w_ref[...])
    o_ref[...] = x_ref[...] + _rms(y, gp_ref[...])


def _out_proj(x, am, ga, gm, w, gp, *, tm=512):
    T, D = x.shape
    W = MLA_WIDTH + MOBA_WIDTH
    tok = lambda wd: pl.BlockSpec((tm, wd), lambda i: (i, 0))
    vmem = 2 * 2 * tm * D * 4 + 2 * tm * W * 4 + W * D * 2 + 4 * tm * D * 4 + 8 * MIB
    return pl.pallas_call(
        _out_kernel,
        grid=(T // tm,),
        in_specs=[tok(D), tok(W), _const_spec((1, MLA_WIDTH)), _const_spec((1, MOBA_WIDTH)),
                  _const_spec((W, D)), _const_spec((1, D))],
        out_specs=tok(D),
        out_shape=jax.ShapeDtypeStruct((T, D), F32),
        compiler_params=pltpu.CompilerParams(
            dimension_semantics=("parallel",), vmem_limit_bytes=vmem),
        name="out_proj",
    )(x, am, ga, gm, w, gp)


def _spread_rope_cols(w):
    half = MLA_ROPE_DIM // 2
    z = jnp.zeros(w.shape[:-1] + (LANES // 2 - half,), w.dtype)
    return jnp.concatenate([w[..., :half], z, w[..., half:], z], axis=-1)


def _rope_tables(seq):
    pos = jnp.arange(seq, dtype=F32)[:, None]

    def cos_sin(dim):
        inv = 1.0 / (ROPE_THETA ** (jnp.arange(0, dim, 2, dtype=F32) / dim))
        ang = pos * inv[None, :]
        return jnp.cos(ang), jnp.sin(ang)

    c, s = cos_sin(MOBA_HEAD_DIM)
    cm = jnp.concatenate([c, c], axis=-1)
    sm = jnp.concatenate([-s, s], axis=-1)
    c, s = cos_sin(MLA_ROPE_DIM)
    z = jnp.zeros_like(c)
    ca = jnp.concatenate([c, z, c, z], axis=-1)
    sa = jnp.concatenate([-s, z, s, z], axis=-1)
    return ca, sa, cm, sm


def kernel(x, ffn1_pre_g, ffn1_w_gate, ffn1_w_up, ffn1_w_down, ffn1_post_g, mix_pre_g, w_in, mla_q_norm_g, mla_kv_norm_g, mla_w_uq, mla_w_ukv, mla_out_g, moba_out_g, w_out, mix_post_g, ffn2_pre_g, ffn2_w_gate, ffn2_w_up, ffn2_w_down, ffn2_post_g):
    B, S, D = x.shape
    depth = w_in.shape[0]
    H = MLA_HEADS
    ca, sa, cm, sm = _rope_tables(S)
    xt = x.reshape(B * S, D)
    for l in range(depth):
        o_kpe = MLA_Q_RANK + MLA_KV_RANK
        w_in_l = jnp.concatenate(
            [w_in[l][:, :o_kpe], _spread_rope_cols(w_in[l][:, o_kpe:o_kpe + MLA_ROPE_DIM]),
             w_in[l][:, o_kpe + MLA_ROPE_DIM:]], axis=1).astype(BF16)
        uq = mla_w_uq[l].reshape(MLA_Q_RANK, H, MLA_QK_DIM)
        w_uq = jnp.concatenate(
            [uq[..., :MLA_NOPE_DIM], _spread_rope_cols(uq[..., MLA_NOPE_DIM:])], axis=-1
        ).reshape(MLA_Q_RANK, H * QK_PAD).astype(BF16)
        ukv = mla_w_ukv[l].reshape(MLA_KV_RANK, H, MLA_NOPE_DIM + MLA_V_DIM)
        w_uk = ukv[..., :MLA_NOPE_DIM].reshape(MLA_KV_RANK, MLA_WIDTH).astype(BF16)
        w_uv = ukv[..., MLA_NOPE_DIM:].reshape(MLA_KV_RANK, MLA_WIDTH).astype(BF16)

        xt = _ffn(xt, ffn1_pre_g[l][None], ffn1_w_gate[l].astype(BF16), ffn1_w_up[l].astype(BF16),
                  ffn1_w_down[l].astype(BF16), ffn1_post_g[l][None])
        q, k, v = _proj(xt, mix_pre_g[l][None], w_in_l, mla_q_norm_g[l][None],
                        mla_kv_norm_g[l][None], w_uq, w_uk, w_uv, ca, sa, cm, sm, seq=S)
        am = _attention(q.reshape(B, S, -1), k.reshape(B, S, -1), v.reshape(B, S, -1))
        xt = _out_proj(xt, am.reshape(B * S, -1), mla_out_g[l][None], moba_out_g[l][None],
                       w_out[l].astype(BF16), mix_post_g[l][None])
        xt = _ffn(xt, ffn2_pre_g[l][None], ffn2_w_gate[l].astype(BF16), ffn2_w_up[l].astype(BF16),
                  ffn2_w_down[l].astype(BF16), ffn2_post_g[l][None])
    return xt.reshape(B, S, D)
```

```python
import functools

import jax
import jax.numpy as jnp
from jax import lax
from jax.experimental import pallas as pl
from jax.experimental.pallas import tpu as pltpu

D_MODEL = 2048
FFN_DIM = 5632
MLA_HEADS = 8
MLA_Q_RANK = 512
MLA_KV_RANK = 256
MLA_NOPE_DIM = 128
MLA_ROPE_DIM = 64
MLA_V_DIM = 128
MLA_QK_DIM = MLA_NOPE_DIM + MLA_ROPE_DIM
MOBA_HEADS = 8
MOBA_HEAD_DIM = 128
MOBA_BLOCK = 256
MOBA_TOPK = 3
MLA_WIDTH = MLA_HEADS * MLA_V_DIM
MOBA_WIDTH = MOBA_HEADS * MOBA_HEAD_DIM
ROPE_THETA = 10000.0
NORM_EPS = 1e-6
NEG_INF = -1e30
MASK_BIAS = -(2.0 ** 100)

LANES = 128
SUBLANES_BF16 = 16
HEADS = MLA_HEADS + MOBA_HEADS
QK_PAD = 2 * LANES
V_DIM = 128
MIB = 2 ** 20

BF16 = jnp.bfloat16
F32 = jnp.float32


def _rms(x, g):
    return x * lax.rsqrt(jnp.mean(x * x, axis=-1, keepdims=True) + NORM_EPS) * g


def _dot(a, b):
    return jnp.dot(a, b, preferred_element_type=F32)


def _dot_nt(a, b):
    return lax.dot_general(a, b, (((1,), (1,)), ((), ())), preferred_element_type=F32)


def _const_spec(shape):
    return pl.BlockSpec(shape, lambda *_: (0,) * len(shape), pipeline_mode=pl.Buffered(1))


def _ffn_kernel(x_ref, pre_g_ref, wg_ref, wu_ref, wd_ref, post_g_ref, o_ref, xn_ref, acc_ref):
    f = pl.program_id(1)

    @pl.when(f == 0)
    def _():
        xn_ref[...] = _rms(x_ref[...], pre_g_ref[...]).astype(BF16)
        acc_ref[...] = jnp.zeros_like(acc_ref)

    xn = xn_ref[...]
    h = _dot(xn, wg_ref[...])
    u = _dot(xn, wu_ref[...])
    a = (h * jax.nn.sigmoid(h)) * u
    acc_ref[...] += _dot(a.astype(BF16), wd_ref[...])

    @pl.when(f == pl.num_programs(1) - 1)
    def _():
        o_ref[...] = x_ref[...] + 0.5 * _rms(acc_ref[...], post_g_ref[...])


def _ffn(x, pre_g, wg, wu, wd, post_g, *, tm=512, tf=512):
    T, D = x.shape
    F = wg.shape[1]
    vmem = (2 * 2 * tm * D * 4
            + tm * D * 2 + tm * D * 4
            + 2 * 3 * D * tf * 2
            + 4 * tm * tf * 4
            + 8 * MIB)
    return pl.pallas_call(
        _ffn_kernel,
        grid=(T // tm, F // tf),
        in_specs=[
            pl.BlockSpec((tm, D), lambda i, f: (i, 0)),
            pl.BlockSpec((1, D), lambda i, f: (0, 0)),
            pl.BlockSpec((D, tf), lambda i, f: (0, f)),
            pl.BlockSpec((D, tf), lambda i, f: (0, f)),
            pl.BlockSpec((tf, D), lambda i, f: (f, 0)),
            pl.BlockSpec((1, D), lambda i, f: (0, 0)),
        ],
        out_specs=pl.BlockSpec((tm, D), lambda i, f: (i, 0)),
        out_shape=jax.ShapeDtypeStruct((T, D), F32),
        scratch_shapes=[pltpu.VMEM((tm, D), BF16), pltpu.VMEM((tm, D), F32)],
        compiler_params=pltpu.CompilerParams(
            dimension_semantics=("parallel", "arbitrary"), vmem_limit_bytes=vmem),
        name="ffn",
    )(x, pre_g, wg, wu, wd, post_g)


def _rope(x, c, s):
    return x * c + pltpu.roll(x, LANES // 2, axis=1) * s


def _moba_bias(kmean, qh, blk):
    nb = kmean.shape[0]
    gate = _dot_nt(kmean.astype(BF16), qh)
    n = lax.broadcasted_iota(jnp.int32, gate.shape, 0)
    cand = jnp.where(n < blk, gate, -jnp.inf)
    allowed = n == blk
    for _ in range(min(MOBA_TOPK, nb - 1)):
        best = jnp.max(cand, axis=0, keepdims=True)
        first = jnp.min(jnp.where(cand == best, n, nb), axis=0, keepdims=True)
        pick = (n == first) & (best > -jnp.inf)
        allowed = allowed | pick
        cand = jnp.where(pick, -jnp.inf, cand)
    bias = jnp.where(allowed, 0.0, MASK_BIAS)
    bias = jnp.concatenate([bias, jnp.zeros((LANES - nb, bias.shape[1]), F32)], axis=0)
    return bias.T


def _proj_kernel(x_ref, g_ref, w_lat_ref, w_kpe_ref, w_moba_ref, gq_ref, gkv_ref,
                 w_uq_ref, w_uk_ref, w_uv_ref, ca_ref, sa_ref, cm_ref, sm_ref,
                 q_ref, k_ref, v_ref, kmean_ref):
    nb = kmean_ref.shape[0]
    blk = pl.program_id(0) % nb

    @pl.when(pl.program_id(0) == 0)
    def _():
        kmean_ref[...] = jnp.zeros_like(kmean_ref)

    un = _rms(x_ref[...], g_ref[...]).astype(BF16)
    ca, sa, cm, sm = ca_ref[...], sa_ref[...], cm_ref[...], sm_ref[...]

    qn = _rms(_dot(un, w_lat_ref[:, :MLA_Q_RANK]), gq_ref[...]).astype(BF16)
    q = _dot(qn, w_uq_ref[...])
    for h in range(MLA_HEADS):
        lo = h * QK_PAD
        q_ref[:, lo:lo + LANES] = q[:, lo:lo + LANES].astype(BF16)
        q_ref[:, lo + LANES:lo + QK_PAD] = _rope(q[:, lo + LANES:lo + QK_PAD], ca, sa).astype(BF16)

    kvn = _rms(_dot(un, w_lat_ref[:, MLA_Q_RANK:]), gkv_ref[...]).astype(BF16)
    k_nope = _dot(kvn, w_uk_ref[...])
    v_ref[:, :MLA_WIDTH] = _dot(kvn, w_uv_ref[...]).astype(BF16)
    k_pe = _rope(_dot(un, w_kpe_ref[...]), ca, sa).astype(BF16)
    for h in range(MLA_HEADS):
        lo = h * QK_PAD
        k_ref[:, lo:lo + LANES] = k_nope[:, h * LANES:(h + 1) * LANES].astype(BF16)
        k_ref[:, lo + LANES:lo + QK_PAD] = k_pe

    qm = _dot(un, w_moba_ref[:, :MOBA_WIDTH])
    km = _dot(un, w_moba_ref[:, MOBA_WIDTH:2 * MOBA_WIDTH])
    v_ref[:, MLA_WIDTH:] = _dot(un, w_moba_ref[:, 2 * MOBA_WIDTH:]).astype(BF16)
    lane = lax.broadcasted_iota(jnp.int32, (x_ref.shape[0], LANES), 1)
    one_hot = jnp.where(lane == blk, 1.0, 0.0).astype(BF16)
    blk_row = lax.broadcasted_iota(jnp.int32, (nb, LANES), 0)
    for h in range(MOBA_HEADS):
        sl = slice(h * LANES, (h + 1) * LANES)
        lo = (MLA_HEADS + h) * QK_PAD
        qh = _rope(qm[:, sl], cm, sm).astype(BF16)
        kh = _rope(km[:, sl], cm, sm)
        kmean = jnp.where(blk_row == blk, jnp.mean(kh, axis=0, keepdims=True), kmean_ref[:, sl])
        kmean_ref[:, sl] = kmean
        q_ref[:, lo:lo + LANES] = qh
        q_ref[:, lo + LANES:lo + QK_PAD] = _moba_bias(kmean, qh, blk).astype(BF16)
        k_ref[:, lo:lo + LANES] = kh.astype(BF16)
        k_ref[:, lo + LANES:lo + QK_PAD] = one_hot


def _proj(x, g, w_lat, w_kpe, w_moba, gq, gkv, w_uq, w_uk, w_uv, ca, sa, cm, sm, *, seq):
    T, D = x.shape
    tm = MOBA_BLOCK
    n_in = w_lat.shape[1] + w_kpe.shape[1] + w_moba.shape[1]
    nb = seq // tm
    assert seq % tm == 0 and nb % SUBLANES_BF16 == 0 and nb <= LANES
    tok = lambda w: pl.BlockSpec((tm, w), lambda i: (i, 0))
    tab = pl.BlockSpec((tm, LANES), lambda i: (i % nb, 0))
    wa = MLA_HEADS * QK_PAD
    vmem = (2 * tm * D * 4
            + (D * n_in + MLA_Q_RANK * wa + 2 * MLA_KV_RANK * MLA_WIDTH) * 2
            + 2 * tm * (2 * HEADS * QK_PAD + HEADS * V_DIM) * 2
            + 6 * tm * wa * 4
            + 8 * MIB)
    return pl.pallas_call(
        _proj_kernel,
        grid=(T // tm,),
        in_specs=[
            tok(D), _const_spec((1, D)),
            _const_spec(w_lat.shape), _const_spec(w_kpe.shape), _const_spec(w_moba.shape),
            _const_spec((1, MLA_Q_RANK)), _const_spec((1, MLA_KV_RANK)),
            _const_spec((MLA_Q_RANK, wa)), _const_spec((MLA_KV_RANK, MLA_WIDTH)),
            _const_spec((MLA_KV_RANK, MLA_WIDTH)),
            tab, tab, tab, tab,
        ],
        out_specs=[tok(HEADS * QK_PAD), tok(HEADS * QK_PAD), tok(HEADS * V_DIM)],
        out_shape=[
            jax.ShapeDtypeStruct((T, HEADS * QK_PAD), BF16),
            jax.ShapeDtypeStruct((T, HEADS * QK_PAD), BF16),
            jax.ShapeDtypeStruct((T, HEADS * V_DIM), BF16),
        ],
        scratch_shapes=[pltpu.VMEM((nb, MOBA_WIDTH), F32)],
        compiler_params=pltpu.CompilerParams(
            dimension_semantics=("arbitrary",), vmem_limit_bytes=vmem),
        name="proj",
    )(x, g, w_lat, w_kpe, w_moba, gq, gkv, w_uq, w_uk, w_uv, ca, sa, cm, sm)


LOG2_E = 1.4426950408889634


def _attn_kernel(q_ref, k_ref, v_ref, o_ref, m_ref, acc_ref, *, t, nsub):
    h = pl.program_id(1)
    i = pl.program_id(2)
    scale = jnp.where(h < MLA_HEADS, MLA_QK_DIM ** -0.5, MOBA_HEAD_DIM ** -0.5)
    c2 = (scale * LOG2_E).astype(F32)
    m_ref[...] = jnp.full(m_ref.shape, -jnp.inf, F32)
    acc_ref[...] = jnp.zeros_like(acc_ref)

    def logits(sub, start, tk):
        q = q_ref[sub * t:(sub + 1) * t, :]
        return _dot_nt(q, k_ref[pl.ds(pl.multiple_of(start, t), tk), :])

    def update(sub, s, start, tk, diagonal):
        parts = [pl.ds(pl.multiple_of(start + c * t, t), t) for c in range(tk // t)]
        if diagonal:
            row = lax.broadcasted_iota(jnp.int32, s.shape, 0)
            col = lax.broadcasted_iota(jnp.int32, s.shape, 1)
            s = jnp.where(col <= row, s, NEG_INF)
        chunks = [s[:, c * LANES:(c + 1) * LANES] for c in range(tk // LANES)]
        m_cur = jnp.max(functools.reduce(jnp.maximum, chunks), axis=1, keepdims=True)
        m_prev = m_ref[sub]
        m_new = jnp.maximum(m_prev, jnp.broadcast_to(m_cur, (t, LANES)) * c2)
        alpha = jnp.exp2(m_prev - m_new)
        p = [jnp.exp2(c * c2 - m_new).astype(BF16) for c in chunks]
        ones = jnp.ones((t, LANES), BF16)
        n = t // LANES
        pv = [_dot(jnp.concatenate(p[c * n:(c + 1) * n], axis=1),
                   jnp.concatenate([v_ref[r, :], ones], axis=1)) for c, r in enumerate(parts)]
        acc_ref[sub] = jnp.concatenate([alpha, alpha], axis=1) * acc_ref[sub] + sum(pv)
        m_ref[sub] = m_new

    def body(j, carry):
        start = j * 2 * t
        ss = [logits(sub, start, 2 * t) for sub in range(nsub)]
        for sub in range(nsub):
            update(sub, ss[sub], start, 2 * t, False)
        return carry

    lax.fori_loop(0, i * (nsub // 2), body, 0)
    first = i * nsub
    for kb in range(nsub):
        for sub in range(kb, nsub):
            start = (first + kb) * t
            update(sub, logits(sub, start, t), start, t, sub == kb)
    for sub in range(nsub):
        acc = acc_ref[sub]
        o_ref[sub * t:(sub + 1) * t, :] = acc[:, :V_DIM] / acc[:, V_DIM:]


def _attention(q, k, v, *, t=512, nsub=4):
    B, S, _ = q.shape
    tq = nsub * t
    assert S % tq == 0 and t % MOBA_BLOCK == 0 and nsub % 2 == 0
    kern = functools.partial(_attn_kernel, t=t, nsub=nsub)
    vmem = (2 * S * (QK_PAD + V_DIM) * 2 + 2 * tq * QK_PAD * 2 + 2 * tq * V_DIM * 4
            + tq * (LANES + 2 * V_DIM) * 4 + nsub * 6 * t * t * 4 + 8 * MIB)
    return pl.pallas_call(
        kern,
        grid=(B, HEADS, S // tq),
        in_specs=[
            pl.BlockSpec((None, tq, QK_PAD), lambda b, h, i: (b, i, h)),
            pl.BlockSpec((None, S, QK_PAD), lambda b, h, i: (b, 0, h)),
            pl.BlockSpec((None, S, V_DIM), lambda b, h, i: (b, 0, h)),
        ],
        out_specs=pl.BlockSpec((None, tq, V_DIM), lambda b, h, i: (b, i, h)),
        out_shape=jax.ShapeDtypeStruct((B, S, HEADS * V_DIM), F32),
        scratch_shapes=[pltpu.VMEM((nsub, t, LANES), F32),
                        pltpu.VMEM((nsub, t, 2 * V_DIM), F32)],
        compiler_params=pltpu.CompilerParams(
            dimension_semantics=("parallel", "parallel", "arbitrary"), vmem_limit_bytes=vmem),
        name="attention",
    )(q, k, v)


def _out_kernel(x_ref, am_ref, ga_ref, gm_ref, w_ref, gp_ref, o_ref):
    an = _rms(am_ref[:, :MLA_WIDTH], ga_ref[...]).astype(BF16)
    mn = _rms(am_ref[:, MLA_WIDTH:], gm_ref[...]).astype(BF16)
    y = _dot(jnp.concatenate([an, mn], axis=1), w_ref[...])
    o_ref[...] = x_ref[...] + _rms(y, gp_ref[...])


def _out_proj(x, am, ga, gm, w, gp, *, tm=512):
    T, D = x.shape
    W = MLA_WIDTH + MOBA_WIDTH
    tok = lambda wd: pl.BlockSpec((tm, wd), lambda i: (i, 0))
    vmem = 2 * 2 * tm * D * 4 + 2 * tm * W * 4 + W * D * 2 + 4 * tm * D * 4 + 8 * MIB
    return pl.pallas_call(
        _out_kernel,
        grid=(T // tm,),
        in_specs=[tok(D), tok(W), _const_spec((1, MLA_WIDTH)), _const_spec((1, MOBA_WIDTH)),
                  _const_spec((W, D)), _const_spec((1, D))],
        out_specs=tok(D),
        out_shape=jax.ShapeDtypeStruct((T, D), F32),
        compiler_params=pltpu.CompilerParams(
            dimension_semantics=("parallel",), vmem_limit_bytes=vmem),
        name="out_proj",
    )(x, am, ga, gm, w, gp)


def _spread_rope_cols(w):
    half = MLA_ROPE_DIM // 2
    z = jnp.zeros(w.shape[:-1] + (LANES // 2 - half,), w.dtype)
    return jnp.concatenate([w[..., :half], z, w[..., half:], z], axis=-1)


def _rope_tables(seq):
    pos = jnp.arange(seq, dtype=F32)[:, None]

    def cos_sin(dim):
        inv = 1.0 / (ROPE_THETA ** (jnp.arange(0, dim, 2, dtype=F32) / dim))
        ang = pos * inv[None, :]
        return jnp.cos(ang), jnp.sin(ang)

    c, s = cos_sin(MOBA_HEAD_DIM)
    cm = jnp.concatenate([c, c], axis=-1)
    sm = jnp.concatenate([-s, s], axis=-1)
    c, s = cos_sin(MLA_ROPE_DIM)
    z = jnp.zeros_like(c)
    ca = jnp.concatenate([c, z, c, z], axis=-1)
    sa = jnp.concatenate([-s, z, s, z], axis=-1)
    return ca, sa, cm, sm


def kernel(x, ffn1_pre_g, ffn1_w_gate, ffn1_w_up, ffn1_w_down, ffn1_post_g, mix_pre_g, w_in, mla_q_norm_g, mla_kv_norm_g, mla_w_uq, mla_w_ukv, mla_out_g, moba_out_g, w_out, mix_post_g, ffn2_pre_g, ffn2_w_gate, ffn2_w_up, ffn2_w_down, ffn2_post_g):
    B, S, D = x.shape
    depth = w_in.shape[0]
    H = MLA_HEADS
    ca, sa, cm, sm = _rope_tables(S)
    xt = x.reshape(B * S, D)
    for l in range(depth):
        o_kpe = MLA_Q_RANK + MLA_KV_RANK
        w_lat = w_in[l][:, :o_kpe].astype(BF16)
        w_kpe = _spread_rope_cols(w_in[l][:, o_kpe:o_kpe + MLA_ROPE_DIM]).astype(BF16)
        w_moba = w_in[l][:, o_kpe + MLA_ROPE_DIM:].astype(BF16)
        uq = mla_w_uq[l].reshape(MLA_Q_RANK, H, MLA_QK_DIM)
        w_uq = jnp.concatenate(
            [uq[..., :MLA_NOPE_DIM], _spread_rope_cols(uq[..., MLA_NOPE_DIM:])], axis=-1
        ).reshape(MLA_Q_RANK, H * QK_PAD).astype(BF16)
        ukv = mla_w_ukv[l].reshape(MLA_KV_RANK, H, MLA_NOPE_DIM + MLA_V_DIM)
        w_uk = ukv[..., :MLA_NOPE_DIM].reshape(MLA_KV_RANK, MLA_WIDTH).astype(BF16)
        w_uv = ukv[..., MLA_NOPE_DIM:].reshape(MLA_KV_RANK, MLA_WIDTH).astype(BF16)

        xt = _ffn(xt, ffn1_pre_g[l][None], ffn1_w_gate[l].astype(BF16), ffn1_w_up[l].astype(BF16),
                  ffn1_w_down[l].astype(BF16), ffn1_post_g[l][None])
        q, k, v = _proj(xt, mix_pre_g[l][None], w_lat, w_kpe, w_moba, mla_q_norm_g[l][None],
                        mla_kv_norm_g[l][None], w_uq, w_uk, w_uv, ca, sa, cm, sm, seq=S)
        am = _attention(q.reshape(B, S, -1), k.reshape(B, S, -1), v.reshape(B, S, -1))
        xt = _out_proj(xt, am.reshape(B * S, -1), mla_out_g[l][None], moba_out_g[l][None],
                       w_out[l].astype(BF16), mix_post_g[l][None])
        xt = _ffn(xt, ffn2_pre_g[l][None], ffn2_w_gate[l].astype(BF16), ffn2_w_up[l].astype(BF16),
                  ffn2_w_down[l].astype(BF16), ffn2_post_g[l][None])
    return xt.reshape(B, S, D)
```

```python
import functools

import jax
import jax.numpy as jnp
from jax import lax
from jax.experimental import pallas as pl
from jax.experimental.pallas import tpu as pltpu

D_MODEL = 2048
FFN_DIM = 5632
MLA_HEADS = 8
MLA_Q_RANK = 512
MLA_KV_RANK = 256
MLA_NOPE_DIM = 128
MLA_ROPE_DIM = 64
MLA_V_DIM = 128
MLA_QK_DIM = MLA_NOPE_DIM + MLA_ROPE_DIM
MOBA_HEADS = 8
MOBA_HEAD_DIM = 128
MOBA_BLOCK = 256
MOBA_TOPK = 3
MLA_WIDTH = MLA_HEADS * MLA_V_DIM
MOBA_WIDTH = MOBA_HEADS * MOBA_HEAD_DIM
ROPE_THETA = 10000.0
NORM_EPS = 1e-6
NEG_INF = -1e30
MASK_BIAS = -(2.0 ** 100)

LANES = 128
SUBLANES_BF16 = 16
HEADS = MLA_HEADS + MOBA_HEADS
QK_PAD = 2 * LANES
V_DIM = 128
V7X_VMEM_BYTES = 64 * 2 ** 20
MIB = 2 ** 20

BF16 = jnp.bfloat16
F32 = jnp.float32


def _rms(x, g):
    return x * lax.rsqrt(jnp.mean(x * x, axis=-1, keepdims=True) + NORM_EPS) * g


def _dot(a, b):
    return jnp.dot(a, b, preferred_element_type=F32)


def _dot_nt(a, b):
    return lax.dot_general(a, b, (((1,), (1,)), ((), ())), preferred_element_type=F32)


def _const_spec(shape):
    return pl.BlockSpec(shape, lambda *_: (0,) * len(shape), pipeline_mode=pl.Buffered(1))


def _ffn_kernel(x_ref, pre_g_ref, wg_ref, wu_ref, wd_ref, post_g_ref, o_ref, xn_ref):
    f = pl.program_id(1)

    @pl.when(f == 0)
    def _():
        xn_ref[...] = _rms(x_ref[...], pre_g_ref[...]).astype(BF16)
        o_ref[...] = jnp.zeros_like(o_ref)

    xn = xn_ref[...]
    h = _dot(xn, wg_ref[...].astype(BF16))
    u = _dot(xn, wu_ref[...].astype(BF16))
    a = (h * jax.nn.sigmoid(h)) * u
    o_ref[...] += _dot(a.astype(BF16), wd_ref[...].astype(BF16))

    @pl.when(f == pl.num_programs(1) - 1)
    def _():
        o_ref[...] = x_ref[...] + 0.5 * _rms(o_ref[...], post_g_ref[...])


def _ffn(x, pre_g, wg, wu, wd, post_g, *, tm=1024, tf=256):
    T, D = x.shape
    F = wg.shape[1]
    vmem = (2 * 2 * tm * D * 4
            + tm * D * 2
            + 2 * 3 * D * tf * 4
            + 3 * D * tf * 2
            + 4 * tm * tf * 4 + tm * D * 4)
    vmem = min(vmem, V7X_VMEM_BYTES)
    return pl.pallas_call(
        _ffn_kernel,
        grid=(T // tm, F // tf),
        in_specs=[
            pl.BlockSpec((tm, D), lambda i, f: (i, 0)),
            pl.BlockSpec((1, D), lambda i, f: (0, 0)),
            pl.BlockSpec((D, tf), lambda i, f: (0, f)),
            pl.BlockSpec((D, tf), lambda i, f: (0, f)),
            pl.BlockSpec((tf, D), lambda i, f: (f, 0)),
            pl.BlockSpec((1, D), lambda i, f: (0, 0)),
        ],
        out_specs=pl.BlockSpec((tm, D), lambda i, f: (i, 0)),
        out_shape=jax.ShapeDtypeStruct((T, D), F32),
        scratch_shapes=[pltpu.VMEM((tm, D), BF16)],
        compiler_params=pltpu.CompilerParams(
            dimension_semantics=("parallel", "arbitrary"), vmem_limit_bytes=vmem),
        name="ffn",
    )(x, pre_g, wg, wu, wd, post_g)


def _rope(x, c, s):
    return x * c + pltpu.roll(x, LANES // 2, axis=1) * s


def _moba_bias(kmean, qh, blk):
    nb = kmean.shape[0]
    gate = _dot_nt(kmean.astype(BF16), qh)
    n = lax.broadcasted_iota(jnp.int32, gate.shape, 0)
    cand = jnp.where(n < blk, gate, -jnp.inf)
    allowed = n == blk
    for _ in range(min(MOBA_TOPK, nb - 1)):
        best = jnp.max(cand, axis=0, keepdims=True)
        first = jnp.min(jnp.where(cand == best, n, nb), axis=0, keepdims=True)
        pick = (n == first) & (best > -jnp.inf)
        allowed = allowed | pick
        cand = jnp.where(pick, -jnp.inf, cand)
    bias = jnp.where(allowed, 0.0, MASK_BIAS)
    bias = jnp.concatenate([bias, jnp.zeros((LANES - nb, bias.shape[1]), F32)], axis=0)
    return bias.T


def _proj_kernel(x_ref, g_ref, w_lat_ref, w_kpe_ref, w_moba_ref, gq_ref, gkv_ref,
                 w_uq_ref, w_uk_ref, w_uv_ref, ca_ref, sa_ref, cm_ref, sm_ref,
                 q_ref, k_ref, v_ref, kmean_ref):
    nb = kmean_ref.shape[0]
    blk = pl.program_id(0) % nb

    @pl.when(pl.program_id(0) == 0)
    def _():
        kmean_ref[...] = jnp.zeros_like(kmean_ref)

    un = _rms(x_ref[...], g_ref[...]).astype(BF16)
    ca, sa, cm, sm = ca_ref[...], sa_ref[...], cm_ref[...], sm_ref[...]

    qn = _rms(_dot(un, w_lat_ref[:, :MLA_Q_RANK]), gq_ref[...]).astype(BF16)
    q = _dot(qn, w_uq_ref[...])
    for h in range(MLA_HEADS):
        lo = h * QK_PAD
        q_ref[:, lo:lo + LANES] = q[:, lo:lo + LANES].astype(BF16)
        q_ref[:, lo + LANES:lo + QK_PAD] = _rope(q[:, lo + LANES:lo + QK_PAD], ca, sa).astype(BF16)

    kvn = _rms(_dot(un, w_lat_ref[:, MLA_Q_RANK:]), gkv_ref[...]).astype(BF16)
    k_nope = _dot(kvn, w_uk_ref[...])
    v_ref[:, :MLA_WIDTH] = _dot(kvn, w_uv_ref[...]).astype(BF16)
    k_pe = _rope(_dot(un, w_kpe_ref[...]), ca, sa).astype(BF16)
    for h in range(MLA_HEADS):
        lo = h * QK_PAD
        k_ref[:, lo:lo + LANES] = k_nope[:, h * LANES:(h + 1) * LANES].astype(BF16)
        k_ref[:, lo + LANES:lo + QK_PAD] = k_pe

    qm = _dot(un, w_moba_ref[:, :MOBA_WIDTH])
    km = _dot(un, w_moba_ref[:, MOBA_WIDTH:2 * MOBA_WIDTH])
    v_ref[:, MLA_WIDTH:] = _dot(un, w_moba_ref[:, 2 * MOBA_WIDTH:]).astype(BF16)
    lane = lax.broadcasted_iota(jnp.int32, (x_ref.shape[0], LANES), 1)
    one_hot = jnp.where(lane == blk, 1.0, 0.0).astype(BF16)
    blk_row = lax.broadcasted_iota(jnp.int32, (nb, LANES), 0)
    for h in range(MOBA_HEADS):
        sl = slice(h * LANES, (h + 1) * LANES)
        lo = (MLA_HEADS + h) * QK_PAD
        qh = _rope(qm[:, sl], cm, sm).astype(BF16)
        kh = _rope(km[:, sl], cm, sm)
        kmean = jnp.where(blk_row == blk, jnp.mean(kh, axis=0, keepdims=True), kmean_ref[:, sl])
        kmean_ref[:, sl] = kmean
        q_ref[:, lo:lo + LANES] = qh
        q_ref[:, lo + LANES:lo + QK_PAD] = _moba_bias(kmean, qh, blk).astype(BF16)
        k_ref[:, lo:lo + LANES] = kh.astype(BF16)
        k_ref[:, lo + LANES:lo + QK_PAD] = one_hot


def _proj(x, g, w_lat, w_kpe, w_moba, gq, gkv, w_uq, w_uk, w_uv, ca, sa, cm, sm, *, seq):
    T, D = x.shape
    tm = MOBA_BLOCK
    n_in = w_lat.shape[1] + w_kpe.shape[1] + w_moba.shape[1]
    nb = seq // tm
    assert seq % tm == 0 and nb % SUBLANES_BF16 == 0 and nb <= LANES
    tok = lambda w: pl.BlockSpec((tm, w), lambda i: (i, 0))
    tab = pl.BlockSpec((tm, LANES), lambda i: (i % nb, 0))
    wa = MLA_HEADS * QK_PAD
    vmem = (2 * tm * D * 4
            + (D * n_in + MLA_Q_RANK * wa + 2 * MLA_KV_RANK * MLA_WIDTH) * 2
            + 2 * tm * (2 * HEADS * QK_PAD + HEADS * V_DIM) * 2
            + 6 * tm * wa * 4
            + 8 * MIB)
    return pl.pallas_call(
        _proj_kernel,
        grid=(T // tm,),
        in_specs=[
            tok(D), _const_spec((1, D)),
            _const_spec(w_lat.shape), _const_spec(w_kpe.shape), _const_spec(w_moba.shape),
            _const_spec((1, MLA_Q_RANK)), _const_spec((1, MLA_KV_RANK)),
            _const_spec((MLA_Q_RANK, wa)), _const_spec((MLA_KV_RANK, MLA_WIDTH)),
            _const_spec((MLA_KV_RANK, MLA_WIDTH)),
            tab, tab, tab, tab,
        ],
        out_specs=[tok(HEADS * QK_PAD), tok(HEADS * QK_PAD), tok(HEADS * V_DIM)],
        out_shape=[
            jax.ShapeDtypeStruct((T, HEADS * QK_PAD), BF16),
            jax.ShapeDtypeStruct((T, HEADS * QK_PAD), BF16),
            jax.ShapeDtypeStruct((T, HEADS * V_DIM), BF16),
        ],
        scratch_shapes=[pltpu.VMEM((nb, MOBA_WIDTH), F32)],
        compiler_params=pltpu.CompilerParams(
            dimension_semantics=("arbitrary",), vmem_limit_bytes=vmem),
        name="proj",
    )(x, g, w_lat, w_kpe, w_moba, gq, gkv, w_uq, w_uk, w_uv, ca, sa, cm, sm)


LOG2_E = 1.4426950408889634


def _attn_kernel(q_ref, k_ref, v_ref, o_ref, m_ref, acc_ref, *, t, nsub):
    h = pl.program_id(1)
    i = pl.program_id(2)
    scale = jnp.where(h < MLA_HEADS, MLA_QK_DIM ** -0.5, MOBA_HEAD_DIM ** -0.5)
    c2 = (scale * LOG2_E).astype(F32)
    m_ref[...] = jnp.full(m_ref.shape, -jnp.inf, F32)
    acc_ref[...] = jnp.zeros_like(acc_ref)

    def logits(sub, start, tk):
        q = q_ref[sub * t:(sub + 1) * t, :]
        return _dot_nt(q, k_ref[pl.ds(pl.multiple_of(start, t), tk), :])

    def update(sub, s, start, tk, diagonal):
        parts = [pl.ds(pl.multiple_of(start + c * t, t), t) for c in range(tk // t)]
        if diagonal:
            row = lax.broadcasted_iota(jnp.int32, s.shape, 0)
            col = lax.broadcasted_iota(jnp.int32, s.shape, 1)
            s = jnp.where(col <= row, s, NEG_INF)
        chunks = [s[:, c * LANES:(c + 1) * LANES] for c in range(tk // LANES)]
        m_cur = jnp.max(functools.reduce(jnp.maximum, chunks), axis=1, keepdims=True)
        m_prev = m_ref[sub]
        m_new = jnp.maximum(m_prev, jnp.broadcast_to(m_cur, (t, LANES)) * c2)
        alpha = jnp.exp2(m_prev - m_new)
        p = [jnp.exp2(c * c2 - m_new).astype(BF16) for c in chunks]
        ones = jnp.ones((t, LANES), BF16)
        n = t // LANES
        pv = [_dot(jnp.concatenate(p[c * n:(c + 1) * n], axis=1),
                   jnp.concatenate([v_ref[r, :], ones], axis=1)) for c, r in enumerate(parts)]
        acc_ref[sub] = jnp.concatenate([alpha, alpha], axis=1) * acc_ref[sub] + sum(pv)
        m_ref[sub] = m_new

    def body(j, carry):
        start = j * 2 * t
        ss = [logits(sub, start, 2 * t) for sub in range(nsub)]
        for sub in range(nsub):
            update(sub, ss[sub], start, 2 * t, False)
        return carry

    lax.fori_loop(0, i * (nsub // 2), body, 0)
    first = i * nsub
    for kb in range(nsub):
        for sub in range(kb, nsub):
            start = (first + kb) * t
            update(sub, logits(sub, start, t), start, t, sub == kb)
    for sub in range(nsub):
        acc = acc_ref[sub]
        o_ref[sub * t:(sub + 1) * t, :] = acc[:, :V_DIM] / acc[:, V_DIM:]


def _attention(q, k, v, *, t=512, nsub=4):
    B, S, _ = q.shape
    tq = nsub * t
    assert S % tq == 0 and t % MOBA_BLOCK == 0 and nsub % 2 == 0
    kern = functools.partial(_attn_kernel, t=t, nsub=nsub)
    vmem = (2 * S * (QK_PAD + V_DIM) * 2 + 2 * tq * QK_PAD * 2 + 2 * tq * V_DIM * 4
            + tq * (LANES + 2 * V_DIM) * 4 + nsub * 6 * t * t * 4 + 8 * MIB)
    return pl.pallas_call(
        kern,
        grid=(B, HEADS, S // tq),
        in_specs=[
            pl.BlockSpec((None, tq, QK_PAD), lambda b, h, i: (b, i, h)),
            pl.BlockSpec((None, S, QK_PAD), lambda b, h, i: (b, 0, h)),
            pl.BlockSpec((None, S, V_DIM), lambda b, h, i: (b, 0, h)),
        ],
        out_specs=pl.BlockSpec((None, tq, V_DIM), lambda b, h, i: (b, i, h)),
        out_shape=jax.ShapeDtypeStruct((B, S, HEADS * V_DIM), F32),
        scratch_shapes=[pltpu.VMEM((nsub, t, LANES), F32),
                        pltpu.VMEM((nsub, t, 2 * V_DIM), F32)],
        compiler_params=pltpu.CompilerParams(
            dimension_semantics=("parallel", "parallel", "arbitrary"), vmem_limit_bytes=vmem),
        name="attention",
    )(q, k, v)


def _out_kernel(x_ref, am_ref, ga_ref, gm_ref, w_ref, gp_ref, o_ref):
    an = _rms(am_ref[:, :MLA_WIDTH], ga_ref[...]).astype(BF16)
    mn = _rms(am_ref[:, MLA_WIDTH:], gm_ref[...]).astype(BF16)
    y = _dot(jnp.concatenate([an, mn], axis=1), w_ref[...])
    o_ref[...] = x_ref[...] + _rms(y, gp_ref[...])


def _out_proj(x, am, ga, gm, w, gp, *, tm=512):
    T, D = x.shape
    W = MLA_WIDTH + MOBA_WIDTH
    tok = lambda wd: pl.BlockSpec((tm, wd), lambda i: (i, 0))
    vmem = 2 * 2 * tm * D * 4 + 2 * tm * W * 4 + W * D * 2 + 4 * tm * D * 4 + 8 * MIB
    return pl.pallas_call(
        _out_kernel,
        grid=(T // tm,),
        in_specs=[tok(D), tok(W), _const_spec((1, MLA_WIDTH)), _const_spec((1, MOBA_WIDTH)),
                  _const_spec((W, D)), _const_spec((1, D))],
        out_specs=tok(D),
        out_shape=jax.ShapeDtypeStruct((T, D), F32),
        compiler_params=pltpu.CompilerParams(
            dimension_semantics=("parallel",), vmem_limit_bytes=vmem),
        name="out_proj",
    )(x, am, ga, gm, w, gp)


def _spread_rope_cols(w):
    half = MLA_ROPE_DIM // 2
    z = jnp.zeros(w.shape[:-1] + (LANES // 2 - half,), w.dtype)
    return jnp.concatenate([w[..., :half], z, w[..., half:], z], axis=-1)


def _rope_tables(seq):
    pos = jnp.arange(seq, dtype=F32)[:, None]

    def cos_sin(dim):
        inv = 1.0 / (ROPE_THETA ** (jnp.arange(0, dim, 2, dtype=F32) / dim))
        ang = pos * inv[None, :]
        return jnp.cos(ang), jnp.sin(ang)

    c, s = cos_sin(MOBA_HEAD_DIM)
    cm = jnp.concatenate([c, c], axis=-1)
    sm = jnp.concatenate([-s, s], axis=-1)
    c, s = cos_sin(MLA_ROPE_DIM)
    z = jnp.zeros_like(c)
    ca = jnp.concatenate([c, z, c, z], axis=-1)
    sa = jnp.concatenate([-s, z, s, z], axis=-1)
    return ca, sa, cm, sm


def kernel(x, ffn1_pre_g, ffn1_w_gate, ffn1_w_up, ffn1_w_down, ffn1_post_g, mix_pre_g, w_in, mla_q_norm_g, mla_kv_norm_g, mla_w_uq, mla_w_ukv, mla_out_g, moba_out_g, w_out, mix_post_g, ffn2_pre_g, ffn2_w_gate, ffn2_w_up, ffn2_w_down, ffn2_post_g):
    B, S, D = x.shape
    depth = w_in.shape[0]
    H = MLA_HEADS
    ca, sa, cm, sm = _rope_tables(S)
    xt = x.reshape(B * S, D)
    for l in range(depth):
        o_kpe = MLA_Q_RANK + MLA_KV_RANK
        w_lat = w_in[l][:, :o_kpe].astype(BF16)
        w_kpe = _spread_rope_cols(w_in[l][:, o_kpe:o_kpe + MLA_ROPE_DIM]).astype(BF16)
        w_moba = w_in[l][:, o_kpe + MLA_ROPE_DIM:].astype(BF16)
        uq = mla_w_uq[l].reshape(MLA_Q_RANK, H, MLA_QK_DIM)
        w_uq = jnp.concatenate(
            [uq[..., :MLA_NOPE_DIM], _spread_rope_cols(uq[..., MLA_NOPE_DIM:])], axis=-1
        ).reshape(MLA_Q_RANK, H * QK_PAD).astype(BF16)
        ukv = mla_w_ukv[l].reshape(MLA_KV_RANK, H, MLA_NOPE_DIM + MLA_V_DIM)
        w_uk = ukv[..., :MLA_NOPE_DIM].reshape(MLA_KV_RANK, MLA_WIDTH).astype(BF16)
        w_uv = ukv[..., MLA_NOPE_DIM:].reshape(MLA_KV_RANK, MLA_WIDTH).astype(BF16)

        xt = _ffn(xt, ffn1_pre_g[l][None], ffn1_w_gate[l], ffn1_w_up[l], ffn1_w_down[l],
                  ffn1_post_g[l][None])
        q, k, v = _proj(xt, mix_pre_g[l][None], w_lat, w_kpe, w_moba, mla_q_norm_g[l][None],
                        mla_kv_norm_g[l][None], w_uq, w_uk, w_uv, ca, sa, cm, sm, seq=S)
        am = _attention(q.reshape(B, S, -1), k.reshape(B, S, -1), v.reshape(B, S, -1))
        xt = _out_proj(xt, am.reshape(B * S, -1), mla_out_g[l][None], moba_out_g[l][None],
                       w_out[l].astype(BF16), mix_post_g[l][None])
        xt = _ffn(xt, ffn2_pre_g[l][None], ffn2_w_gate[l], ffn2_w_up[l], ffn2_w_down[l],
                  ffn2_post_g[l][None])
    return xt.reshape(B, S, D)
```

```python
import functools

import jax
import jax.numpy as jnp
from jax import lax
from jax.experimental import pallas as pl
from jax.experimental.pallas import tpu as pltpu

D_MODEL = 2048
FFN_DIM = 5632
MLA_HEADS = 8
MLA_Q_RANK = 512
MLA_KV_RANK = 256
MLA_NOPE_DIM = 128
MLA_ROPE_DIM = 64
MLA_V_DIM = 128
MLA_QK_DIM = MLA_NOPE_DIM + MLA_ROPE_DIM
MOBA_HEADS = 8
MOBA_HEAD_DIM = 128
MOBA_BLOCK = 256
MOBA_TOPK = 3
MLA_WIDTH = MLA_HEADS * MLA_V_DIM
MOBA_WIDTH = MOBA_HEADS * MOBA_HEAD_DIM
ROPE_THETA = 10000.0
NORM_EPS = 1e-6
NEG_INF = -1e30
MASK_BIAS = -(2.0 ** 100)

LANES = 128
SUBLANES_BF16 = 16
HEADS = MLA_HEADS + MOBA_HEADS
QK_PAD = 2 * LANES
V_DIM = 128
V7X_VMEM_BYTES = 64 * 2 ** 20
MIB = 2 ** 20

BF16 = jnp.bfloat16
F32 = jnp.float32


def _rms(x, g):
    return x * lax.rsqrt(jnp.mean(x * x, axis=-1, keepdims=True) + NORM_EPS) * g


def _dot(a, b):
    return jnp.dot(a, b, preferred_element_type=F32)


def _dot_nt(a, b):
    return lax.dot_general(a, b, (((1,), (1,)), ((), ())), preferred_element_type=F32)


def _const_spec(shape):
    return pl.BlockSpec(shape, lambda *_: (0,) * len(shape), pipeline_mode=pl.Buffered(1))


def _ffn_kernel(x_ref, pre_g_ref, wg_ref, wu_ref, wd_ref, post_g_ref, o_ref, xn_ref):
    f = pl.program_id(1)

    @pl.when(f == 0)
    def _():
        xn_ref[...] = _rms(x_ref[...], pre_g_ref[...]).astype(BF16)
        o_ref[...] = jnp.zeros_like(o_ref)

    xn = xn_ref[...]
    h = _dot(xn, wg_ref[...].astype(BF16))
    u = _dot(xn, wu_ref[...].astype(BF16))
    a = (h * jax.nn.sigmoid(h)) * u
    o_ref[...] += _dot(a.astype(BF16), wd_ref[...].astype(BF16))

    @pl.when(f == pl.num_programs(1) - 1)
    def _():
        o_ref[...] = x_ref[...] + 0.5 * _rms(o_ref[...], post_g_ref[...])


def _ffn(x, pre_g, wg, wu, wd, post_g, *, tm=1024, tf=256):
    T, D = x.shape
    F = wg.shape[1]
    vmem = (2 * 2 * tm * D * 4
            + tm * D * 2
            + 2 * 3 * D * tf * 4
            + 3 * D * tf * 2
            + 4 * tm * tf * 4 + tm * D * 4)
    vmem = min(vmem, V7X_VMEM_BYTES)
    return pl.pallas_call(
        _ffn_kernel,
        grid=(T // tm, F // tf),
        in_specs=[
            pl.BlockSpec((tm, D), lambda i, f: (i, 0)),
            pl.BlockSpec((1, D), lambda i, f: (0, 0)),
            pl.BlockSpec((D, tf), lambda i, f: (0, f)),
            pl.BlockSpec((D, tf), lambda i, f: (0, f)),
            pl.BlockSpec((tf, D), lambda i, f: (f, 0)),
            pl.BlockSpec((1, D), lambda i, f: (0, 0)),
        ],
        out_specs=pl.BlockSpec((tm, D), lambda i, f: (i, 0)),
        out_shape=jax.ShapeDtypeStruct((T, D), F32),
        scratch_shapes=[pltpu.VMEM((tm, D), BF16)],
        compiler_params=pltpu.CompilerParams(
            dimension_semantics=("parallel", "arbitrary"), vmem_limit_bytes=vmem),
        name="ffn",
    )(x, pre_g, wg, wu, wd, post_g)


def _rope(x, c, s):
    return x * c + pltpu.roll(x, LANES // 2, axis=1) * s


def _moba_bias(kmean, qh, blk):
    nb = kmean.shape[0]
    gate = _dot_nt(kmean.astype(BF16), qh)
    n = lax.broadcasted_iota(jnp.int32, gate.shape, 0)
    cand = jnp.where(n < blk, gate, -jnp.inf)
    allowed = n == blk
    for _ in range(min(MOBA_TOPK, nb - 1)):
        best = jnp.max(cand, axis=0, keepdims=True)
        first = jnp.min(jnp.where(cand == best, n, nb), axis=0, keepdims=True)
        pick = (n == first) & (best > -jnp.inf)
        allowed = allowed | pick
        cand = jnp.where(pick, -jnp.inf, cand)
    bias = jnp.where(allowed, 0.0, MASK_BIAS)
    bias = jnp.concatenate([bias, jnp.zeros((LANES - nb, bias.shape[1]), F32)], axis=0)
    return bias.T


def _proj_kernel(x_ref, g_ref, w_lat_ref, w_kpe_ref, w_moba_ref, gq_ref, gkv_ref,
                 w_uq_ref, w_uk_ref, w_uv_ref, ca_ref, sa_ref, cm_ref, sm_ref,
                 q_ref, k_ref, v_ref, un_ref, kmean_ref):
    nb = kmean_ref.shape[0]
    L = MOBA_BLOCK
    group = pl.program_id(1)

    @pl.when(group == 0)
    def _():
        un = _rms(x_ref[...], g_ref[...]).astype(BF16)
        un_ref[...] = un
        ca, sa = ca_ref[...], sa_ref[...]

        qn = _rms(_dot(un, w_lat_ref[:, :MLA_Q_RANK]), gq_ref[...]).astype(BF16)
        q = _dot(qn, w_uq_ref[...])
        for h in range(MLA_HEADS):
            lo = h * QK_PAD
            q_ref[:, lo:lo + LANES] = q[:, lo:lo + LANES].astype(BF16)
            q_ref[:, lo + LANES:lo + QK_PAD] = _rope(
                q[:, lo + LANES:lo + QK_PAD], ca, sa).astype(BF16)

        kvn = _rms(_dot(un, w_lat_ref[:, MLA_Q_RANK:]), gkv_ref[...]).astype(BF16)
        k_nope = _dot(kvn, w_uk_ref[...])
        v_ref[...] = _dot(kvn, w_uv_ref[...]).astype(BF16)
        k_pe = _rope(_dot(un, w_kpe_ref[...]), ca, sa).astype(BF16)
        for h in range(MLA_HEADS):
            lo = h * QK_PAD
            k_ref[:, lo:lo + LANES] = k_nope[:, h * LANES:(h + 1) * LANES].astype(BF16)
            k_ref[:, lo + LANES:lo + QK_PAD] = k_pe

    @pl.when(group == 1)
    def _():
        @pl.when(pl.program_id(0) == 0)
        def _():
            kmean_ref[...] = jnp.zeros_like(kmean_ref)

        un = un_ref[...]
        cm, sm = cm_ref[...], sm_ref[...]
        tm = un.shape[0]
        blk0 = (pl.program_id(0) * (tm // L)) % nb
        qm = _dot(un, w_moba_ref[:, :MOBA_WIDTH])
        km = _dot(un, w_moba_ref[:, MOBA_WIDTH:2 * MOBA_WIDTH])
        v_ref[...] = _dot(un, w_moba_ref[:, 2 * MOBA_WIDTH:]).astype(BF16)
        lane = lax.broadcasted_iota(jnp.int32, (L, LANES), 1)
        blk_row = lax.broadcasted_iota(jnp.int32, (nb, LANES), 0)
        for h in range(MOBA_HEADS):
            sl = slice(h * LANES, (h + 1) * LANES)
            lo = h * QK_PAD
            qh = _rope(qm[:, sl], cm, sm).astype(BF16)
            kh = _rope(km[:, sl], cm, sm)
            q_ref[:, lo:lo + LANES] = qh
            k_ref[:, lo:lo + LANES] = kh.astype(BF16)
            kmean = kmean_ref[:, sl]
            for part in range(tm // L):
                rows = slice(part * L, (part + 1) * L)
                blk = blk0 + part
                kmean = jnp.where(blk_row == blk, jnp.mean(kh[rows], axis=0, keepdims=True), kmean)
                q_ref[rows, lo + LANES:lo + QK_PAD] = _moba_bias(kmean, qh[rows], blk).astype(BF16)
                k_ref[rows, lo + LANES:lo + QK_PAD] = jnp.where(lane == blk, 1.0, 0.0).astype(BF16)
            kmean_ref[:, sl] = kmean


def _proj(x, g, w_lat, w_kpe, w_moba, gq, gkv, w_uq, w_uk, w_uv, ca, sa, cm, sm, *, seq, tm=512):
    T, D = x.shape
    n_in = w_lat.shape[1] + w_kpe.shape[1] + w_moba.shape[1]
    nb = seq // MOBA_BLOCK
    assert seq % tm == 0 and tm % MOBA_BLOCK == 0 and nb % SUBLANES_BF16 == 0 and nb <= LANES
    assert MLA_HEADS == MOBA_HEADS and MLA_WIDTH == MOBA_WIDTH
    wq, wv = MLA_HEADS * QK_PAD, MLA_WIDTH
    out = lambda w: pl.BlockSpec((tm, w), lambda i, grp: (i, grp))
    tab = pl.BlockSpec((tm, LANES), lambda i, grp: (i % (seq // tm), 0))
    vmem = (2 * tm * D * 4 + tm * D * 2
            + (D * n_in + MLA_Q_RANK * wq + 2 * MLA_KV_RANK * MLA_WIDTH) * 2
            + 2 * tm * (2 * wq + wv) * 2
            + 6 * tm * wq * 4
            + 8 * MIB)
    return pl.pallas_call(
        _proj_kernel,
        grid=(T // tm, 2),
        in_specs=[
            pl.BlockSpec((tm, D), lambda i, grp: (i, 0)), _const_spec((1, D)),
            _const_spec(w_lat.shape), _const_spec(w_kpe.shape), _const_spec(w_moba.shape),
            _const_spec((1, MLA_Q_RANK)), _const_spec((1, MLA_KV_RANK)),
            _const_spec((MLA_Q_RANK, wq)), _const_spec((MLA_KV_RANK, MLA_WIDTH)),
            _const_spec((MLA_KV_RANK, MLA_WIDTH)),
            tab, tab, tab, tab,
        ],
        out_specs=[out(wq), out(wq), out(wv)],
        out_shape=[
            jax.ShapeDtypeStruct((T, 2 * wq), BF16),
            jax.ShapeDtypeStruct((T, 2 * wq), BF16),
            jax.ShapeDtypeStruct((T, 2 * wv), BF16),
        ],
        scratch_shapes=[pltpu.VMEM((tm, D), BF16), pltpu.VMEM((nb, MOBA_WIDTH), F32)],
        compiler_params=pltpu.CompilerParams(
            dimension_semantics=("arbitrary", "arbitrary"), vmem_limit_bytes=vmem),
        name="proj",
    )(x, g, w_lat, w_kpe, w_moba, gq, gkv, w_uq, w_uk, w_uv, ca, sa, cm, sm)


LOG2_E = 1.4426950408889634


def _attn_kernel(q_ref, k_ref, v_ref, o_ref, m_ref, acc_ref, *, t, nsub):
    h = pl.program_id(1)
    i = pl.program_id(2)
    scale = jnp.where(h < MLA_HEADS, MLA_QK_DIM ** -0.5, MOBA_HEAD_DIM ** -0.5)
    c2 = (scale * LOG2_E).astype(F32)
    m_ref[...] = jnp.full(m_ref.shape, -jnp.inf, F32)
    acc_ref[...] = jnp.zeros_like(acc_ref)

    def logits(sub, start, tk):
        q = q_ref[sub * t:(sub + 1) * t, :]
        return _dot_nt(q, k_ref[pl.ds(pl.multiple_of(start, t), tk), :])

    def update(sub, s, start, tk, diagonal):
        parts = [pl.ds(pl.multiple_of(start + c * t, t), t) for c in range(tk // t)]
        if diagonal:
            row = lax.broadcasted_iota(jnp.int32, s.shape, 0)
            col = lax.broadcasted_iota(jnp.int32, s.shape, 1)
            s = jnp.where(col <= row, s, NEG_INF)
        chunks = [s[:, c * LANES:(c + 1) * LANES] for c in range(tk // LANES)]
        m_cur = jnp.max(functools.reduce(jnp.maximum, chunks), axis=1, keepdims=True)
        m_prev = m_ref[sub]
        m_new = jnp.maximum(m_prev, jnp.broadcast_to(m_cur, (t, LANES)) * c2)
        alpha = jnp.exp2(m_prev - m_new)
        p = [jnp.exp2(c * c2 - m_new).astype(BF16) for c in chunks]
        ones = jnp.ones((t, LANES), BF16)
        n = t // LANES
        pv = [_dot(jnp.concatenate(p[c * n:(c + 1) * n], axis=1),
                   jnp.concatenate([v_ref[r, :], ones], axis=1)) for c, r in enumerate(parts)]
        acc_ref[sub] = jnp.concatenate([alpha, alpha], axis=1) * acc_ref[sub] + sum(pv)
        m_ref[sub] = m_new

    def body(j, carry):
        start = j * 2 * t
        ss = [logits(sub, start, 2 * t) for sub in range(nsub)]
        for sub in range(nsub):
            update(sub, ss[sub], start, 2 * t, False)
        return carry

    lax.fori_loop(0, i * (nsub // 2), body, 0)
    first = i * nsub
    for kb in range(nsub):
        for sub in range(kb, nsub):
            start = (first + kb) * t
            update(sub, logits(sub, start, t), start, t, sub == kb)
    for sub in range(nsub):
        acc = acc_ref[sub]
        o_ref[sub * t:(sub + 1) * t, :] = acc[:, :V_DIM] / acc[:, V_DIM:]


def _attention(q, k, v, *, t=512, nsub=4):
    B, S, _ = q.shape
    tq = nsub * t
    assert S % tq == 0 and t % MOBA_BLOCK == 0 and nsub % 2 == 0
    kern = functools.partial(_attn_kernel, t=t, nsub=nsub)
    vmem = (2 * S * (QK_PAD + V_DIM) * 2 + 2 * tq * QK_PAD * 2 + 2 * tq * V_DIM * 4
            + tq * (LANES + 2 * V_DIM) * 4 + nsub * 6 * t * t * 4 + 8 * MIB)
    return pl.pallas_call(
        kern,
        grid=(B, HEADS, S // tq),
        in_specs=[
            pl.BlockSpec((None, tq, QK_PAD), lambda b, h, i: (b, i, h)),
            pl.BlockSpec((None, S, QK_PAD), lambda b, h, i: (b, 0, h)),
            pl.BlockSpec((None, S, V_DIM), lambda b, h, i: (b, 0, h)),
        ],
        out_specs=pl.BlockSpec((None, tq, V_DIM), lambda b, h, i: (b, i, h)),
        out_shape=jax.ShapeDtypeStruct((B, S, HEADS * V_DIM), F32),
        scratch_shapes=[pltpu.VMEM((nsub, t, LANES), F32),
                        pltpu.VMEM((nsub, t, 2 * V_DIM), F32)],
        compiler_params=pltpu.CompilerParams(
            dimension_semantics=("parallel", "parallel", "arbitrary"), vmem_limit_bytes=vmem),
        name="attention",
    )(q, k, v)


def _out_kernel(x_ref, am_ref, ga_ref, gm_ref, w_ref, gp_ref, o_ref):
    an = _rms(am_ref[:, :MLA_WIDTH], ga_ref[...]).astype(BF16)
    mn = _rms(am_ref[:, MLA_WIDTH:], gm_ref[...]).astype(BF16)
    y = _dot(jnp.concatenate([an, mn], axis=1), w_ref[...])
    o_ref[...] = x_ref[...] + _rms(y, gp_ref[...])


def _out_proj(x, am, ga, gm, w, gp, *, tm=512):
    T, D = x.shape
    W = MLA_WIDTH + MOBA_WIDTH
    tok = lambda wd: pl.BlockSpec((tm, wd), lambda i: (i, 0))
    vmem = 2 * 2 * tm * D * 4 + 2 * tm * W * 4 + W * D * 2 + 4 * tm * D * 4 + 8 * MIB
    return pl.pallas_call(
        _out_kernel,
        grid=(T // tm,),
        in_specs=[tok(D), tok(W), _const_spec((1, MLA_WIDTH)), _const_spec((1, MOBA_WIDTH)),
                  _const_spec((W, D)), _const_spec((1, D))],
        out_specs=tok(D),
        out_shape=jax.ShapeDtypeStruct((T, D), F32),
        compiler_params=pltpu.CompilerParams(
            dimension_semantics=("parallel",), vmem_limit_bytes=vmem),
        name="out_proj",
    )(x, am, ga, gm, w, gp)


def _spread_rope_cols(w):
    half = MLA_ROPE_DIM // 2
    z = jnp.zeros(w.shape[:-1] + (LANES // 2 - half,), w.dtype)
    return jnp.concatenate([w[..., :half], z, w[..., half:], z], axis=-1)


def _rope_tables(seq):
    pos = jnp.arange(seq, dtype=F32)[:, None]

    def cos_sin(dim):
        inv = 1.0 / (ROPE_THETA ** (jnp.arange(0, dim, 2, dtype=F32) / dim))
        ang = pos * inv[None, :]
        return jnp.cos(ang), jnp.sin(ang)

    c, s = cos_sin(MOBA_HEAD_DIM)
    cm = jnp.concatenate([c, c], axis=-1)
    sm = jnp.concatenate([-s, s], axis=-1)
    c, s = cos_sin(MLA_ROPE_DIM)
    z = jnp.zeros_like(c)
    ca = jnp.concatenate([c, z, c, z], axis=-1)
    sa = jnp.concatenate([-s, z, s, z], axis=-1)
    return ca, sa, cm, sm


def kernel(x, ffn1_pre_g, ffn1_w_gate, ffn1_w_up, ffn1_w_down, ffn1_post_g, mix_pre_g, w_in, mla_q_norm_g, mla_kv_norm_g, mla_w_uq, mla_w_ukv, mla_out_g, moba_out_g, w_out, mix_post_g, ffn2_pre_g, ffn2_w_gate, ffn2_w_up, ffn2_w_down, ffn2_post_g):
    B, S, D = x.shape
    depth = w_in.shape[0]
    H = MLA_HEADS
    ca, sa, cm, sm = _rope_tables(S)
    xt = x.reshape(B * S, D)
    for l in range(depth):
        o_kpe = MLA_Q_RANK + MLA_KV_RANK
        w_lat = w_in[l][:, :o_kpe].astype(BF16)
        w_kpe = _spread_rope_cols(w_in[l][:, o_kpe:o_kpe + MLA_ROPE_DIM]).astype(BF16)
        w_moba = w_in[l][:, o_kpe + MLA_ROPE_DIM:].astype(BF16)
        uq = mla_w_uq[l].reshape(MLA_Q_RANK, H, MLA_QK_DIM)
        w_uq = jnp.concatenate(
            [uq[..., :MLA_NOPE_DIM], _spread_rope_cols(uq[..., MLA_NOPE_DIM:])], axis=-1
        ).reshape(MLA_Q_RANK, H * QK_PAD).astype(BF16)
        ukv = mla_w_ukv[l].reshape(MLA_KV_RANK, H, MLA_NOPE_DIM + MLA_V_DIM)
        w_uk = ukv[..., :MLA_NOPE_DIM].reshape(MLA_KV_RANK, MLA_WIDTH).astype(BF16)
        w_uv = ukv[..., MLA_NOPE_DIM:].reshape(MLA_KV_RANK, MLA_WIDTH).astype(BF16)

        xt = _ffn(xt, ffn1_pre_g[l][None], ffn1_w_gate[l], ffn1_w_up[l], ffn1_w_down[l],
                  ffn1_post_g[l][None])
        q, k, v = _proj(xt, mix_pre_g[l][None], w_lat, w_kpe, w_moba, mla_q_norm_g[l][None],
                        mla_kv_norm_g[l][None], w_uq, w_uk, w_uv, ca, sa, cm, sm, seq=S)
        am = _attention(q.reshape(B, S, -1), k.reshape(B, S, -1), v.reshape(B, S, -1))
        xt = _out_proj(xt, am.reshape(B * S, -1), mla_out_g[l][None], moba_out_g[l][None],
                       w_out[l].astype(BF16), mix_post_g[l][None])
        xt = _ffn(xt, ffn2_pre_g[l][None], ffn2_w_gate[l], ffn2_w_up[l], ffn2_w_down[l],
                  ffn2_post_g[l][None])
    return xt.reshape(B, S, D)
```

```python
import functools

import jax
import jax.numpy as jnp
from jax import lax
from jax.experimental import pallas as pl
from jax.experimental.pallas import tpu as pltpu

D_MODEL = 2048
FFN_DIM = 5632
MLA_HEADS = 8
MLA_Q_RANK = 512
MLA_KV_RANK = 256
MLA_NOPE_DIM = 128
MLA_ROPE_DIM = 64
MLA_V_DIM = 128
MLA_QK_DIM = MLA_NOPE_DIM + MLA_ROPE_DIM
MOBA_HEADS = 8
MOBA_HEAD_DIM = 128
MOBA_BLOCK = 256
MOBA_TOPK = 3
MLA_WIDTH = MLA_HEADS * MLA_V_DIM
MOBA_WIDTH = MOBA_HEADS * MOBA_HEAD_DIM
ROPE_THETA = 10000.0
NORM_EPS = 1e-6
NEG_INF = -1e30
MASK_BIAS = -(2.0 ** 100)

LANES = 128
SUBLANES_BF16 = 16
HEADS = MLA_HEADS + MOBA_HEADS
QK_PAD = 2 * LANES
V_DIM = 128
V7X_VMEM_BYTES = 64 * 2 ** 20
MIB = 2 ** 20

BF16 = jnp.bfloat16
F32 = jnp.float32


def _rms(x, g):
    return x * lax.rsqrt(jnp.mean(x * x, axis=-1, keepdims=True) + NORM_EPS) * g


def _dot(a, b):
    return jnp.dot(a, b, preferred_element_type=F32)


def _dot_nt(a, b):
    return lax.dot_general(a, b, (((1,), (1,)), ((), ())), preferred_element_type=F32)


def _const_spec(shape):
    return pl.BlockSpec(shape, lambda *_: (0,) * len(shape), pipeline_mode=pl.Buffered(1))


def _ffn_kernel(x_ref, pre_g_ref, wg_ref, wu_ref, wd_ref, post_g_ref, o_ref, xn_ref):
    f = pl.program_id(1)
    last = pl.num_programs(1) - 1
    tm = x_ref.shape[0]
    halves = [slice(0, tm // 2), slice(tm // 2, tm)]

    def swiglu_down(xn):
        h = _dot(xn, wg_ref[...].astype(BF16))
        u = _dot(xn, wu_ref[...].astype(BF16))
        a = (h * jax.nn.sigmoid(h)) * u
        return _dot(a.astype(BF16), wd_ref[...].astype(BF16))

    @pl.when(f == 0)
    def _():
        for rows in halves:
            xn = _rms(x_ref[rows, :], pre_g_ref[...]).astype(BF16)
            xn_ref[rows, :] = xn
            o_ref[rows, :] = swiglu_down(xn)

    @pl.when((f > 0) & (f < last))
    def _():
        o_ref[...] += swiglu_down(xn_ref[...])

    @pl.when(f == last)
    def _():
        for rows in halves:
            acc = o_ref[rows, :] + swiglu_down(xn_ref[rows, :])
            o_ref[rows, :] = x_ref[rows, :] + 0.5 * _rms(acc, post_g_ref[...])


def _ffn(x, pre_g, wg, wu, wd, post_g, *, tm=1024, tf=256):
    T, D = x.shape
    F = wg.shape[1]
    vmem = (2 * 2 * tm * D * 4
            + tm * D * 2
            + 2 * 3 * D * tf * 4
            + 3 * D * tf * 2
            + 4 * tm * tf * 4 + tm * D * 4)
    vmem = min(vmem, V7X_VMEM_BYTES)
    return pl.pallas_call(
        _ffn_kernel,
        grid=(T // tm, F // tf),
        in_specs=[
            pl.BlockSpec((tm, D), lambda i, f: (i, 0)),
            pl.BlockSpec((1, D), lambda i, f: (0, 0)),
            pl.BlockSpec((D, tf), lambda i, f: (0, f)),
            pl.BlockSpec((D, tf), lambda i, f: (0, f)),
            pl.BlockSpec((tf, D), lambda i, f: (f, 0)),
            pl.BlockSpec((1, D), lambda i, f: (0, 0)),
        ],
        out_specs=pl.BlockSpec((tm, D), lambda i, f: (i, 0)),
        out_shape=jax.ShapeDtypeStruct((T, D), F32),
        scratch_shapes=[pltpu.VMEM((tm, D), BF16)],
        compiler_params=pltpu.CompilerParams(
            dimension_semantics=("parallel", "arbitrary"), vmem_limit_bytes=vmem),
        name="ffn",
    )(x, pre_g, wg, wu, wd, post_g)


def _rope(x, c, s):
    return x * c + pltpu.roll(x, LANES // 2, axis=1) * s


def _moba_bias(kmean, qh, blk):
    nb = kmean.shape[0]
    gate = _dot_nt(kmean.astype(BF16), qh)
    n = lax.broadcasted_iota(jnp.int32, gate.shape, 0)
    cand = jnp.where(n < blk, gate, -jnp.inf)
    allowed = n == blk
    for _ in range(min(MOBA_TOPK, nb - 1)):
        best = jnp.max(cand, axis=0, keepdims=True)
        first = jnp.min(jnp.where(cand == best, n, nb), axis=0, keepdims=True)
        pick = (n == first) & (best > -jnp.inf)
        allowed = allowed | pick
        cand = jnp.where(pick, -jnp.inf, cand)
    bias = jnp.where(allowed, 0.0, MASK_BIAS)
    bias = jnp.concatenate([bias, jnp.zeros((LANES - nb, bias.shape[1]), F32)], axis=0)
    return bias.T


def _proj_kernel(x_ref, g_ref, w_lat_ref, w_kpe_ref, w_moba_ref, gq_ref, gkv_ref,
                 w_uq_ref, w_uk_ref, w_uv_ref, ca_ref, sa_ref, cm_ref, sm_ref,
                 q_ref, k_ref, v_ref, un_ref, kmean_ref):
    nb = kmean_ref.shape[0]
    L = MOBA_BLOCK
    group = pl.program_id(1)

    @pl.when(group == 0)
    def _():
        un = _rms(x_ref[...], g_ref[...]).astype(BF16)
        un_ref[...] = un
        ca, sa = ca_ref[...], sa_ref[...]

        qn = _rms(_dot(un, w_lat_ref[:, :MLA_Q_RANK]), gq_ref[...]).astype(BF16)
        q = _dot(qn, w_uq_ref[...])
        for h in range(MLA_HEADS):
            lo = h * QK_PAD
            q_ref[:, lo:lo + LANES] = q[:, lo:lo + LANES].astype(BF16)
            q_ref[:, lo + LANES:lo + QK_PAD] = _rope(
                q[:, lo + LANES:lo + QK_PAD], ca, sa).astype(BF16)

        kvn = _rms(_dot(un, w_lat_ref[:, MLA_Q_RANK:]), gkv_ref[...]).astype(BF16)
        k_nope = _dot(kvn, w_uk_ref[...])
        v_ref[...] = _dot(kvn, w_uv_ref[...]).astype(BF16)
        k_pe = _rope(_dot(un, w_kpe_ref[...]), ca, sa).astype(BF16)
        for h in range(MLA_HEADS):
            lo = h * QK_PAD
            k_ref[:, lo:lo + LANES] = k_nope[:, h * LANES:(h + 1) * LANES].astype(BF16)
            k_ref[:, lo + LANES:lo + QK_PAD] = k_pe

    @pl.when(group == 1)
    def _():
        @pl.when(pl.program_id(0) == 0)
        def _():
            kmean_ref[...] = jnp.zeros_like(kmean_ref)

        un = un_ref[...]
        cm, sm = cm_ref[...], sm_ref[...]
        tm = un.shape[0]
        blk0 = (pl.program_id(0) * (tm // L)) % nb
        qm = _dot(un, w_moba_ref[:, :MOBA_WIDTH])
        km = _dot(un, w_moba_ref[:, MOBA_WIDTH:2 * MOBA_WIDTH])
        v_ref[...] = _dot(un, w_moba_ref[:, 2 * MOBA_WIDTH:]).astype(BF16)
        lane = lax.broadcasted_iota(jnp.int32, (L, LANES), 1)
        blk_row = lax.broadcasted_iota(jnp.int32, (nb, LANES), 0)
        for h in range(MOBA_HEADS):
            sl = slice(h * LANES, (h + 1) * LANES)
            lo = h * QK_PAD
            qh = _rope(qm[:, sl], cm, sm).astype(BF16)
            kh = _rope(km[:, sl], cm, sm)
            q_ref[:, lo:lo + LANES] = qh
            k_ref[:, lo:lo + LANES] = kh.astype(BF16)
            kmean = kmean_ref[:, sl]
            for part in range(tm // L):
                rows = slice(part * L, (part + 1) * L)
                blk = blk0 + part
                kmean = jnp.where(blk_row == blk, jnp.mean(kh[rows], axis=0, keepdims=True), kmean)
                q_ref[rows, lo + LANES:lo + QK_PAD] = _moba_bias(kmean, qh[rows], blk).astype(BF16)
                k_ref[rows, lo + LANES:lo + QK_PAD] = jnp.where(lane == blk, 1.0, 0.0).astype(BF16)
            kmean_ref[:, sl] = kmean


def _proj(x, g, w_lat, w_kpe, w_moba, gq, gkv, w_uq, w_uk, w_uv, ca, sa, cm, sm, *, seq, tm=512):
    T, D = x.shape
    n_in = w_lat.shape[1] + w_kpe.shape[1] + w_moba.shape[1]
    nb = seq // MOBA_BLOCK
    assert seq % tm == 0 and tm % MOBA_BLOCK == 0 and nb % SUBLANES_BF16 == 0 and nb <= LANES
    assert MLA_HEADS == MOBA_HEADS and MLA_WIDTH == MOBA_WIDTH
    wq, wv = MLA_HEADS * QK_PAD, MLA_WIDTH
    out = lambda w: pl.BlockSpec((tm, w), lambda i, grp: (i, grp))
    tab = pl.BlockSpec((tm, LANES), lambda i, grp: (i % (seq // tm), 0))
    vmem = (2 * tm * D * 4 + tm * D * 2
            + (D * n_in + MLA_Q_RANK * wq + 2 * MLA_KV_RANK * MLA_WIDTH) * 2
            + 2 * tm * (2 * wq + wv) * 2
            + 6 * tm * wq * 4
            + 8 * MIB)
    return pl.pallas_call(
        _proj_kernel,
        grid=(T // tm, 2),
        in_specs=[
            pl.BlockSpec((tm, D), lambda i, grp: (i, 0)), _const_spec((1, D)),
            _const_spec(w_lat.shape), _const_spec(w_kpe.shape), _const_spec(w_moba.shape),
            _const_spec((1, MLA_Q_RANK)), _const_spec((1, MLA_KV_RANK)),
            _const_spec((MLA_Q_RANK, wq)), _const_spec((MLA_KV_RANK, MLA_WIDTH)),
            _const_spec((MLA_KV_RANK, MLA_WIDTH)),
            tab, tab, tab, tab,
        ],
        out_specs=[out(wq), out(wq), out(wv)],
        out_shape=[
            jax.ShapeDtypeStruct((T, 2 * wq), BF16),
            jax.ShapeDtypeStruct((T, 2 * wq), BF16),
            jax.ShapeDtypeStruct((T, 2 * wv), BF16),
        ],
        scratch_shapes=[pltpu.VMEM((tm, D), BF16), pltpu.VMEM((nb, MOBA_WIDTH), F32)],
        compiler_params=pltpu.CompilerParams(
            dimension_semantics=("arbitrary", "arbitrary"), vmem_limit_bytes=vmem),
        name="proj",
    )(x, g, w_lat, w_kpe, w_moba, gq, gkv, w_uq, w_uk, w_uv, ca, sa, cm, sm)


LOG2_E = 1.4426950408889634


def _attn_kernel(q_ref, k_ref, v_ref, o_ref, m_ref, acc_ref, *, t, nsub):
    h = pl.program_id(1)
    i = pl.program_id(2)
    scale = jnp.where(h < MLA_HEADS, MLA_QK_DIM ** -0.5, MOBA_HEAD_DIM ** -0.5)
    c2 = (scale * LOG2_E).astype(F32)
    m_ref[...] = jnp.full(m_ref.shape, -jnp.inf, F32)
    acc_ref[...] = jnp.zeros_like(acc_ref)

    def logits(sub, start, tk):
        q = q_ref[sub * t:(sub + 1) * t, :]
        return _dot_nt(q, k_ref[pl.ds(pl.multiple_of(start, t), tk), :])

    def update(sub, s, start, tk, diagonal):
        parts = [pl.ds(pl.multiple_of(start + c * t, t), t) for c in range(tk // t)]
        if diagonal:
            row = lax.broadcasted_iota(jnp.int32, s.shape, 0)
            col = lax.broadcasted_iota(jnp.int32, s.shape, 1)
            s = jnp.where(col <= row, s, NEG_INF)
        chunks = [s[:, c * LANES:(c + 1) * LANES] for c in range(tk // LANES)]
        m_cur = jnp.max(functools.reduce(jnp.maximum, chunks), axis=1, keepdims=True)
        m_prev = m_ref[sub]
        m_new = jnp.maximum(m_prev, jnp.broadcast_to(m_cur, (t, LANES)) * c2)
        alpha = jnp.exp2(m_prev - m_new)
        p = [jnp.exp2(c * c2 - m_new).astype(BF16) for c in chunks]
        ones = jnp.ones((t, LANES), BF16)
        n = t // LANES
        pv = [_dot(jnp.concatenate(p[c * n:(c + 1) * n], axis=1),
                   jnp.concatenate([v_ref[r, :], ones], axis=1)) for c, r in enumerate(parts)]
        acc_ref[sub] = jnp.concatenate([alpha, alpha], axis=1) * acc_ref[sub] + sum(pv)
        m_ref[sub] = m_new

    def body(j, carry):
        start = j * 2 * t
        ss = [logits(sub, start, 2 * t) for sub in range(nsub)]
        for sub in range(nsub):
            update(sub, ss[sub], start, 2 * t, False)
        return carry

    lax.fori_loop(0, i * (nsub // 2), body, 0)
    first = i * nsub
    for kb in range(nsub):
        for sub in range(kb, nsub):
            start = (first + kb) * t
            update(sub, logits(sub, start, t), start, t, sub == kb)
    for sub in range(nsub):
        acc = acc_ref[sub]
        o_ref[sub * t:(sub + 1) * t, :] = acc[:, :V_DIM] / acc[:, V_DIM:]


def _attention(q, k, v, *, t=512, nsub=4):
    B, S, _ = q.shape
    tq = nsub * t
    assert S % tq == 0 and t % MOBA_BLOCK == 0 and nsub % 2 == 0
    kern = functools.partial(_attn_kernel, t=t, nsub=nsub)
    vmem = (2 * S * (QK_PAD + V_DIM) * 2 + 2 * tq * QK_PAD * 2 + 2 * tq * V_DIM * 4
            + tq * (LANES + 2 * V_DIM) * 4 + nsub * 6 * t * t * 4 + 8 * MIB)
    return pl.pallas_call(
        kern,
        grid=(B, HEADS, S // tq),
        in_specs=[
            pl.BlockSpec((None, tq, QK_PAD), lambda b, h, i: (b, i, h)),
            pl.BlockSpec((None, S, QK_PAD), lambda b, h, i: (b, 0, h)),
            pl.BlockSpec((None, S, V_DIM), lambda b, h, i: (b, 0, h)),
        ],
        out_specs=pl.BlockSpec((None, tq, V_DIM), lambda b, h, i: (b, i, h)),
        out_shape=jax.ShapeDtypeStruct((B, S, HEADS * V_DIM), F32),
        scratch_shapes=[pltpu.VMEM((nsub, t, LANES), F32),
                        pltpu.VMEM((nsub, t, 2 * V_DIM), F32)],
        compiler_params=pltpu.CompilerParams(
            dimension_semantics=("parallel", "parallel", "arbitrary"), vmem_limit_bytes=vmem),
        name="attention",
    )(q, k, v)


def _out_kernel(x_ref, am_ref, ga_ref, gm_ref, w_ref, gp_ref, o_ref):
    an = _rms(am_ref[:, :MLA_WIDTH], ga_ref[...]).astype(BF16)
    mn = _rms(am_ref[:, MLA_WIDTH:], gm_ref[...]).astype(BF16)
    y = _dot(jnp.concatenate([an, mn], axis=1), w_ref[...])
    o_ref[...] = x_ref[...] + _rms(y, gp_ref[...])


def _out_proj(x, am, ga, gm, w, gp, *, tm=512):
    T, D = x.shape
    W = MLA_WIDTH + MOBA_WIDTH
    tok = lambda wd: pl.BlockSpec((tm, wd), lambda i: (i, 0))
    vmem = 2 * 2 * tm * D * 4 + 2 * tm * W * 4 + W * D * 2 + 4 * tm * D * 4 + 8 * MIB
    return pl.pallas_call(
        _out_kernel,
        grid=(T // tm,),
        in_specs=[tok(D), tok(W), _const_spec((1, MLA_WIDTH)), _const_spec((1, MOBA_WIDTH)),
                  _const_spec((W, D)), _const_spec((1, D))],
        out_specs=tok(D),
        out_shape=jax.ShapeDtypeStruct((T, D), F32),
        compiler_params=pltpu.CompilerParams(
            dimension_semantics=("parallel",), vmem_limit_bytes=vmem),
        name="out_proj",
    )(x, am, ga, gm, w, gp)


def _spread_rope_cols(w):
    half = MLA_ROPE_DIM // 2
    z = jnp.zeros(w.shape[:-1] + (LANES // 2 - half,), w.dtype)
    return jnp.concatenate([w[..., :half], z, w[..., half:], z], axis=-1)


def _rope_tables(seq):
    pos = jnp.arange(seq, dtype=F32)[:, None]

    def cos_sin(dim):
        inv = 1.0 / (ROPE_THETA ** (jnp.arange(0, dim, 2, dtype=F32) / dim))
        ang = pos * inv[None, :]
        return jnp.cos(ang), jnp.sin(ang)

    c, s = cos_sin(MOBA_HEAD_DIM)
    cm = jnp.concatenate([c, c], axis=-1)
    sm = jnp.concatenate([-s, s], axis=-1)
    c, s = cos_sin(MLA_ROPE_DIM)
    z = jnp.zeros_like(c)
    ca = jnp.concatenate([c, z, c, z], axis=-1)
    sa = jnp.concatenate([-s, z, s, z], axis=-1)
    return ca, sa, cm, sm


def kernel(x, ffn1_pre_g, ffn1_w_gate, ffn1_w_up, ffn1_w_down, ffn1_post_g, mix_pre_g, w_in, mla_q_norm_g, mla_kv_norm_g, mla_w_uq, mla_w_ukv, mla_out_g, moba_out_g, w_out, mix_post_g, ffn2_pre_g, ffn2_w_gate, ffn2_w_up, ffn2_w_down, ffn2_post_g):
    B, S, D = x.shape
    depth = w_in.shape[0]
    H = MLA_HEADS
    ca, sa, cm, sm = _rope_tables(S)
    xt = x.reshape(B * S, D)
    for l in range(depth):
        o_kpe = MLA_Q_RANK + MLA_KV_RANK
        w_lat = w_in[l][:, :o_kpe].astype(BF16)
        w_kpe = _spread_rope_cols(w_in[l][:, o_kpe:o_kpe + MLA_ROPE_DIM]).astype(BF16)
        w_moba = w_in[l][:, o_kpe + MLA_ROPE_DIM:].astype(BF16)
        uq = mla_w_uq[l].reshape(MLA_Q_RANK, H, MLA_QK_DIM)
        w_uq = jnp.concatenate(
            [uq[..., :MLA_NOPE_DIM], _spread_rope_cols(uq[..., MLA_NOPE_DIM:])], axis=-1
        ).reshape(MLA_Q_RANK, H * QK_PAD).astype(BF16)
        ukv = mla_w_ukv[l].reshape(MLA_KV_RANK, H, MLA_NOPE_DIM + MLA_V_DIM)
        w_uk = ukv[..., :MLA_NOPE_DIM].reshape(MLA_KV_RANK, MLA_WIDTH).astype(BF16)
        w_uv = ukv[..., MLA_NOPE_DIM:].reshape(MLA_KV_RANK, MLA_WIDTH).astype(BF16)

        xt = _ffn(xt, ffn1_pre_g[l][None], ffn1_w_gate[l], ffn1_w_up[l], ffn1_w_down[l],
                  ffn1_post_g[l][None])
        q, k, v = _proj(xt, mix_pre_g[l][None], w_lat, w_kpe, w_moba, mla_q_norm_g[l][None],
                        mla_kv_norm_g[l][None], w_uq, w_uk, w_uv, ca, sa, cm, sm, seq=S)
        am = _attention(q.reshape(B, S, -1), k.reshape(B, S, -1), v.reshape(B, S, -1))
        xt = _out_proj(xt, am.reshape(B * S, -1), mla_out_g[l][None], moba_out_g[l][None],
                       w_out[l].astype(BF16), mix_post_g[l][None])
        xt = _ffn(xt, ffn2_pre_g[l][None], ffn2_w_gate[l], ffn2_w_up[l], ffn2_w_down[l],
                  ffn2_post_g[l][None])
    return xt.reshape(B, S, D)
```

```python
import functools

import jax
import jax.numpy as jnp
from jax import lax
from jax.experimental import pallas as pl
from jax.experimental.pallas import tpu as pltpu

D_MODEL = 2048
FFN_DIM = 5632
MLA_HEADS = 8
MLA_Q_RANK = 512
MLA_KV_RANK = 256
MLA_NOPE_DIM = 128
MLA_ROPE_DIM = 64
MLA_V_DIM = 128
MLA_QK_DIM = MLA_NOPE_DIM + MLA_ROPE_DIM
MOBA_HEADS = 8
MOBA_HEAD_DIM = 128
MOBA_BLOCK = 256
MOBA_TOPK = 3
MLA_WIDTH = MLA_HEADS * MLA_V_DIM
MOBA_WIDTH = MOBA_HEADS * MOBA_HEAD_DIM
ROPE_THETA = 10000.0
NORM_EPS = 1e-6
NEG_INF = -1e30
MASK_BIAS = -(2.0 ** 100)

LANES = 128
SUBLANES_BF16 = 16
HEADS = MLA_HEADS + MOBA_HEADS
QK_PAD = 2 * LANES
V_DIM = 128
V7X_VMEM_BYTES = 64 * 2 ** 20
MIB = 2 ** 20

BF16 = jnp.bfloat16
F32 = jnp.float32


def _rms(x, g):
    return x * lax.rsqrt(jnp.mean(x * x, axis=-1, keepdims=True) + NORM_EPS) * g


def _dot(a, b):
    return jnp.dot(a, b, preferred_element_type=F32)


def _dot_nt(a, b):
    return lax.dot_general(a, b, (((1,), (1,)), ((), ())), preferred_element_type=F32)


def _const_spec(shape):
    return pl.BlockSpec(shape, lambda *_: (0,) * len(shape), pipeline_mode=pl.Buffered(1))


def _ffn_kernel(x_ref, pre_g_ref, wg_ref, wu_ref, wd_ref, post_g_ref, o_ref, xn_ref):
    f = pl.program_id(1)
    last = pl.num_programs(1) - 1
    tm = x_ref.shape[0]
    halves = [slice(0, tm // 2), slice(tm // 2, tm)]

    def swiglu_down(xn):
        h = _dot(xn, wg_ref[...].astype(BF16))
        u = _dot(xn, wu_ref[...].astype(BF16))
        a = (h * jax.nn.sigmoid(h)) * u
        return _dot(a.astype(BF16), wd_ref[...].astype(BF16))

    @pl.when(f == 0)
    def _():
        for rows in halves:
            xn = _rms(x_ref[rows, :], pre_g_ref[...]).astype(BF16)
            xn_ref[rows, :] = xn
            o_ref[rows, :] = swiglu_down(xn)

    @pl.when((f > 0) & (f < last))
    def _():
        o_ref[...] += swiglu_down(xn_ref[...])

    @pl.when(f == last)
    def _():
        for rows in halves:
            acc = o_ref[rows, :] + swiglu_down(xn_ref[rows, :])
            o_ref[rows, :] = x_ref[rows, :] + 0.5 * _rms(acc, post_g_ref[...])


def _ffn(x, pre_g, wg, wu, wd, post_g, *, tm=1024, tf=256):
    T, D = x.shape
    F = wg.shape[1]
    vmem = (2 * 2 * tm * D * 4
            + tm * D * 2
            + 2 * 3 * D * tf * 4
            + 3 * D * tf * 2
            + 4 * tm * tf * 4 + tm * D * 4)
    vmem = min(vmem, V7X_VMEM_BYTES)
    return pl.pallas_call(
        _ffn_kernel,
        grid=(T // tm, F // tf),
        in_specs=[
            pl.BlockSpec((tm, D), lambda i, f: (i, 0)),
            pl.BlockSpec((1, D), lambda i, f: (0, 0)),
            pl.BlockSpec((D, tf), lambda i, f: (0, f)),
            pl.BlockSpec((D, tf), lambda i, f: (0, f)),
            pl.BlockSpec((tf, D), lambda i, f: (f, 0)),
            pl.BlockSpec((1, D), lambda i, f: (0, 0)),
        ],
        out_specs=pl.BlockSpec((tm, D), lambda i, f: (i, 0)),
        out_shape=jax.ShapeDtypeStruct((T, D), F32),
        scratch_shapes=[pltpu.VMEM((tm, D), BF16)],
        compiler_params=pltpu.CompilerParams(
            dimension_semantics=("parallel", "arbitrary"), vmem_limit_bytes=vmem),
        name="ffn",
    )(x, pre_g, wg, wu, wd, post_g)


def _rope(x, c, s):
    return x * c + pltpu.roll(x, LANES // 2, axis=1) * s


def _moba_bias(kmean, qh, blk):
    nb = kmean.shape[0]
    gate = _dot_nt(kmean.astype(BF16), qh)
    n = lax.broadcasted_iota(jnp.int32, gate.shape, 0)
    cand = jnp.where(n < blk, gate, -jnp.inf)
    allowed = n == blk
    for _ in range(min(MOBA_TOPK, nb - 1)):
        best = jnp.max(cand, axis=0, keepdims=True)
        first = jnp.min(jnp.where(cand == best, n, nb), axis=0, keepdims=True)
        pick = (n == first) & (best > -jnp.inf)
        allowed = allowed | pick
        cand = jnp.where(pick, -jnp.inf, cand)
    bias = jnp.where(allowed, 0.0, MASK_BIAS)
    bias = jnp.concatenate([bias, jnp.zeros((LANES - nb, bias.shape[1]), F32)], axis=0)
    return bias.T


def _proj_kernel(x_ref, g_ref, w_lat_ref, w_kpe_ref, w_moba_ref, gq_ref, gkv_ref,
                 w_uq_ref, w_uk_ref, w_uv_ref, ca_ref, sa_ref, cm_ref, sm_ref,
                 q_ref, k_ref, v_ref, un_ref, kmean_ref):
    nb = kmean_ref.shape[0]
    L = MOBA_BLOCK
    group = pl.program_id(1)

    @pl.when(group == 0)
    def _():
        un = _rms(x_ref[...], g_ref[...]).astype(BF16)
        un_ref[...] = un
        ca, sa = ca_ref[...], sa_ref[...]

        qn = _rms(_dot(un, w_lat_ref[:, :MLA_Q_RANK]), gq_ref[...]).astype(BF16)
        q = _dot(qn, w_uq_ref[...])
        for h in range(MLA_HEADS):
            lo = h * QK_PAD
            q_ref[:, lo:lo + LANES] = q[:, lo:lo + LANES].astype(BF16)
            q_ref[:, lo + LANES:lo + QK_PAD] = _rope(
                q[:, lo + LANES:lo + QK_PAD], ca, sa).astype(BF16)

        kvn = _rms(_dot(un, w_lat_ref[:, MLA_Q_RANK:]), gkv_ref[...]).astype(BF16)
        k_nope = _dot(kvn, w_uk_ref[...])
        v_ref[...] = _dot(kvn, w_uv_ref[...]).astype(BF16)
        k_pe = _rope(_dot(un, w_kpe_ref[...]), ca, sa).astype(BF16)
        for h in range(MLA_HEADS):
            lo = h * QK_PAD
            k_ref[:, lo:lo + LANES] = k_nope[:, h * LANES:(h + 1) * LANES].astype(BF16)
            k_ref[:, lo + LANES:lo + QK_PAD] = k_pe

    @pl.when(group == 1)
    def _():
        @pl.when(pl.program_id(0) == 0)
        def _():
            kmean_ref[...] = jnp.zeros_like(kmean_ref)

        un = un_ref[...]
        cm, sm = cm_ref[...], sm_ref[...]
        tm = un.shape[0]
        blk0 = (pl.program_id(0) * (tm // L)) % nb
        qm = _dot(un, w_moba_ref[:, :MOBA_WIDTH])
        km = _dot(un, w_moba_ref[:, MOBA_WIDTH:2 * MOBA_WIDTH])
        v_ref[...] = _dot(un, w_moba_ref[:, 2 * MOBA_WIDTH:]).astype(BF16)
        lane = lax.broadcasted_iota(jnp.int32, (L, LANES), 1)
        blk_row = lax.broadcasted_iota(jnp.int32, (nb, LANES), 0)
        for h in range(MOBA_HEADS):
            sl = slice(h * LANES, (h + 1) * LANES)
            lo = h * QK_PAD
            qh = _rope(qm[:, sl], cm, sm).astype(BF16)
            kh = _rope(km[:, sl], cm, sm)
            q_ref[:, lo:lo + LANES] = qh
            k_ref[:, lo:lo + LANES] = kh.astype(BF16)
            kmean = kmean_ref[:, sl]
            for part in range(tm // L):
                rows = slice(part * L, (part + 1) * L)
                blk = blk0 + part
                kmean = jnp.where(blk_row == blk, jnp.mean(kh[rows], axis=0, keepdims=True), kmean)
                q_ref[rows, lo + LANES:lo + QK_PAD] = _moba_bias(kmean, qh[rows], blk).astype(BF16)
                k_ref[rows, lo + LANES:lo + QK_PAD] = jnp.where(lane == blk, 1.0, 0.0).astype(BF16)
            kmean_ref[:, sl] = kmean


def _proj(x, g, w_lat, w_kpe, w_moba, gq, gkv, w_uq, w_uk, w_uv, ca, sa, cm, sm, *, seq, tm=512):
    T, D = x.shape
    n_in = w_lat.shape[1] + w_kpe.shape[1] + w_moba.shape[1]
    nb = seq // MOBA_BLOCK
    assert seq % tm == 0 and tm % MOBA_BLOCK == 0 and nb % SUBLANES_BF16 == 0 and nb <= LANES
    assert MLA_HEADS == MOBA_HEADS and MLA_WIDTH == MOBA_WIDTH
    wq, wv = MLA_HEADS * QK_PAD, MLA_WIDTH
    out = lambda w: pl.BlockSpec((tm, w), lambda i, grp: (i, grp))
    tab = pl.BlockSpec((tm, LANES), lambda i, grp: (i % (seq // tm), 0))
    vmem = (2 * tm * D * 4 + tm * D * 2
            + (D * n_in + MLA_Q_RANK * wq + 2 * MLA_KV_RANK * MLA_WIDTH) * 2
            + 2 * tm * (2 * wq + wv) * 2
            + 6 * tm * wq * 4
            + 8 * MIB)
    return pl.pallas_call(
        _proj_kernel,
        grid=(T // tm, 2),
        in_specs=[
            pl.BlockSpec((tm, D), lambda i, grp: (i, 0)), _const_spec((1, D)),
            _const_spec(w_lat.shape), _const_spec(w_kpe.shape), _const_spec(w_moba.shape),
            _const_spec((1, MLA_Q_RANK)), _const_spec((1, MLA_KV_RANK)),
            _const_spec((MLA_Q_RANK, wq)), _const_spec((MLA_KV_RANK, MLA_WIDTH)),
            _const_spec((MLA_KV_RANK, MLA_WIDTH)),
            tab, tab, tab, tab,
        ],
        out_specs=[out(wq), out(wq), out(wv)],
        out_shape=[
            jax.ShapeDtypeStruct((T, 2 * wq), BF16),
            jax.ShapeDtypeStruct((T, 2 * wq), BF16),
            jax.ShapeDtypeStruct((T, 2 * wv), BF16),
        ],
        scratch_shapes=[pltpu.VMEM((tm, D), BF16), pltpu.VMEM((nb, MOBA_WIDTH), F32)],
        compiler_params=pltpu.CompilerParams(
            dimension_semantics=("arbitrary", "arbitrary"), vmem_limit_bytes=vmem),
        name="proj",
    )(x, g, w_lat, w_kpe, w_moba, gq, gkv, w_uq, w_uk, w_uv, ca, sa, cm, sm)


LOG2_E = 1.4426950408889634


def _attn_kernel(q_ref, k_ref, v_ref, o_ref, m_ref, acc_ref, *, t, nsub):
    h = pl.program_id(1)
    i = pl.program_id(2)
    scale = jnp.where(h < MLA_HEADS, MLA_QK_DIM ** -0.5, MOBA_HEAD_DIM ** -0.5)
    c2 = (scale * LOG2_E).astype(F32)
    m_ref[...] = jnp.full(m_ref.shape, -jnp.inf, F32)
    acc_ref[...] = jnp.zeros_like(acc_ref)

    def logits(sub, start, tk):
        q = q_ref[sub * t:(sub + 1) * t, :]
        return _dot_nt(q, k_ref[pl.ds(pl.multiple_of(start, t), tk), :])

    def update(sub, s, start, tk, diagonal):
        parts = [pl.ds(pl.multiple_of(start + c * t, t), t) for c in range(tk // t)]
        if diagonal:
            row = lax.broadcasted_iota(jnp.int32, s.shape, 0)
            col = lax.broadcasted_iota(jnp.int32, s.shape, 1)
            s = jnp.where(col <= row, s, NEG_INF)
        chunks = [s[:, c * LANES:(c + 1) * LANES] for c in range(tk // LANES)]
        m_cur = jnp.max(functools.reduce(jnp.maximum, chunks), axis=1, keepdims=True)
        m_prev = m_ref[sub]
        m_new = jnp.maximum(m_prev, jnp.broadcast_to(m_cur, (t, LANES)) * c2)
        alpha = jnp.exp2(m_prev - m_new)
        p = [jnp.exp2(c * c2 - m_new).astype(BF16) for c in chunks]
        ones = jnp.ones((t, LANES), BF16)
        n = t // LANES
        pv = [_dot(jnp.concatenate(p[c * n:(c + 1) * n], axis=1),
                   jnp.concatenate([v_ref[r, :], ones], axis=1)) for c, r in enumerate(parts)]
        acc_ref[sub] = jnp.concatenate([alpha, alpha], axis=1) * acc_ref[sub] + sum(pv)
        m_ref[sub] = m_new

    def body(j, carry):
        start = j * 2 * t
        ss = [logits(sub, start, 2 * t) for sub in range(nsub)]
        for sub in range(nsub):
            update(sub, ss[sub], start, 2 * t, False)
        return carry

    lax.fori_loop(0, i * (nsub // 2), body, 0)
    first = i * nsub
    for kb in range(nsub):
        for sub in range(kb, nsub):
            start = (first + kb) * t
            update(sub, logits(sub, start, t), start, t, sub == kb)
    for sub in range(nsub):
        acc = acc_ref[sub]
        o_ref[sub * t:(sub + 1) * t, :] = acc[:, :V_DIM] / acc[:, V_DIM:]


def _attention(q, k, v, *, t=512, nsub=4):
    B, S, _ = q.shape
    tq = nsub * t
    assert S % tq == 0 and t % MOBA_BLOCK == 0 and nsub % 2 == 0
    kern = functools.partial(_attn_kernel, t=t, nsub=nsub)
    vmem = (2 * S * (QK_PAD + V_DIM) * 2 + 2 * tq * QK_PAD * 2 + 2 * tq * V_DIM * 4
            + tq * (LANES + 2 * V_DIM) * 4 + nsub * 6 * t * t * 4 + 8 * MIB)
    return pl.pallas_call(
        kern,
        grid=(B, HEADS, S // tq),
        in_specs=[
            pl.BlockSpec((None, tq, QK_PAD), lambda b, h, i: (b, i, h)),
            pl.BlockSpec((None, S, QK_PAD), lambda b, h, i: (b, 0, h)),
            pl.BlockSpec((None, S, V_DIM), lambda b, h, i: (b, 0, h)),
        ],
        out_specs=pl.BlockSpec((None, tq, V_DIM), lambda b, h, i: (b, i, h)),
        out_shape=jax.ShapeDtypeStruct((B, S, HEADS * V_DIM), F32),
        scratch_shapes=[pltpu.VMEM((nsub, t, LANES), F32),
                        pltpu.VMEM((nsub, t, 2 * V_DIM), F32)],
        compiler_params=pltpu.CompilerParams(
            dimension_semantics=("parallel", "parallel", "arbitrary"), vmem_limit_bytes=vmem),
        name="attention",
    )(q, k, v)


def _out_kernel(x_ref, am_ref, ga_ref, gm_ref, w_ref, gp_ref, o_ref):
    an = _rms(am_ref[:, :MLA_WIDTH], ga_ref[...]).astype(BF16)
    mn = _rms(am_ref[:, MLA_WIDTH:], gm_ref[...]).astype(BF16)
    y = _dot(jnp.concatenate([an, mn], axis=1), w_ref[...])
    o_ref[...] = x_ref[...] + _rms(y, gp_ref[...])


def _out_proj(x, am, ga, gm, w, gp, *, tm=512):
    T, D = x.shape
    W = MLA_WIDTH + MOBA_WIDTH
    tok = lambda wd: pl.BlockSpec((tm, wd), lambda i: (i, 0))
    vmem = 2 * 2 * tm * D * 4 + 2 * tm * W * 4 + W * D * 2 + 4 * tm * D * 4 + 8 * MIB
    return pl.pallas_call(
        _out_kernel,
        grid=(T // tm,),
        in_specs=[tok(D), tok(W), _const_spec((1, MLA_WIDTH)), _const_spec((1, MOBA_WIDTH)),
                  _const_spec((W, D)), _const_spec((1, D))],
        out_specs=tok(D),
        out_shape=jax.ShapeDtypeStruct((T, D), F32),
        compiler_params=pltpu.CompilerParams(
            dimension_semantics=("parallel",), vmem_limit_bytes=vmem),
        name="out_proj",
    )(x, am, ga, gm, w, gp)


def _split_w_in_kernel(w_ref, lat_ref, kpe_ref, moba_ref):
    o_kpe = MLA_Q_RANK + MLA_KV_RANK
    half = MLA_ROPE_DIM // 2
    w = w_ref[...]
    lat_ref[...] = w[:, :o_kpe].astype(BF16)
    z = jnp.zeros((w.shape[0], LANES // 2 - half), F32)
    kpe = jnp.concatenate([w[:, o_kpe:o_kpe + half], z,
                           w[:, o_kpe + half:o_kpe + MLA_ROPE_DIM], z], axis=1)
    kpe_ref[...] = kpe.astype(BF16)
    moba_ref[...] = w[:, o_kpe + MLA_ROPE_DIM:].astype(BF16)


def _split_w_in(w, *, tr=256):
    D, n = w.shape
    n_lat = MLA_Q_RANK + MLA_KV_RANK
    n_moba = n - n_lat - MLA_ROPE_DIM
    row = lambda width: pl.BlockSpec((tr, width), lambda i: (i, 0))
    return pl.pallas_call(
        _split_w_in_kernel,
        grid=(D // tr,),
        in_specs=[row(n)],
        out_specs=[row(n_lat), row(LANES), row(n_moba)],
        out_shape=[jax.ShapeDtypeStruct((D, n_lat), BF16), jax.ShapeDtypeStruct((D, LANES), BF16),
                   jax.ShapeDtypeStruct((D, n_moba), BF16)],
        compiler_params=pltpu.CompilerParams(dimension_semantics=("parallel",)),
        name="split_w_in",
    )(w)


def _spread_rope_cols(w):
    half = MLA_ROPE_DIM // 2
    z = jnp.zeros(w.shape[:-1] + (LANES // 2 - half,), w.dtype)
    return jnp.concatenate([w[..., :half], z, w[..., half:], z], axis=-1)


def _rope_tables(seq):
    pos = jnp.arange(seq, dtype=F32)[:, None]

    def cos_sin(dim):
        inv = 1.0 / (ROPE_THETA ** (jnp.arange(0, dim, 2, dtype=F32) / dim))
        ang = pos * inv[None, :]
        return jnp.cos(ang), jnp.sin(ang)

    c, s = cos_sin(MOBA_HEAD_DIM)
    cm = jnp.concatenate([c, c], axis=-1)
    sm = jnp.concatenate([-s, s], axis=-1)
    c, s = cos_sin(MLA_ROPE_DIM)
    z = jnp.zeros_like(c)
    ca = jnp.concatenate([c, z, c, z], axis=-1)
    sa = jnp.concatenate([-s, z, s, z], axis=-1)
    return ca, sa, cm, sm


def kernel(x, ffn1_pre_g, ffn1_w_gate, ffn1_w_up, ffn1_w_down, ffn1_post_g, mix_pre_g, w_in, mla_q_norm_g, mla_kv_norm_g, mla_w_uq, mla_w_ukv, mla_out_g, moba_out_g, w_out, mix_post_g, ffn2_pre_g, ffn2_w_gate, ffn2_w_up, ffn2_w_down, ffn2_post_g):
    B, S, D = x.shape
    depth = w_in.shape[0]
    H = MLA_HEADS
    ca, sa, cm, sm = _rope_tables(S)
    xt = x.reshape(B * S, D)
    for l in range(depth):
        w_lat, w_kpe, w_moba = _split_w_in(w_in[l])
        uq = mla_w_uq[l].reshape(MLA_Q_RANK, H, MLA_QK_DIM)
        w_uq = jnp.concatenate(
            [uq[..., :MLA_NOPE_DIM], _spread_rope_cols(uq[..., MLA_NOPE_DIM:])], axis=-1
        ).reshape(MLA_Q_RANK, H * QK_PAD).astype(BF16)
        ukv = mla_w_ukv[l].reshape(MLA_KV_RANK, H, MLA_NOPE_DIM + MLA_V_DIM)
        w_uk = ukv[..., :MLA_NOPE_DIM].reshape(MLA_KV_RANK, MLA_WIDTH).astype(BF16)
        w_uv = ukv[..., MLA_NOPE_DIM:].reshape(MLA_KV_RANK, MLA_WIDTH).astype(BF16)

        xt = _ffn(xt, ffn1_pre_g[l][None], ffn1_w_gate[l], ffn1_w_up[l], ffn1_w_down[l],
                  ffn1_post_g[l][None])
        q, k, v = _proj(xt, mix_pre_g[l][None], w_lat, w_kpe, w_moba, mla_q_norm_g[l][None],
                        mla_kv_norm_g[l][None], w_uq, w_uk, w_uv, ca, sa, cm, sm, seq=S)
        am = _attention(q.reshape(B, S, -1), k.reshape(B, S, -1), v.reshape(B, S, -1))
        xt = _out_proj(xt, am.reshape(B * S, -1), mla_out_g[l][None], moba_out_g[l][None],
                       w_out[l].astype(BF16), mix_post_g[l][None])
        xt = _ffn(xt, ffn2_pre_g[l][None], ffn2_w_gate[l], ffn2_w_up[l], ffn2_w_down[l],
                  ffn2_post_g[l][None])
    return xt.reshape(B, S, D)
```

```python
import functools

import jax
import jax.numpy as jnp
from jax import lax
from jax.experimental import pallas as pl
from jax.experimental.pallas import tpu as pltpu

D_MODEL = 2048
FFN_DIM = 5632
MLA_HEADS = 8
MLA_Q_RANK = 512
MLA_KV_RANK = 256
MLA_NOPE_DIM = 128
MLA_ROPE_DIM = 64
MLA_V_DIM = 128
MLA_QK_DIM = MLA_NOPE_DIM + MLA_ROPE_DIM
MOBA_HEADS = 8
MOBA_HEAD_DIM = 128
MOBA_BLOCK = 256
MOBA_TOPK = 3
MLA_WIDTH = MLA_HEADS * MLA_V_DIM
MOBA_WIDTH = MOBA_HEADS * MOBA_HEAD_DIM
ROPE_THETA = 10000.0
NORM_EPS = 1e-6
NEG_INF = -1e30
MASK_BIAS = -(2.0 ** 100)

LANES = 128
SUBLANES_BF16 = 16
HEADS = MLA_HEADS + MOBA_HEADS
QK_PAD = 2 * LANES
V_DIM = 128
V7X_VMEM_BYTES = 64 * 2 ** 20
MIB = 2 ** 20

BF16 = jnp.bfloat16
F32 = jnp.float32


def _rms(x, g):
    return x * lax.rsqrt(jnp.mean(x * x, axis=-1, keepdims=True) + NORM_EPS) * g


def _dot(a, b):
    return jnp.dot(a, b, preferred_element_type=F32)


def _dot_nt(a, b):
    return lax.dot_general(a, b, (((1,), (1,)), ((), ())), preferred_element_type=F32)


def _const_spec(shape):
    return pl.BlockSpec(shape, lambda *_: (0,) * len(shape), pipeline_mode=pl.Buffered(1))


def _ffn_kernel(x_ref, pre_g_ref, wg_ref, wu_ref, wd_ref, post_g_ref, o_ref, xn_ref):
    f = pl.program_id(1)
    last = pl.num_programs(1) - 1
    tm = x_ref.shape[0]
    halves = [slice(0, tm // 2), slice(tm // 2, tm)]

    def swiglu_down(xn):
        h = _dot(xn, wg_ref[...].astype(BF16))
        u = _dot(xn, wu_ref[...].astype(BF16))
        a = (h * jax.nn.sigmoid(h)) * u
        return _dot(a.astype(BF16), wd_ref[...].astype(BF16))

    @pl.when(f == 0)
    def _():
        for rows in halves:
            xn = _rms(x_ref[rows, :], pre_g_ref[...]).astype(BF16)
            xn_ref[rows, :] = xn
            o_ref[rows, :] = swiglu_down(xn)

    @pl.when((f > 0) & (f < last))
    def _():
        o_ref[...] += swiglu_down(xn_ref[...])

    @pl.when(f == last)
    def _():
        for rows in halves:
            acc = o_ref[rows, :] + swiglu_down(xn_ref[rows, :])
            o_ref[rows, :] = x_ref[rows, :] + 0.5 * _rms(acc, post_g_ref[...])


def _ffn(x, pre_g, wg, wu, wd, post_g, *, tm=1024, tf=256):
    T, D = x.shape
    F = wg.shape[1]
    vmem = (2 * 2 * tm * D * 4
            + tm * D * 2
            + 2 * 3 * D * tf * 4
            + 3 * D * tf * 2
            + 4 * tm * tf * 4 + tm * D * 4)
    vmem = min(vmem, V7X_VMEM_BYTES)
    return pl.pallas_call(
        _ffn_kernel,
        grid=(T // tm, F // tf),
        in_specs=[
            pl.BlockSpec((tm, D), lambda i, f: (i, 0)),
            pl.BlockSpec((1, D), lambda i, f: (0, 0)),
            pl.BlockSpec((D, tf), lambda i, f: (0, f)),
            pl.BlockSpec((D, tf), lambda i, f: (0, f)),
            pl.BlockSpec((tf, D), lambda i, f: (f, 0)),
            pl.BlockSpec((1, D), lambda i, f: (0, 0)),
        ],
        out_specs=pl.BlockSpec((tm, D), lambda i, f: (i, 0)),
        out_shape=jax.ShapeDtypeStruct((T, D), F32),
        scratch_shapes=[pltpu.VMEM((tm, D), BF16)],
        compiler_params=pltpu.CompilerParams(
            dimension_semantics=("parallel", "arbitrary"), vmem_limit_bytes=vmem),
        name="ffn",
    )(x, pre_g, wg, wu, wd, post_g)


def _rope(x, c, s):
    return x * c + pltpu.roll(x, LANES // 2, axis=1) * s


def _moba_bias(kmean, qh, blk):
    nb = kmean.shape[0]
    gate = _dot_nt(kmean.astype(BF16), qh)
    n = lax.broadcasted_iota(jnp.int32, gate.shape, 0)
    cand = jnp.where(n < blk, gate, -jnp.inf)
    allowed = n == blk
    for _ in range(min(MOBA_TOPK, nb - 1)):
        best = jnp.max(cand, axis=0, keepdims=True)
        first = jnp.min(jnp.where(cand == best, n, nb), axis=0, keepdims=True)
        pick = (n == first) & (best > -jnp.inf)
        allowed = allowed | pick
        cand = jnp.where(pick, -jnp.inf, cand)
    bias = jnp.where(allowed, 0.0, MASK_BIAS)
    bias = jnp.concatenate([bias, jnp.zeros((LANES - nb, bias.shape[1]), F32)], axis=0)
    return bias.T


def _proj_kernel(x_ref, g_ref, w_lat_ref, w_kpe_ref, w_moba_ref, gq_ref, gkv_ref,
                 w_uq_ref, w_uk_ref, w_uv_ref, ca_ref, sa_ref, cm_ref, sm_ref,
                 q_ref, k_ref, v_ref, un_ref, kmean_ref):
    nb = kmean_ref.shape[0]
    L = MOBA_BLOCK
    group = pl.program_id(1)

    @pl.when(group == 0)
    def _():
        un = _rms(x_ref[...], g_ref[...]).astype(BF16)
        un_ref[...] = un
        ca, sa = ca_ref[...], sa_ref[...]

        c_q = _dot(un, w_lat_ref[:, :MLA_Q_RANK])
        c_kv = _dot(un, w_lat_ref[:, MLA_Q_RANK:])
        k_pe = _dot(un, w_kpe_ref[...])

        qn = _rms(c_q, gq_ref[...]).astype(BF16)
        kvn = _rms(c_kv, gkv_ref[...]).astype(BF16)
        q = _dot(qn, w_uq_ref[...])
        k_nope = _dot(kvn, w_uk_ref[...])
        v_ref[...] = _dot(kvn, w_uv_ref[...]).astype(BF16)
        for h in range(MLA_HEADS):
            lo = h * QK_PAD
            q_ref[:, lo:lo + LANES] = q[:, lo:lo + LANES].astype(BF16)
            q_ref[:, lo + LANES:lo + QK_PAD] = _rope(
                q[:, lo + LANES:lo + QK_PAD], ca, sa).astype(BF16)

        k_pe = _rope(k_pe, ca, sa).astype(BF16)
        for h in range(MLA_HEADS):
            lo = h * QK_PAD
            k_ref[:, lo:lo + LANES] = k_nope[:, h * LANES:(h + 1) * LANES].astype(BF16)
            k_ref[:, lo + LANES:lo + QK_PAD] = k_pe

    @pl.when(group == 1)
    def _():
        @pl.when(pl.program_id(0) == 0)
        def _():
            kmean_ref[...] = jnp.zeros_like(kmean_ref)

        un = un_ref[...]
        cm, sm = cm_ref[...], sm_ref[...]
        tm = un.shape[0]
        blk0 = (pl.program_id(0) * (tm // L)) % nb
        km = _dot(un, w_moba_ref[:, MOBA_WIDTH:2 * MOBA_WIDTH])
        qm = _dot(un, w_moba_ref[:, :MOBA_WIDTH])
        lane = lax.broadcasted_iota(jnp.int32, (L, LANES), 1)
        blk_row = lax.broadcasted_iota(jnp.int32, (nb, LANES), 0)
        for h in range(MOBA_HEADS):
            sl = slice(h * LANES, (h + 1) * LANES)
            lo = h * QK_PAD
            qh = _rope(qm[:, sl], cm, sm).astype(BF16)
            kh = _rope(km[:, sl], cm, sm)
            q_ref[:, lo:lo + LANES] = qh
            k_ref[:, lo:lo + LANES] = kh.astype(BF16)
            kmean = kmean_ref[:, sl]
            for part in range(tm // L):
                rows = slice(part * L, (part + 1) * L)
                blk = blk0 + part
                kmean = jnp.where(blk_row == blk, jnp.mean(kh[rows], axis=0, keepdims=True), kmean)
                q_ref[rows, lo + LANES:lo + QK_PAD] = _moba_bias(kmean, qh[rows], blk).astype(BF16)
                k_ref[rows, lo + LANES:lo + QK_PAD] = jnp.where(lane == blk, 1.0, 0.0).astype(BF16)
            kmean_ref[:, sl] = kmean
        v_ref[...] = _dot(un, w_moba_ref[:, 2 * MOBA_WIDTH:]).astype(BF16)


def _proj(x, g, w_lat, w_kpe, w_moba, gq, gkv, w_uq, w_uk, w_uv, ca, sa, cm, sm, *, seq, tm=512):
    T, D = x.shape
    n_in = w_lat.shape[1] + w_kpe.shape[1] + w_moba.shape[1]
    nb = seq // MOBA_BLOCK
    assert seq % tm == 0 and tm % MOBA_BLOCK == 0 and nb % SUBLANES_BF16 == 0 and nb <= LANES
    assert MLA_HEADS == MOBA_HEADS and MLA_WIDTH == MOBA_WIDTH
    wq, wv = MLA_HEADS * QK_PAD, MLA_WIDTH
    out = lambda w: pl.BlockSpec((tm, w), lambda i, grp: (i, grp))
    tab = pl.BlockSpec((tm, LANES), lambda i, grp: (i % (seq // tm), 0))
    vmem = (2 * tm * D * 4 + tm * D * 2
            + (D * n_in + MLA_Q_RANK * wq + 2 * MLA_KV_RANK * MLA_WIDTH) * 2
            + 2 * tm * (2 * wq + wv) * 2
            + 6 * tm * wq * 4
            + 8 * MIB)
    return pl.pallas_call(
        _proj_kernel,
        grid=(T // tm, 2),
        in_specs=[
            pl.BlockSpec((tm, D), lambda i, grp: (i, 0)), _const_spec((1, D)),
            _const_spec(w_lat.shape), _const_spec(w_kpe.shape), _const_spec(w_moba.shape),
            _const_spec((1, MLA_Q_RANK)), _const_spec((1, MLA_KV_RANK)),
            _const_spec((MLA_Q_RANK, wq)), _const_spec((MLA_KV_RANK, MLA_WIDTH)),
            _const_spec((MLA_KV_RANK, MLA_WIDTH)),
            tab, tab, tab, tab,
        ],
        out_specs=[out(wq), out(wq), out(wv)],
        out_shape=[
            jax.ShapeDtypeStruct((T, 2 * wq), BF16),
            jax.ShapeDtypeStruct((T, 2 * wq), BF16),
            jax.ShapeDtypeStruct((T, 2 * wv), BF16),
        ],
        scratch_shapes=[pltpu.VMEM((tm, D), BF16), pltpu.VMEM((nb, MOBA_WIDTH), F32)],
        compiler_params=pltpu.CompilerParams(
            dimension_semantics=("arbitrary", "arbitrary"), vmem_limit_bytes=vmem),
        name="proj",
    )(x, g, w_lat, w_kpe, w_moba, gq, gkv, w_uq, w_uk, w_uv, ca, sa, cm, sm)


LOG2_E = 1.4426950408889634


def _attn_kernel(q_ref, k_ref, v_ref, o_ref, m_ref, acc_ref, *, t, nsub):
    h = pl.program_id(1)
    i = pl.program_id(2)
    scale = jnp.where(h < MLA_HEADS, MLA_QK_DIM ** -0.5, MOBA_HEAD_DIM ** -0.5)
    c2 = (scale * LOG2_E).astype(F32)
    m_ref[...] = jnp.full(m_ref.shape, -jnp.inf, F32)
    acc_ref[...] = jnp.zeros_like(acc_ref)

    def logits(sub, start, tk):
        q = q_ref[sub * t:(sub + 1) * t, :]
        return _dot_nt(q, k_ref[pl.ds(pl.multiple_of(start, t), tk), :])

    def update(sub, s, start, tk, diagonal):
        parts = [pl.ds(pl.multiple_of(start + c * t, t), t) for c in range(tk // t)]
        if diagonal:
            row = lax.broadcasted_iota(jnp.int32, s.shape, 0)
            col = lax.broadcasted_iota(jnp.int32, s.shape, 1)
            s = jnp.where(col <= row, s, NEG_INF)
        chunks = [s[:, c * LANES:(c + 1) * LANES] for c in range(tk // LANES)]
        m_cur = jnp.max(functools.reduce(jnp.maximum, chunks), axis=1, keepdims=True)
        m_prev = m_ref[sub]
        m_new = jnp.maximum(m_prev, jnp.broadcast_to(m_cur, (t, LANES)) * c2)
        alpha = jnp.exp2(m_prev - m_new)
        p = [jnp.exp2(c * c2 - m_new).astype(BF16) for c in chunks]
        ones = jnp.ones((t, LANES), BF16)
        n = t // LANES
        pv = [_dot(jnp.concatenate(p[c * n:(c + 1) * n], axis=1),
                   jnp.concatenate([v_ref[r, :], ones], axis=1)) for c, r in enumerate(parts)]
        acc_ref[sub] = jnp.concatenate([alpha, alpha], axis=1) * acc_ref[sub] + sum(pv)
        m_ref[sub] = m_new

    def body(j, carry):
        start = j * 2 * t
        ss = [logits(sub, start, 2 * t) for sub in range(nsub)]
        for sub in range(nsub):
            update(sub, ss[sub], start, 2 * t, False)
        return carry

    lax.fori_loop(0, i * (nsub // 2), body, 0)
    first = i * nsub
    for kb in range(nsub):
        for sub in range(kb, nsub):
            start = (first + kb) * t
            update(sub, logits(sub, start, t), start, t, sub == kb)
    for sub in range(nsub):
        acc = acc_ref[sub]
        o_ref[sub * t:(sub + 1) * t, :] = acc[:, :V_DIM] / acc[:, V_DIM:]


def _attention(q, k, v, *, t=512, nsub=4):
    B, S, _ = q.shape
    tq = nsub * t
    assert S % tq == 0 and t % MOBA_BLOCK == 0 and nsub % 2 == 0
    kern = functools.partial(_attn_kernel, t=t, nsub=nsub)
    vmem = (2 * S * (QK_PAD + V_DIM) * 2 + 2 * tq * QK_PAD * 2 + 2 * tq * V_DIM * 4
            + tq * (LANES + 2 * V_DIM) * 4 + nsub * 6 * t * t * 4 + 8 * MIB)
    return pl.pallas_call(
        kern,
        grid=(B, HEADS, S // tq),
        in_specs=[
            pl.BlockSpec((None, tq, QK_PAD), lambda b, h, i: (b, i, h)),
            pl.BlockSpec((None, S, QK_PAD), lambda b, h, i: (b, 0, h)),
            pl.BlockSpec((None, S, V_DIM), lambda b, h, i: (b, 0, h)),
        ],
        out_specs=pl.BlockSpec((None, tq, V_DIM), lambda b, h, i: (b, i, h)),
        out_shape=jax.ShapeDtypeStruct((B, S, HEADS * V_DIM), F32),
        scratch_shapes=[pltpu.VMEM((nsub, t, LANES), F32),
                        pltpu.VMEM((nsub, t, 2 * V_DIM), F32)],
        compiler_params=pltpu.CompilerParams(
            dimension_semantics=("parallel", "parallel", "arbitrary"), vmem_limit_bytes=vmem),
        name="attention",
    )(q, k, v)


def _out_kernel(x_ref, am_ref, ga_ref, gm_ref, w_ref, gp_ref, o_ref):
    an = _rms(am_ref[:, :MLA_WIDTH], ga_ref[...]).astype(BF16)
    mn = _rms(am_ref[:, MLA_WIDTH:], gm_ref[...]).astype(BF16)
    y = _dot(jnp.concatenate([an, mn], axis=1), w_ref[...])
    o_ref[...] = x_ref[...] + _rms(y, gp_ref[...])


def _out_proj(x, am, ga, gm, w, gp, *, tm=512):
    T, D = x.shape
    W = MLA_WIDTH + MOBA_WIDTH
    tok = lambda wd: pl.BlockSpec((tm, wd), lambda i: (i, 0))
    vmem = 2 * 2 * tm * D * 4 + 2 * tm * W * 4 + W * D * 2 + 4 * tm * D * 4 + 8 * MIB
    return pl.pallas_call(
        _out_kernel,
        grid=(T // tm,),
        in_specs=[tok(D), tok(W), _const_spec((1, MLA_WIDTH)), _const_spec((1, MOBA_WIDTH)),
                  _const_spec((W, D)), _const_spec((1, D))],
        out_specs=tok(D),
        out_shape=jax.ShapeDtypeStruct((T, D), F32),
        compiler_params=pltpu.CompilerParams(
            dimension_semantics=("parallel",), vmem_limit_bytes=vmem),
        name="out_proj",
    )(x, am, ga, gm, w, gp)


def _spread_rope_cols(w):
    half = MLA_ROPE_DIM // 2
    z = jnp.zeros(w.shape[:-1] + (LANES // 2 - half,), w.dtype)
    return jnp.concatenate([w[..., :half], z, w[..., half:], z], axis=-1)


def _rope_tables(seq):
    pos = jnp.arange(seq, dtype=F32)[:, None]

    def inv_freq(dim):
        return 1.0 / (ROPE_THETA ** (jnp.arange(0, dim, 2, dtype=F32) / dim))

    def tables(inv, live):
        ang = pos * jnp.concatenate([inv, inv])[None, :]
        sign = jnp.concatenate([-live, live])[None, :]
        return jnp.cos(ang) * jnp.abs(sign), jnp.sin(ang) * sign

    cm, sm = tables(inv_freq(MOBA_HEAD_DIM), jnp.ones((LANES // 2,), F32))
    pad = jnp.zeros((LANES // 2 - MLA_ROPE_DIM // 2,), F32)
    ca, sa = tables(jnp.concatenate([inv_freq(MLA_ROPE_DIM), pad]),
                    jnp.concatenate([jnp.ones((MLA_ROPE_DIM // 2,), F32), pad]))
    return ca, sa, cm, sm


def kernel(x, ffn1_pre_g, ffn1_w_gate, ffn1_w_up, ffn1_w_down, ffn1_post_g, mix_pre_g, w_in, mla_q_norm_g, mla_kv_norm_g, mla_w_uq, mla_w_ukv, mla_out_g, moba_out_g, w_out, mix_post_g, ffn2_pre_g, ffn2_w_gate, ffn2_w_up, ffn2_w_down, ffn2_post_g):
    B, S, D = x.shape
    depth = w_in.shape[0]
    H = MLA_HEADS
    ca, sa, cm, sm = _rope_tables(S)
    xt = x.reshape(B * S, D)
    for l in range(depth):
        o_kpe = MLA_Q_RANK + MLA_KV_RANK
        w_lat = w_in[l][:, :o_kpe].astype(BF16)
        w_kpe = _spread_rope_cols(w_in[l][:, o_kpe:o_kpe + MLA_ROPE_DIM]).astype(BF16)
        w_moba = w_in[l][:, o_kpe + MLA_ROPE_DIM:].astype(BF16)
        uq = mla_w_uq[l].reshape(MLA_Q_RANK, H, MLA_QK_DIM)
        w_uq = jnp.concatenate(
            [uq[..., :MLA_NOPE_DIM], _spread_rope_cols(uq[..., MLA_NOPE_DIM:])], axis=-1
        ).reshape(MLA_Q_RANK, H * QK_PAD).astype(BF16)
        ukv = mla_w_ukv[l].reshape(MLA_KV_RANK, H, MLA_NOPE_DIM + MLA_V_DIM)
        w_uk = ukv[..., :MLA_NOPE_DIM].reshape(MLA_KV_RANK, MLA_WIDTH).astype(BF16)
        w_uv = ukv[..., MLA_NOPE_DIM:].reshape(MLA_KV_RANK, MLA_WIDTH).astype(BF16)

        xt = _ffn(xt, ffn1_pre_g[l][None], ffn1_w_gate[l], ffn1_w_up[l], ffn1_w_down[l],
                  ffn1_post_g[l][None])
        q, k, v = _proj(xt, mix_pre_g[l][None], w_lat, w_kpe, w_moba, mla_q_norm_g[l][None],
                        mla_kv_norm_g[l][None], w_uq, w_uk, w_uv, ca, sa, cm, sm, seq=S)
        am = _attention(q.reshape(B, S, -1), k.reshape(B, S, -1), v.reshape(B, S, -1))
        xt = _out_proj(xt, am.reshape(B * S, -1), mla_out_g[l][None], moba_out_g[l][None],
                       w_out[l].astype(BF16), mix_post_g[l][None])
        xt = _ffn(xt, ffn2_pre_g[l][None], ffn2_w_gate[l], ffn2_w_up[l], ffn2_w_down[l],
                  ffn2_post_g[l][None])
    return xt.reshape(B, S, D)
```

```python
import functools

import jax
import jax.numpy as jnp
from jax import lax
from jax.experimental import pallas as pl
from jax.experimental.pallas import tpu as pltpu

D_MODEL = 2048
FFN_DIM = 5632
MLA_HEADS = 8
MLA_Q_RANK = 512
MLA_KV_RANK = 256
MLA_NOPE_DIM = 128
MLA_ROPE_DIM = 64
MLA_V_DIM = 128
MLA_QK_DIM = MLA_NOPE_DIM + MLA_ROPE_DIM
MOBA_HEADS = 8
MOBA_HEAD_DIM = 128
MOBA_BLOCK = 256
MOBA_TOPK = 3
MLA_WIDTH = MLA_HEADS * MLA_V_DIM
MOBA_WIDTH = MOBA_HEADS * MOBA_HEAD_DIM
ROPE_THETA = 10000.0
NORM_EPS = 1e-6
NEG_INF = -1e30
MASK_BIAS = -(2.0 ** 100)

LANES = 128
SUBLANES_BF16 = 16
HEADS = MLA_HEADS + MOBA_HEADS
QK_PAD = 2 * LANES
V_DIM = 128
V7X_VMEM_BYTES = 64 * 2 ** 20
MIB = 2 ** 20

BF16 = jnp.bfloat16
F32 = jnp.float32


def _rms(x, g):
    return x * lax.rsqrt(jnp.mean(x * x, axis=-1, keepdims=True) + NORM_EPS) * g


def _dot(a, b):
    return jnp.dot(a, b, preferred_element_type=F32)


def _dot_nt(a, b):
    return lax.dot_general(a, b, (((1,), (1,)), ((), ())), preferred_element_type=F32)


def _const_spec(shape):
    return pl.BlockSpec(shape, lambda *_: (0,) * len(shape), pipeline_mode=pl.Buffered(1))


def _ffn_kernel(x_ref, pre_g_ref, wg_ref, wu_ref, wd_ref, post_g_ref, o_ref, xn_ref):
    f = pl.program_id(1)
    last = pl.num_programs(1) - 1
    tm = x_ref.shape[0]
    halves = [slice(0, tm // 2), slice(tm // 2, tm)]

    def swiglu_down(xn):
        h = _dot(xn, wg_ref[...].astype(BF16))
        u = _dot(xn, wu_ref[...].astype(BF16))
        a = (h * jax.nn.sigmoid(h)) * u
        return _dot(a.astype(BF16), wd_ref[...].astype(BF16))

    @pl.when(f == 0)
    def _():
        for rows in halves:
            xn = _rms(x_ref[rows, :], pre_g_ref[...]).astype(BF16)
            xn_ref[rows, :] = xn
            o_ref[rows, :] = swiglu_down(xn)

    @pl.when((f > 0) & (f < last))
    def _():
        o_ref[...] += swiglu_down(xn_ref[...])

    @pl.when(f == last)
    def _():
        for rows in halves:
            acc = o_ref[rows, :] + swiglu_down(xn_ref[rows, :])
            o_ref[rows, :] = x_ref[rows, :] + 0.5 * _rms(acc, post_g_ref[...])


def _ffn(x, pre_g, wg, wu, wd, post_g, *, tm=1024, tf=256):
    T, D = x.shape
    F = wg.shape[1]
    wbytes = wg.dtype.itemsize
    vmem = (2 * 2 * tm * D * 4
            + tm * D * 2
            + 2 * 3 * D * tf * wbytes
            + 3 * D * tf * 2
            + 4 * tm * tf * 4 + tm * D * 4)
    vmem = min(vmem, V7X_VMEM_BYTES)
    return pl.pallas_call(
        _ffn_kernel,
        grid=(T // tm, F // tf),
        in_specs=[
            pl.BlockSpec((tm, D), lambda i, f: (i, 0)),
            pl.BlockSpec((1, D), lambda i, f: (0, 0)),
            pl.BlockSpec((D, tf), lambda i, f: (0, f)),
            pl.BlockSpec((D, tf), lambda i, f: (0, f)),
            pl.BlockSpec((tf, D), lambda i, f: (f, 0)),
            pl.BlockSpec((1, D), lambda i, f: (0, 0)),
        ],
        out_specs=pl.BlockSpec((tm, D), lambda i, f: (i, 0)),
        out_shape=jax.ShapeDtypeStruct((T, D), F32),
        scratch_shapes=[pltpu.VMEM((tm, D), BF16)],
        compiler_params=pltpu.CompilerParams(
            dimension_semantics=("parallel", "arbitrary"), vmem_limit_bytes=vmem),
        name="ffn",
    )(x, pre_g, wg, wu, wd, post_g)


def _rope(x, c, s):
    return x * c + pltpu.roll(x, LANES // 2, axis=1) * s


def _moba_bias(kmean, qh, blk):
    nb = kmean.shape[0]
    gate = _dot_nt(kmean.astype(BF16), qh)
    n = lax.broadcasted_iota(jnp.int32, gate.shape, 0)
    cand = jnp.where(n < blk, gate, -jnp.inf)
    allowed = n == blk
    for _ in range(min(MOBA_TOPK, nb - 1)):
        best = jnp.max(cand, axis=0, keepdims=True)
        first = jnp.min(jnp.where(cand == best, n, nb), axis=0, keepdims=True)
        pick = (n == first) & (best > -jnp.inf)
        allowed = allowed | pick
        cand = jnp.where(pick, -jnp.inf, cand)
    bias = jnp.where(allowed, 0.0, MASK_BIAS)
    bias = jnp.concatenate([bias, jnp.zeros((LANES - nb, bias.shape[1]), F32)], axis=0)
    return bias.T


def _proj_kernel(x_ref, g_ref, w_lat_ref, w_kpe_ref, w_moba_ref, gq_ref, gkv_ref,
                 w_uq_ref, w_uk_ref, w_uv_ref, ca_ref, sa_ref, cm_ref, sm_ref,
                 q_ref, k_ref, v_ref, un_ref, kmean_ref):
    nb = kmean_ref.shape[0]
    L = MOBA_BLOCK
    group = pl.program_id(1)

    @pl.when(group == 0)
    def _():
        un = _rms(x_ref[...], g_ref[...]).astype(BF16)
        un_ref[...] = un
        ca, sa = ca_ref[...], sa_ref[...]

        c_q = _dot(un, w_lat_ref[:, :MLA_Q_RANK])
        c_kv = _dot(un, w_lat_ref[:, MLA_Q_RANK:])
        k_pe = _dot(un, w_kpe_ref[...])

        qn = _rms(c_q, gq_ref[...]).astype(BF16)
        kvn = _rms(c_kv, gkv_ref[...]).astype(BF16)
        q = _dot(qn, w_uq_ref[...])
        k_nope = _dot(kvn, w_uk_ref[...])
        v_ref[...] = _dot(kvn, w_uv_ref[...]).astype(BF16)
        for h in range(MLA_HEADS):
            lo = h * QK_PAD
            q_ref[:, lo:lo + LANES] = q[:, lo:lo + LANES].astype(BF16)
            q_ref[:, lo + LANES:lo + QK_PAD] = _rope(
                q[:, lo + LANES:lo + QK_PAD], ca, sa).astype(BF16)

        k_pe = _rope(k_pe, ca, sa).astype(BF16)
        for h in range(MLA_HEADS):
            lo = h * QK_PAD
            k_ref[:, lo:lo + LANES] = k_nope[:, h * LANES:(h + 1) * LANES].astype(BF16)
            k_ref[:, lo + LANES:lo + QK_PAD] = k_pe

    @pl.when(group == 1)
    def _():
        @pl.when(pl.program_id(0) == 0)
        def _():
            kmean_ref[...] = jnp.zeros_like(kmean_ref)

        un = un_ref[...]
        cm, sm = cm_ref[...], sm_ref[...]
        tm = un.shape[0]
        blk0 = (pl.program_id(0) * (tm // L)) % nb
        km = _dot(un, w_moba_ref[:, MOBA_WIDTH:2 * MOBA_WIDTH])
        qm = _dot(un, w_moba_ref[:, :MOBA_WIDTH])
        lane = lax.broadcasted_iota(jnp.int32, (L, LANES), 1)
        blk_row = lax.broadcasted_iota(jnp.int32, (nb, LANES), 0)
        for h in range(MOBA_HEADS):
            sl = slice(h * LANES, (h + 1) * LANES)
            lo = h * QK_PAD
            qh = _rope(qm[:, sl], cm, sm).astype(BF16)
            kh = _rope(km[:, sl], cm, sm)
            q_ref[:, lo:lo + LANES] = qh
            k_ref[:, lo:lo + LANES] = kh.astype(BF16)
            kmean = kmean_ref[:, sl]
            for part in range(tm // L):
                rows = slice(part * L, (part + 1) * L)
                blk = blk0 + part
                kmean = jnp.where(blk_row == blk, jnp.mean(kh[rows], axis=0, keepdims=True), kmean)
                q_ref[rows, lo + LANES:lo + QK_PAD] = _moba_bias(kmean, qh[rows], blk).astype(BF16)
                k_ref[rows, lo + LANES:lo + QK_PAD] = jnp.where(lane == blk, 1.0, 0.0).astype(BF16)
            kmean_ref[:, sl] = kmean
        v_ref[...] = _dot(un, w_moba_ref[:, 2 * MOBA_WIDTH:]).astype(BF16)


def _proj(x, g, w_lat, w_kpe, w_moba, gq, gkv, w_uq, w_uk, w_uv, ca, sa, cm, sm, *, seq, tm=512):
    T, D = x.shape
    n_in = w_lat.shape[1] + w_kpe.shape[1] + w_moba.shape[1]
    nb = seq // MOBA_BLOCK
    assert seq % tm == 0 and tm % MOBA_BLOCK == 0 and nb % SUBLANES_BF16 == 0 and nb <= LANES
    assert MLA_HEADS == MOBA_HEADS and MLA_WIDTH == MOBA_WIDTH
    wq, wv = MLA_HEADS * QK_PAD, MLA_WIDTH
    out = lambda w: pl.BlockSpec((tm, w), lambda i, grp: (i, grp))
    tab = pl.BlockSpec((tm, LANES), lambda i, grp: (i % (seq // tm), 0))
    vmem = (2 * tm * D * 4 + tm * D * 2
            + (D * n_in + MLA_Q_RANK * wq + 2 * MLA_KV_RANK * MLA_WIDTH) * 2
            + 2 * tm * (2 * wq + wv) * 2
            + 6 * tm * wq * 4
            + 8 * MIB)
    return pl.pallas_call(
        _proj_kernel,
        grid=(T // tm, 2),
        in_specs=[
            pl.BlockSpec((tm, D), lambda i, grp: (i, 0)), _const_spec((1, D)),
            _const_spec(w_lat.shape), _const_spec(w_kpe.shape), _const_spec(w_moba.shape),
            _const_spec((1, MLA_Q_RANK)), _const_spec((1, MLA_KV_RANK)),
            _const_spec((MLA_Q_RANK, wq)), _const_spec((MLA_KV_RANK, MLA_WIDTH)),
            _const_spec((MLA_KV_RANK, MLA_WIDTH)),
            tab, tab, tab, tab,
        ],
        out_specs=[out(wq), out(wq), out(wv)],
        out_shape=[
            jax.ShapeDtypeStruct((T, 2 * wq), BF16),
            jax.ShapeDtypeStruct((T, 2 * wq), BF16),
            jax.ShapeDtypeStruct((T, 2 * wv), BF16),
        ],
        scratch_shapes=[pltpu.VMEM((tm, D), BF16), pltpu.VMEM((nb, MOBA_WIDTH), F32)],
        compiler_params=pltpu.CompilerParams(
            dimension_semantics=("arbitrary", "arbitrary"), vmem_limit_bytes=vmem),
        name="proj",
    )(x, g, w_lat, w_kpe, w_moba, gq, gkv, w_uq, w_uk, w_uv, ca, sa, cm, sm)


LOG2_E = 1.4426950408889634


def _attn_kernel(*refs, t, nsub, nside):
    q_ref, k_ref, v_ref = refs[:3]
    side_in, (o_ref, *side_out) = refs[3:3 + nside], refs[3 + nside:4 + 2 * nside]
    m_ref, acc_ref = refs[4 + 2 * nside:]
    for w_ref, wo_ref in zip(side_in, side_out):
        wo_ref[...] = w_ref[...].astype(BF16)

    h = pl.program_id(1)
    i = pl.program_id(2)
    scale = jnp.where(h < MLA_HEADS, MLA_QK_DIM ** -0.5, MOBA_HEAD_DIM ** -0.5)
    c2 = (scale * LOG2_E).astype(F32)
    m_ref[...] = jnp.full(m_ref.shape, -jnp.inf, F32)
    acc_ref[...] = jnp.zeros_like(acc_ref)

    def logits(sub, start, tk):
        q = q_ref[sub * t:(sub + 1) * t, :]
        return _dot_nt(q, k_ref[pl.ds(pl.multiple_of(start, t), tk), :])

    def softmax(sub, s, diagonal):
        tk = s.shape[1]
        if diagonal:
            row = lax.broadcasted_iota(jnp.int32, s.shape, 0)
            col = lax.broadcasted_iota(jnp.int32, s.shape, 1)
            s = jnp.where(col <= row, s, NEG_INF)
        chunks = [s[:, c * LANES:(c + 1) * LANES] for c in range(tk // LANES)]
        m_cur = jnp.max(functools.reduce(jnp.maximum, chunks), axis=1, keepdims=True)
        m_prev = m_ref[sub]
        m_new = jnp.maximum(m_prev, jnp.broadcast_to(m_cur, (t, LANES)) * c2)
        alpha = jnp.exp2(m_prev - m_new)
        p = [jnp.exp2(c * c2 - m_new).astype(BF16) for c in chunks]
        m_ref[sub] = m_new
        return p, alpha

    def accumulate(sub, p, alpha, start):
        ones = jnp.ones((t, LANES), BF16)
        n = t // LANES
        pv = [_dot(jnp.concatenate(p[c * n:(c + 1) * n], axis=1),
                   jnp.concatenate([v_ref[pl.ds(pl.multiple_of(start + c * t, t), t), :], ones],
                                   axis=1)) for c in range(len(p) // n)]
        acc_ref[sub] = jnp.concatenate([alpha, alpha], axis=1) * acc_ref[sub] + sum(pv)

    def update(sub, s, start, diagonal):
        accumulate(sub, *softmax(sub, s, diagonal), start)

    def body(j, carry):
        start = j * 2 * t
        ss = [logits(sub, start, 2 * t) for sub in range(nsub)]
        for sub in range(nsub):
            update(sub, ss[sub], start, False)
        return carry

    lax.fori_loop(0, i * (nsub // 2), body, 0)
    first = i * nsub
    for kb in range(nsub):
        for sub in range(kb, nsub):
            start = (first + kb) * t
            update(sub, logits(sub, start, t), start, sub == kb)
    for sub in range(nsub):
        acc = acc_ref[sub]
        o_ref[sub * t:(sub + 1) * t, :] = acc[:, :V_DIM] / acc[:, V_DIM:]


def _side_block_rows(rows, steps):
    return next(rb for rb in range(SUBLANES_BF16, rows + 1, SUBLANES_BF16)
                if rows % rb == 0 and rows // rb <= steps)


def _attention(q, k, v, side_weights=(), *, t=512, nsub=4):
    B, S, _ = q.shape
    tq = nsub * t
    assert S % tq == 0 and t % MOBA_BLOCK == 0 and nsub % 2 == 0
    nq = S // tq
    steps = B * HEADS * nq
    kern = functools.partial(_attn_kernel, t=t, nsub=nsub, nside=len(side_weights))
    side_specs, side_bytes = [], 0
    for w in side_weights:
        rb = _side_block_rows(w.shape[0], steps)
        last = w.shape[0] // rb - 1
        side_specs.append(pl.BlockSpec(
            (rb, w.shape[1]),
            lambda b, h, i, last=last: (jnp.minimum((b * HEADS + h) * nq + i, last), 0)))
        side_bytes += 2 * rb * w.shape[1] * (4 + 2)
    vmem = (2 * S * (QK_PAD + V_DIM) * 2 + 2 * tq * QK_PAD * 2 + 2 * tq * V_DIM * 4
            + tq * (LANES + 2 * V_DIM) * 4 + nsub * 6 * t * t * 4 + side_bytes + 8 * MIB)
    out = pl.pallas_call(
        kern,
        grid=(B, HEADS, nq),
        in_specs=[
            pl.BlockSpec((None, tq, QK_PAD), lambda b, h, i: (b, i, h)),
            pl.BlockSpec((None, S, QK_PAD), lambda b, h, i: (b, 0, h)),
            pl.BlockSpec((None, S, V_DIM), lambda b, h, i: (b, 0, h)),
        ] + side_specs,
        out_specs=[pl.BlockSpec((None, tq, V_DIM), lambda b, h, i: (b, i, h))] + side_specs,
        out_shape=[jax.ShapeDtypeStruct((B, S, HEADS * V_DIM), F32)]
        + [jax.ShapeDtypeStruct(w.shape, BF16) for w in side_weights],
        scratch_shapes=[pltpu.VMEM((nsub, t, LANES), F32),
                        pltpu.VMEM((nsub, t, 2 * V_DIM), F32)],
        compiler_params=pltpu.CompilerParams(
            dimension_semantics=("arbitrary", "arbitrary", "arbitrary"), vmem_limit_bytes=vmem),
        name="attention",
    )(q, k, v, *side_weights)
    return out[0], tuple(out[1:])


def _out_kernel(x_ref, am_ref, ga_ref, gm_ref, w_ref, gp_ref, o_ref):
    an = _rms(am_ref[:, :MLA_WIDTH], ga_ref[...]).astype(BF16)
    mn = _rms(am_ref[:, MLA_WIDTH:], gm_ref[...]).astype(BF16)
    y = _dot(jnp.concatenate([an, mn], axis=1), w_ref[...])
    o_ref[...] = x_ref[...] + _rms(y, gp_ref[...])


def _out_proj(x, am, ga, gm, w, gp, *, tm=512):
    T, D = x.shape
    W = MLA_WIDTH + MOBA_WIDTH
    tok = lambda wd: pl.BlockSpec((tm, wd), lambda i: (i, 0))
    vmem = 2 * 2 * tm * D * 4 + 2 * tm * W * 4 + W * D * 2 + 4 * tm * D * 4 + 8 * MIB
    return pl.pallas_call(
        _out_kernel,
        grid=(T // tm,),
        in_specs=[tok(D), tok(W), _const_spec((1, MLA_WIDTH)), _const_spec((1, MOBA_WIDTH)),
                  _const_spec((W, D)), _const_spec((1, D))],
        out_specs=tok(D),
        out_shape=jax.ShapeDtypeStruct((T, D), F32),
        compiler_params=pltpu.CompilerParams(
            dimension_semantics=("parallel",), vmem_limit_bytes=vmem),
        name="out_proj",
    )(x, am, ga, gm, w, gp)


def _spread_rope_cols(w):
    half = MLA_ROPE_DIM // 2
    z = jnp.zeros(w.shape[:-1] + (LANES // 2 - half,), w.dtype)
    return jnp.concatenate([w[..., :half], z, w[..., half:], z], axis=-1)


def _rope_tables(seq):
    pos = jnp.arange(seq, dtype=F32)[:, None]

    def inv_freq(dim):
        return 1.0 / (ROPE_THETA ** (jnp.arange(0, dim, 2, dtype=F32) / dim))

    def tables(inv, live):
        ang = pos * jnp.concatenate([inv, inv])[None, :]
        sign = jnp.concatenate([-live, live])[None, :]
        return jnp.cos(ang) * jnp.abs(sign), jnp.sin(ang) * sign

    cm, sm = tables(inv_freq(MOBA_HEAD_DIM), jnp.ones((LANES // 2,), F32))
    pad = jnp.zeros((LANES // 2 - MLA_ROPE_DIM // 2,), F32)
    ca, sa = tables(jnp.concatenate([inv_freq(MLA_ROPE_DIM), pad]),
                    jnp.concatenate([jnp.ones((MLA_ROPE_DIM // 2,), F32), pad]))
    return ca, sa, cm, sm


def kernel(x, ffn1_pre_g, ffn1_w_gate, ffn1_w_up, ffn1_w_down, ffn1_post_g, mix_pre_g, w_in, mla_q_norm_g, mla_kv_norm_g, mla_w_uq, mla_w_ukv, mla_out_g, moba_out_g, w_out, mix_post_g, ffn2_pre_g, ffn2_w_gate, ffn2_w_up, ffn2_w_down, ffn2_post_g):
    B, S, D = x.shape
    depth = w_in.shape[0]
    H = MLA_HEADS
    ca, sa, cm, sm = _rope_tables(S)
    xt = x.reshape(B * S, D)
    for l in range(depth):
        o_kpe = MLA_Q_RANK + MLA_KV_RANK
        w_lat = w_in[l][:, :o_kpe].astype(BF16)
        w_kpe = _spread_rope_cols(w_in[l][:, o_kpe:o_kpe + MLA_ROPE_DIM]).astype(BF16)
        w_moba = w_in[l][:, o_kpe + MLA_ROPE_DIM:].astype(BF16)
        uq = mla_w_uq[l].reshape(MLA_Q_RANK, H, MLA_QK_DIM)
        w_uq = jnp.concatenate(
            [uq[..., :MLA_NOPE_DIM], _spread_rope_cols(uq[..., MLA_NOPE_DIM:])], axis=-1
        ).reshape(MLA_Q_RANK, H * QK_PAD).astype(BF16)
        ukv = mla_w_ukv[l].reshape(MLA_KV_RANK, H, MLA_NOPE_DIM + MLA_V_DIM)
        w_uk = ukv[..., :MLA_NOPE_DIM].reshape(MLA_KV_RANK, MLA_WIDTH).astype(BF16)
        w_uv = ukv[..., MLA_NOPE_DIM:].reshape(MLA_KV_RANK, MLA_WIDTH).astype(BF16)

        xt = _ffn(xt, ffn1_pre_g[l][None], ffn1_w_gate[l], ffn1_w_up[l], ffn1_w_down[l],
                  ffn1_post_g[l][None])
        q, k, v = _proj(xt, mix_pre_g[l][None], w_lat, w_kpe, w_moba, mla_q_norm_g[l][None],
                        mla_kv_norm_g[l][None], w_uq, w_uk, w_uv, ca, sa, cm, sm, seq=S)
        am, (wg2, wu2, wd2) = _attention(
            q.reshape(B, S, -1), k.reshape(B, S, -1), v.reshape(B, S, -1),
            side_weights=(ffn2_w_gate[l], ffn2_w_up[l], ffn2_w_down[l]))
        xt = _out_proj(xt, am.reshape(B * S, -1), mla_out_g[l][None], moba_out_g[l][None],
                       w_out[l].astype(BF16), mix_post_g[l][None])
        xt = _ffn(xt, ffn2_pre_g[l][None], wg2, wu2, wd2, ffn2_post_g[l][None], tf=512)
    return xt.reshape(B, S, D)
```

```python
import functools

import jax
import jax.numpy as jnp
from jax import lax
from jax.experimental import pallas as pl
from jax.experimental.pallas import tpu as pltpu

D_MODEL = 2048
FFN_DIM = 5632
MLA_HEADS = 8
MLA_Q_RANK = 512
MLA_KV_RANK = 256
MLA_NOPE_DIM = 128
MLA_ROPE_DIM = 64
MLA_V_DIM = 128
MLA_QK_DIM = MLA_NOPE_DIM + MLA_ROPE_DIM
MOBA_HEADS = 8
MOBA_HEAD_DIM = 128
MOBA_BLOCK = 256
MOBA_TOPK = 3
MLA_WIDTH = MLA_HEADS * MLA_V_DIM
MOBA_WIDTH = MOBA_HEADS * MOBA_HEAD_DIM
ROPE_THETA = 10000.0
NORM_EPS = 1e-6
NEG_INF = -1e30
MASK_BIAS = -(2.0 ** 100)

LANES = 128
SUBLANES_BF16 = 16
HEADS = MLA_HEADS + MOBA_HEADS
QK_PAD = 2 * LANES
V_DIM = 128
V7X_VMEM_BYTES = 64 * 2 ** 20
MIB = 2 ** 20

BF16 = jnp.bfloat16
F32 = jnp.float32


def _rms(x, g):
    return x * lax.rsqrt(jnp.mean(x * x, axis=-1, keepdims=True) + NORM_EPS) * g


def _dot(a, b):
    return jnp.dot(a, b, preferred_element_type=F32)


def _dot_nt(a, b):
    return lax.dot_general(a, b, (((1,), (1,)), ((), ())), preferred_element_type=F32)


def _const_spec(shape):
    return pl.BlockSpec(shape, lambda *_: (0,) * len(shape), pipeline_mode=pl.Buffered(1))


def _ffn_kernel(x_ref, pre_g_ref, wg_ref, wu_ref, wd_ref, post_g_ref, o_ref, xn_ref):
    f = pl.program_id(1)
    last = pl.num_programs(1) - 1
    tm = x_ref.shape[0]
    halves = [slice(0, tm // 2), slice(tm // 2, tm)]

    def swiglu_down(xn):
        h = _dot(xn, wg_ref[...].astype(BF16))
        u = _dot(xn, wu_ref[...].astype(BF16))
        a = (h * jax.nn.sigmoid(h)) * u
        return _dot(a.astype(BF16), wd_ref[...].astype(BF16))

    @pl.when(f == 0)
    def _():
        for rows in halves:
            xn = _rms(x_ref[rows, :], pre_g_ref[...]).astype(BF16)
            xn_ref[rows, :] = xn
            o_ref[rows, :] = swiglu_down(xn)

    @pl.when((f > 0) & (f < last))
    def _():
        o_ref[...] += swiglu_down(xn_ref[...])

    @pl.when(f == last)
    def _():
        for rows in halves:
            acc = o_ref[rows, :] + swiglu_down(xn_ref[rows, :])
            o_ref[rows, :] = x_ref[rows, :] + 0.5 * _rms(acc, post_g_ref[...])


def _ffn(x, pre_g, wg, wu, wd, post_g, *, tm=1024, tf=256):
    T, D = x.shape
    F = wg.shape[1]
    wbytes = wg.dtype.itemsize
    vmem = (2 * 2 * tm * D * 4
            + tm * D * 2
            + 2 * 3 * D * tf * wbytes
            + 3 * D * tf * 2
            + 4 * tm * tf * 4 + tm * D * 4)
    vmem = min(vmem, V7X_VMEM_BYTES)
    return pl.pallas_call(
        _ffn_kernel,
        grid=(T // tm, F // tf),
        in_specs=[
            pl.BlockSpec((tm, D), lambda i, f: (i, 0)),
            pl.BlockSpec((1, D), lambda i, f: (0, 0)),
            pl.BlockSpec((D, tf), lambda i, f: (0, f)),
            pl.BlockSpec((D, tf), lambda i, f: (0, f)),
            pl.BlockSpec((tf, D), lambda i, f: (f, 0)),
            pl.BlockSpec((1, D), lambda i, f: (0, 0)),
        ],
        out_specs=pl.BlockSpec((tm, D), lambda i, f: (i, 0)),
        out_shape=jax.ShapeDtypeStruct((T, D), F32),
        scratch_shapes=[pltpu.VMEM((tm, D), BF16)],
        compiler_params=pltpu.CompilerParams(
            dimension_semantics=("parallel", "arbitrary"), vmem_limit_bytes=vmem),
        name="ffn",
    )(x, pre_g, wg, wu, wd, post_g)


def _rope(x, c, s):
    return x * c + pltpu.roll(x, LANES // 2, axis=1) * s


def _moba_bias(kmean, qh, blk):
    nb = kmean.shape[0]
    gate = _dot_nt(kmean.astype(BF16), qh)
    n = lax.broadcasted_iota(jnp.int32, gate.shape, 0)
    cand = jnp.where(n < blk, gate, -jnp.inf)
    allowed = n == blk
    for _ in range(min(MOBA_TOPK, nb - 1)):
        best = jnp.max(cand, axis=0, keepdims=True)
        first = jnp.min(jnp.where(cand == best, n, nb), axis=0, keepdims=True)
        pick = (n == first) & (best > -jnp.inf)
        allowed = allowed | pick
        cand = jnp.where(pick, -jnp.inf, cand)
    bias = jnp.where(allowed, 0.0, MASK_BIAS)
    bias = jnp.concatenate([bias, jnp.zeros((LANES - nb, bias.shape[1]), F32)], axis=0)
    return bias.T


def _proj_kernel(x_ref, g_ref, w_lat_ref, w_kpe_ref, w_moba_ref, gq_ref, gkv_ref,
                 w_uq_ref, w_uk_ref, w_uv_ref, ca_ref, sa_ref, cm_ref, sm_ref,
                 q_ref, k_ref, v_ref, un_ref, kmean_ref):
    nb = kmean_ref.shape[0]
    L = MOBA_BLOCK
    group = pl.program_id(1)

    @pl.when(group == 0)
    def _():
        un = _rms(x_ref[...], g_ref[...]).astype(BF16)
        un_ref[...] = un
        ca, sa = ca_ref[...], sa_ref[...]

        c_q = _dot(un, w_lat_ref[:, :MLA_Q_RANK])
        c_kv = _dot(un, w_lat_ref[:, MLA_Q_RANK:])
        k_pe = _dot(un, w_kpe_ref[...])

        qn = _rms(c_q, gq_ref[...]).astype(BF16)
        kvn = _rms(c_kv, gkv_ref[...]).astype(BF16)
        q = _dot(qn, w_uq_ref[...])
        k_nope = _dot(kvn, w_uk_ref[...])
        v_ref[...] = _dot(kvn, w_uv_ref[...]).astype(BF16)
        for h in range(MLA_HEADS):
            lo = h * QK_PAD
            q_ref[:, lo:lo + LANES] = q[:, lo:lo + LANES].astype(BF16)
            q_ref[:, lo + LANES:lo + QK_PAD] = _rope(
                q[:, lo + LANES:lo + QK_PAD], ca, sa).astype(BF16)

        k_pe = _rope(k_pe, ca, sa).astype(BF16)
        for h in range(MLA_HEADS):
            lo = h * QK_PAD
            k_ref[:, lo:lo + LANES] = k_nope[:, h * LANES:(h + 1) * LANES].astype(BF16)
            k_ref[:, lo + LANES:lo + QK_PAD] = k_pe

    @pl.when(group == 1)
    def _():
        @pl.when(pl.program_id(0) == 0)
        def _():
            kmean_ref[...] = jnp.zeros_like(kmean_ref)

        un = un_ref[...]
        cm, sm = cm_ref[...], sm_ref[...]
        tm = un.shape[0]
        blk0 = (pl.program_id(0) * (tm // L)) % nb
        km = _dot(un, w_moba_ref[:, MOBA_WIDTH:2 * MOBA_WIDTH])
        qm = _dot(un, w_moba_ref[:, :MOBA_WIDTH])
        lane = lax.broadcasted_iota(jnp.int32, (L, LANES), 1)
        blk_row = lax.broadcasted_iota(jnp.int32, (nb, LANES), 0)
        for h in range(MOBA_HEADS):
            sl = slice(h * LANES, (h + 1) * LANES)
            lo = h * QK_PAD
            qh = _rope(qm[:, sl], cm, sm).astype(BF16)
            kh = _rope(km[:, sl], cm, sm)
            q_ref[:, lo:lo + LANES] = qh
            k_ref[:, lo:lo + LANES] = kh.astype(BF16)
            kmean = kmean_ref[:, sl]
            for part in range(tm // L):
                rows = slice(part * L, (part + 1) * L)
                blk = blk0 + part
                kmean = jnp.where(blk_row == blk, jnp.mean(kh[rows], axis=0, keepdims=True), kmean)
                q_ref[rows, lo + LANES:lo + QK_PAD] = _moba_bias(kmean, qh[rows], blk).astype(BF16)
                k_ref[rows, lo + LANES:lo + QK_PAD] = jnp.where(lane == blk, 1.0, 0.0).astype(BF16)
            kmean_ref[:, sl] = kmean
        v_ref[...] = _dot(un, w_moba_ref[:, 2 * MOBA_WIDTH:]).astype(BF16)


def _proj(x, g, w_lat, w_kpe, w_moba, gq, gkv, w_uq, w_uk, w_uv, ca, sa, cm, sm, *, seq, tm=512):
    T, D = x.shape
    n_in = w_lat.shape[1] + w_kpe.shape[1] + w_moba.shape[1]
    nb = seq // MOBA_BLOCK
    assert seq % tm == 0 and tm % MOBA_BLOCK == 0 and nb % SUBLANES_BF16 == 0 and nb <= LANES
    assert MLA_HEADS == MOBA_HEADS and MLA_WIDTH == MOBA_WIDTH
    wq, wv = MLA_HEADS * QK_PAD, MLA_WIDTH
    out = lambda w: pl.BlockSpec((tm, w), lambda i, grp: (i, grp))
    tab = pl.BlockSpec((tm, LANES), lambda i, grp: (i % (seq // tm), 0))
    vmem = (2 * tm * D * 4 + tm * D * 2
            + (D * n_in + MLA_Q_RANK * wq + 2 * MLA_KV_RANK * MLA_WIDTH) * 2
            + 2 * tm * (2 * wq + wv) * 2
            + 6 * tm * wq * 4
            + 8 * MIB)
    return pl.pallas_call(
        _proj_kernel,
        grid=(T // tm, 2),
        in_specs=[
            pl.BlockSpec((tm, D), lambda i, grp: (i, 0)), _const_spec((1, D)),
            _const_spec(w_lat.shape), _const_spec(w_kpe.shape), _const_spec(w_moba.shape),
            _const_spec((1, MLA_Q_RANK)), _const_spec((1, MLA_KV_RANK)),
            _const_spec((MLA_Q_RANK, wq)), _const_spec((MLA_KV_RANK, MLA_WIDTH)),
            _const_spec((MLA_KV_RANK, MLA_WIDTH)),
            tab, tab, tab, tab,
        ],
        out_specs=[out(wq), out(wq), out(wv)],
        out_shape=[
            jax.ShapeDtypeStruct((T, 2 * wq), BF16),
            jax.ShapeDtypeStruct((T, 2 * wq), BF16),
            jax.ShapeDtypeStruct((T, 2 * wv), BF16),
        ],
        scratch_shapes=[pltpu.VMEM((tm, D), BF16), pltpu.VMEM((nb, MOBA_WIDTH), F32)],
        compiler_params=pltpu.CompilerParams(
            dimension_semantics=("arbitrary", "arbitrary"), vmem_limit_bytes=vmem),
        name="proj",
    )(x, g, w_lat, w_kpe, w_moba, gq, gkv, w_uq, w_uk, w_uv, ca, sa, cm, sm)


LOG2_E = 1.4426950408889634


def _attn_kernel(*refs, t, nsub, nside):
    q_ref, k_ref, v_ref = refs[:3]
    side_in, (o_ref, *side_out) = refs[3:3 + nside], refs[3 + nside:4 + 2 * nside]
    m_ref, acc_ref = refs[4 + 2 * nside:]
    h = pl.program_id(1)
    i = pl.program_id(2)
    scale = jnp.where(h < MLA_HEADS, MLA_QK_DIM ** -0.5, MOBA_HEAD_DIM ** -0.5)
    c2 = (scale * LOG2_E).astype(F32)
    m_ref[...] = jnp.full(m_ref.shape, -jnp.inf, F32)
    acc_ref[...] = jnp.zeros_like(acc_ref)

    def logits(sub, start, tk):
        q = q_ref[sub * t:(sub + 1) * t, :]
        return _dot_nt(q, k_ref[pl.ds(pl.multiple_of(start, t), tk), :])

    def softmax(sub, s, diagonal):
        tk = s.shape[1]
        if diagonal:
            row = lax.broadcasted_iota(jnp.int32, s.shape, 0)
            col = lax.broadcasted_iota(jnp.int32, s.shape, 1)
            s = jnp.where(col <= row, s, NEG_INF)
        chunks = [s[:, c * LANES:(c + 1) * LANES] for c in range(tk // LANES)]
        m_cur = jnp.max(functools.reduce(jnp.maximum, chunks), axis=1, keepdims=True)
        m_prev = m_ref[sub]
        m_new = jnp.maximum(m_prev, jnp.broadcast_to(m_cur, (t, LANES)) * c2)
        alpha = jnp.exp2(m_prev - m_new)
        p = [jnp.exp2(c * c2 - m_new).astype(BF16) for c in chunks]
        m_ref[sub] = m_new
        return p, alpha

    def accumulate(sub, p, alpha, start):
        ones = jnp.ones((t, LANES), BF16)
        n = t // LANES
        pv = [_dot(jnp.concatenate(p[c * n:(c + 1) * n], axis=1),
                   jnp.concatenate([v_ref[pl.ds(pl.multiple_of(start + c * t, t), t), :], ones],
                                   axis=1)) for c in range(len(p) // n)]
        acc_ref[sub] = jnp.concatenate([alpha, alpha], axis=1) * acc_ref[sub] + sum(pv)

    def update(sub, s, start, diagonal):
        accumulate(sub, *softmax(sub, s, diagonal), start)

    def body(j, carry):
        start = j * 2 * t
        ss = [logits(sub, start, 2 * t) for sub in range(nsub)]
        for sub in range(nsub):
            update(sub, ss[sub], start, False)
        return carry

    lax.fori_loop(0, i * (nsub // 2), body, 0)
    for w_ref, wo_ref in zip(side_in, side_out):
        wo_ref[...] = w_ref[...].astype(BF16)
    first = i * nsub
    for kb in range(nsub):
        for sub in range(kb, nsub):
            start = (first + kb) * t
            update(sub, logits(sub, start, t), start, sub == kb)
    for sub in range(nsub):
        acc = acc_ref[sub]
        o_ref[sub * t:(sub + 1) * t, :] = acc[:, :V_DIM] / acc[:, V_DIM:]


def _side_block_rows(rows, steps):
    return next(rb for rb in range(SUBLANES_BF16, rows + 1, SUBLANES_BF16)
                if rows % rb == 0 and rows // rb <= steps)


def _attention(q, k, v, side_weights=(), *, t=512, nsub=4):
    B, S, _ = q.shape
    tq = nsub * t
    assert S % tq == 0 and t % MOBA_BLOCK == 0 and nsub % 2 == 0
    nq = S // tq
    steps = B * HEADS * nq
    kern = functools.partial(_attn_kernel, t=t, nsub=nsub, nside=len(side_weights))
    side_specs, side_bytes = [], 0
    for w in side_weights:
        rb = _side_block_rows(w.shape[0], steps)
        last = w.shape[0] // rb - 1
        side_specs.append(pl.BlockSpec(
            (rb, w.shape[1]),
            lambda b, h, i, last=last: (jnp.minimum((b * HEADS + h) * nq + i, last), 0)))
        side_bytes += 2 * rb * w.shape[1] * (4 + 2)
    vmem = (2 * S * (QK_PAD + V_DIM) * 2 + 2 * tq * QK_PAD * 2 + 2 * tq * V_DIM * 4
            + tq * (LANES + 2 * V_DIM) * 4 + nsub * 6 * t * t * 4 + side_bytes + 8 * MIB)
    out = pl.pallas_call(
        kern,
        grid=(B, HEADS, nq),
        in_specs=[
            pl.BlockSpec((None, tq, QK_PAD), lambda b, h, i: (b, i, h)),
            pl.BlockSpec((None, S, QK_PAD), lambda b, h, i: (b, 0, h)),
            pl.BlockSpec((None, S, V_DIM), lambda b, h, i: (b, 0, h)),
        ] + side_specs,
        out_specs=[pl.BlockSpec((None, tq, V_DIM), lambda b, h, i: (b, i, h))] + side_specs,
        out_shape=[jax.ShapeDtypeStruct((B, S, HEADS * V_DIM), F32)]
        + [jax.ShapeDtypeStruct(w.shape, BF16) for w in side_weights],
        scratch_shapes=[pltpu.VMEM((nsub, t, LANES), F32),
                        pltpu.VMEM((nsub, t, 2 * V_DIM), F32)],
        compiler_params=pltpu.CompilerParams(
            dimension_semantics=("arbitrary", "arbitrary", "arbitrary"), vmem_limit_bytes=vmem),
        name="attention",
    )(q, k, v, *side_weights)
    return out[0], tuple(out[1:])


def _out_kernel(x_ref, am_ref, ga_ref, gm_ref, w_ref, gp_ref, o_ref):
    an = _rms(am_ref[:, :MLA_WIDTH], ga_ref[...]).astype(BF16)
    mn = _rms(am_ref[:, MLA_WIDTH:], gm_ref[...]).astype(BF16)
    y = _dot(jnp.concatenate([an, mn], axis=1), w_ref[...])
    o_ref[...] = x_ref[...] + _rms(y, gp_ref[...])


def _out_proj(x, am, ga, gm, w, gp, *, tm=512):
    T, D = x.shape
    W = MLA_WIDTH + MOBA_WIDTH
    tok = lambda wd: pl.BlockSpec((tm, wd), lambda i: (i, 0))
    vmem = 2 * 2 * tm * D * 4 + 2 * tm * W * 4 + W * D * 2 + 4 * tm * D * 4 + 8 * MIB
    return pl.pallas_call(
        _out_kernel,
        grid=(T // tm,),
        in_specs=[tok(D), tok(W), _const_spec((1, MLA_WIDTH)), _const_spec((1, MOBA_WIDTH)),
                  _const_spec((W, D)), _const_spec((1, D))],
        out_specs=tok(D),
        out_shape=jax.ShapeDtypeStruct((T, D), F32),
        compiler_params=pltpu.CompilerParams(
            dimension_semantics=("parallel",), vmem_limit_bytes=vmem),
        name="out_proj",
    )(x, am, ga, gm, w, gp)


def _spread_rope_cols(w):
    half = MLA_ROPE_DIM // 2
    z = jnp.zeros(w.shape[:-1] + (LANES // 2 - half,), w.dtype)
    return jnp.concatenate([w[..., :half], z, w[..., half:], z], axis=-1)


def _rope_tables(seq):
    step = 64
    assert seq % step == 0
    lo = jnp.arange(step, dtype=F32)[:, None]
    hi = step * jnp.arange(seq // step, dtype=F32)[:, None]

    def inv_freq(dim):
        return 1.0 / (ROPE_THETA ** (jnp.arange(0, dim, 2, dtype=F32) / dim))

    def tables(inv, live):
        inv = jnp.concatenate([inv, inv])[None, :]
        sign = jnp.concatenate([-live, live])[None, :]
        ch, sh = jnp.cos(hi * inv)[:, None, :], jnp.sin(hi * inv)[:, None, :]
        cl, sl = jnp.cos(lo * inv)[None, :, :], jnp.sin(lo * inv)[None, :, :]
        cos = (ch * cl - sh * sl).reshape(seq, LANES)
        sin = (sh * cl + ch * sl).reshape(seq, LANES)
        return cos * jnp.abs(sign), sin * sign

    cm, sm = tables(inv_freq(MOBA_HEAD_DIM), jnp.ones((LANES // 2,), F32))
    pad = jnp.zeros((LANES // 2 - MLA_ROPE_DIM // 2,), F32)
    ca, sa = tables(jnp.concatenate([inv_freq(MLA_ROPE_DIM), pad]),
                    jnp.concatenate([jnp.ones((MLA_ROPE_DIM // 2,), F32), pad]))
    return ca, sa, cm, sm


def kernel(x, ffn1_pre_g, ffn1_w_gate, ffn1_w_up, ffn1_w_down, ffn1_post_g, mix_pre_g, w_in, mla_q_norm_g, mla_kv_norm_g, mla_w_uq, mla_w_ukv, mla_out_g, moba_out_g, w_out, mix_post_g, ffn2_pre_g, ffn2_w_gate, ffn2_w_up, ffn2_w_down, ffn2_post_g):
    B, S, D = x.shape
    depth = w_in.shape[0]
    H = MLA_HEADS
    ca, sa, cm, sm = _rope_tables(S)
    xt = x.reshape(B * S, D)
    for l in range(depth):
        o_kpe = MLA_Q_RANK + MLA_KV_RANK
        w_lat = w_in[l][:, :o_kpe].astype(BF16)
        w_kpe = _spread_rope_cols(w_in[l][:, o_kpe:o_kpe + MLA_ROPE_DIM]).astype(BF16)
        w_moba = w_in[l][:, o_kpe + MLA_ROPE_DIM:].astype(BF16)
        uq = mla_w_uq[l].reshape(MLA_Q_RANK, H, MLA_QK_DIM)
        w_uq = jnp.concatenate(
            [uq[..., :MLA_NOPE_DIM], _spread_rope_cols(uq[..., MLA_NOPE_DIM:])], axis=-1
        ).reshape(MLA_Q_RANK, H * QK_PAD).astype(BF16)
        ukv = mla_w_ukv[l].reshape(MLA_KV_RANK, H, MLA_NOPE_DIM + MLA_V_DIM)
        w_uk = ukv[..., :MLA_NOPE_DIM].reshape(MLA_KV_RANK, MLA_WIDTH).astype(BF16)
        w_uv = ukv[..., MLA_NOPE_DIM:].reshape(MLA_KV_RANK, MLA_WIDTH).astype(BF16)

        xt = _ffn(xt, ffn1_pre_g[l][None], ffn1_w_gate[l], ffn1_w_up[l], ffn1_w_down[l],
                  ffn1_post_g[l][None])
        q, k, v = _proj(xt, mix_pre_g[l][None], w_lat, w_kpe, w_moba, mla_q_norm_g[l][None],
                        mla_kv_norm_g[l][None], w_uq, w_uk, w_uv, ca, sa, cm, sm, seq=S)
        am, (wg2, wu2, wd2, wo) = _attention(
            q.reshape(B, S, -1), k.reshape(B, S, -1), v.reshape(B, S, -1),
            side_weights=(ffn2_w_gate[l], ffn2_w_up[l], ffn2_w_down[l], w_out[l]))
        xt = _out_proj(xt, am.reshape(B * S, -1), mla_out_g[l][None], moba_out_g[l][None],
                       wo, mix_post_g[l][None])
        xt = _ffn(xt, ffn2_pre_g[l][None], wg2, wu2, wd2, ffn2_post_g[l][None], tf=512)
    return xt.reshape(B, S, D)
```

```python
import functools

import jax
import jax.numpy as jnp
from jax import lax
from jax.experimental import pallas as pl
from jax.experimental.pallas import tpu as pltpu

MLA_HEADS = 8
MLA_Q_RANK = 512
MLA_KV_RANK = 256
MLA_NOPE_DIM = 128
MLA_ROPE_DIM = 64
MLA_V_DIM = 128
MLA_QK_DIM = MLA_NOPE_DIM + MLA_ROPE_DIM
MOBA_HEADS = 8
MOBA_HEAD_DIM = 128
MOBA_BLOCK = 256
MOBA_TOPK = 3
MLA_WIDTH = MLA_HEADS * MLA_V_DIM
MOBA_WIDTH = MOBA_HEADS * MOBA_HEAD_DIM
ROPE_THETA = 10000.0
NORM_EPS = 1e-6
NEG_INF = -1e30
MASK_BIAS = -(2.0 ** 100)

LANES = 128
SUBLANES_BF16 = 16
HEADS = MLA_HEADS + MOBA_HEADS
QK_PAD = 2 * LANES
V_DIM = 128
V7X_VMEM_BYTES = 64 * 2 ** 20
MIB = 2 ** 20

BF16 = jnp.bfloat16
F32 = jnp.float32


def _rms(x, g):
    return x * lax.rsqrt(jnp.mean(x * x, axis=-1, keepdims=True) + NORM_EPS) * g


def _dot(a, b):
    return jnp.dot(a, b, preferred_element_type=F32)


def _dot_nt(a, b):
    return lax.dot_general(a, b, (((1,), (1,)), ((), ())), preferred_element_type=F32)


def _const_spec(shape):
    return pl.BlockSpec(shape, lambda *_: (0,) * len(shape), pipeline_mode=pl.Buffered(1))


def _ffn_kernel(*refs, emit_weights, has_head):
    x_ref, pre_g_ref, wg_ref, wu_ref, wd_ref, post_g_ref = refs[:6]
    rest = list(refs[6:])
    head_ref = rest.pop(0) if has_head else None
    o_ref = rest.pop(0)
    w16_refs = [rest.pop(0) for _ in range(3)] if emit_weights else None
    (xn_ref,) = rest

    i = pl.program_id(0)
    f = pl.program_id(1)
    last = pl.num_programs(1) - 1
    tm = x_ref.shape[0]
    halves = [slice(0, tm // 2), slice(tm // 2, tm)]

    def when_computing(cond):
        return pl.when(cond & (i > 0)) if has_head else pl.when(cond)

    def weights():
        w = [r[...].astype(BF16) for r in (wg_ref, wu_ref, wd_ref)]
        if emit_weights:
            for r16, w16 in zip(w16_refs, w):
                r16[...] = w16
        return w

    def swiglu_down(xn, w):
        h = _dot(xn, w[0])
        u = _dot(xn, w[1])
        a = (h * jax.nn.sigmoid(h)) * u
        return _dot(a.astype(BF16), w[2])

    @when_computing(f == 0)
    def _():
        w = weights()
        for rows in halves:
            xn = _rms(x_ref[rows, :], pre_g_ref[...]).astype(BF16)
            xn_ref[rows, :] = xn
            o_ref[rows, :] = swiglu_down(xn, w)

    @when_computing((f > 0) & (f < last))
    def _():
        o_ref[...] += swiglu_down(xn_ref[...], weights())

    @when_computing(f == last)
    def _():
        w = weights()
        for rows in halves:
            acc = o_ref[rows, :] + swiglu_down(xn_ref[rows, :], w)
            o_ref[rows, :] = x_ref[rows, :] + 0.5 * _rms(acc, post_g_ref[...])

    if has_head:
        @pl.when((i == 0) & (f == last))
        def _():
            o_ref[...] = head_ref[...]


def _ffn(x, pre_g, wg, wu, wd, post_g, *, tf, tm=1024, tiles=None, emit_weights=False,
         head=None):
    T, D = x.shape
    F = wg.shape[1]
    tiles = T // tm if tiles is None else tiles
    assert not emit_weights or tiles == 1
    has_head = head is not None
    wbytes = wg.dtype.itemsize
    vmem = (2 * 2 * tm * D * 4
            + tm * D * 2
            + 2 * 3 * D * tf * wbytes
            + 3 * D * tf * 2 * (3 if emit_weights else 1)
            + has_head * tm * D * 4
            + 4 * tm * tf * 4 + tm * D * 4)
    vmem = min(vmem, V7X_VMEM_BYTES)
    row = (lambda i: jnp.maximum(i, 1)) if has_head else (lambda i: i)
    col = (lambda i, f: jnp.where(i == 0, 0, f)) if has_head else (lambda i, f: f)
    w_specs = [pl.BlockSpec((D, tf), lambda i, f: (0, col(i, f))),
               pl.BlockSpec((D, tf), lambda i, f: (0, col(i, f))),
               pl.BlockSpec((tf, D), lambda i, f: (col(i, f), 0))]
    in_specs = [pl.BlockSpec((tm, D), lambda i, f: (row(i), 0)),
                pl.BlockSpec((1, D), lambda i, f: (0, 0))] + w_specs + [
                pl.BlockSpec((1, D), lambda i, f: (0, 0))]
    out_specs = [pl.BlockSpec((tm, D), lambda i, f: (i, 0))]
    out_shape = [jax.ShapeDtypeStruct((tiles * tm, D), F32)]
    args = [x, pre_g, wg, wu, wd, post_g]
    if has_head:
        in_specs.append(_const_spec((tm, D)))
        args.append(head)
    if emit_weights:
        out_specs += w_specs
        out_shape += [jax.ShapeDtypeStruct(w.shape, BF16) for w in (wg, wu, wd)]
    out = pl.pallas_call(
        functools.partial(_ffn_kernel, emit_weights=emit_weights, has_head=has_head),
        grid=(tiles, F // tf),
        in_specs=in_specs,
        out_specs=out_specs,
        out_shape=out_shape,
        scratch_shapes=[pltpu.VMEM((tm, D), BF16)],
        compiler_params=pltpu.CompilerParams(
            dimension_semantics=("arbitrary", "arbitrary"), vmem_limit_bytes=vmem),
        name="ffn",
    )(*args)
    return out if emit_weights else out[0]


def _rope(x, c, s):
    return x * c + pltpu.roll(x, LANES // 2, axis=1) * s


def _moba_bias(kmean, qh, blk):
    nb = kmean.shape[0]
    gate = _dot_nt(kmean.astype(BF16), qh)
    n = lax.broadcasted_iota(jnp.int32, gate.shape, 0)
    cand = jnp.where(n < blk, gate, -jnp.inf)
    allowed = n == blk
    for _ in range(min(MOBA_TOPK, nb - 1)):
        best = jnp.max(cand, axis=0, keepdims=True)
        first = jnp.min(jnp.where(cand == best, n, nb), axis=0, keepdims=True)
        pick = (n == first) & (best > -jnp.inf)
        allowed = allowed | pick
        cand = jnp.where(pick, -jnp.inf, cand)
    bias = jnp.where(allowed, 0.0, MASK_BIAS)
    bias = jnp.concatenate([bias, jnp.zeros((LANES - nb, bias.shape[1]), F32)], axis=0)
    return bias.T


def _proj_kernel(x_ref, g_ref, w_lat_ref, w_kpe_ref, w_moba_ref, gq_ref, gkv_ref,
                 w_uq_ref, w_uk_ref, w_uv_ref, ca_ref, sa_ref, cm_ref, sm_ref,
                 q_ref, k_ref, v_ref, un_ref, kmean_ref):
    nb = kmean_ref.shape[0]
    L = MOBA_BLOCK
    group = pl.program_id(1)

    @pl.when(group == 0)
    def _():
        un = _rms(x_ref[...], g_ref[...]).astype(BF16)
        un_ref[...] = un
        ca, sa = ca_ref[...], sa_ref[...]

        c_q = _dot(un, w_lat_ref[:, :MLA_Q_RANK])
        c_kv = _dot(un, w_lat_ref[:, MLA_Q_RANK:])
        k_pe = _dot(un, w_kpe_ref[...])

        qn = _rms(c_q, gq_ref[...]).astype(BF16)
        kvn = _rms(c_kv, gkv_ref[...]).astype(BF16)
        q = _dot(qn, w_uq_ref[...])
        k_nope = _dot(kvn, w_uk_ref[...])
        v_ref[...] = _dot(kvn, w_uv_ref[...]).astype(BF16)
        for h in range(MLA_HEADS):
            lo = h * QK_PAD
            q_ref[:, lo:lo + LANES] = q[:, lo:lo + LANES].astype(BF16)
            q_ref[:, lo + LANES:lo + QK_PAD] = _rope(
                q[:, lo + LANES:lo + QK_PAD], ca, sa).astype(BF16)

        k_pe = _rope(k_pe, ca, sa).astype(BF16)
        for h in range(MLA_HEADS):
            lo = h * QK_PAD
            k_ref[:, lo:lo + LANES] = k_nope[:, h * LANES:(h + 1) * LANES].astype(BF16)
            k_ref[:, lo + LANES:lo + QK_PAD] = k_pe

    @pl.when(group == 1)
    def _():
        @pl.when(pl.program_id(0) == 0)
        def _():
            kmean_ref[...] = jnp.zeros_like(kmean_ref)

        un = un_ref[...]
        cm, sm = cm_ref[...], sm_ref[...]
        tm = un.shape[0]
        blk0 = (pl.program_id(0) * (tm // L)) % nb
        km = _dot(un, w_moba_ref[:, MOBA_WIDTH:2 * MOBA_WIDTH])
        qm = _dot(un, w_moba_ref[:, :MOBA_WIDTH])
        lane = lax.broadcasted_iota(jnp.int32, (L, LANES), 1)
        blk_row = lax.broadcasted_iota(jnp.int32, (nb, LANES), 0)
        for h in range(MOBA_HEADS):
            sl = slice(h * LANES, (h + 1) * LANES)
            lo = h * QK_PAD
            qh = _rope(qm[:, sl], cm, sm).astype(BF16)
            kh = _rope(km[:, sl], cm, sm)
            q_ref[:, lo:lo + LANES] = qh
            k_ref[:, lo:lo + LANES] = kh.astype(BF16)
            kmean = kmean_ref[:, sl]
            for part in range(tm // L):
                rows = slice(part * L, (part + 1) * L)
                blk = blk0 + part
                kmean = jnp.where(blk_row == blk, jnp.mean(kh[rows], axis=0, keepdims=True), kmean)
                q_ref[rows, lo + LANES:lo + QK_PAD] = _moba_bias(kmean, qh[rows], blk).astype(BF16)
                k_ref[rows, lo + LANES:lo + QK_PAD] = jnp.where(lane == blk, 1.0, 0.0).astype(BF16)
            kmean_ref[:, sl] = kmean
        v_ref[...] = _dot(un, w_moba_ref[:, 2 * MOBA_WIDTH:]).astype(BF16)


def _proj(x, g, w_lat, w_kpe, w_moba, gq, gkv, w_uq, w_uk, w_uv, ca, sa, cm, sm, *, seq, tm=512):
    T, D = x.shape
    n_in = w_lat.shape[1] + w_kpe.shape[1] + w_moba.shape[1]
    nb = seq // MOBA_BLOCK
    assert seq % tm == 0 and tm % MOBA_BLOCK == 0 and nb % SUBLANES_BF16 == 0 and nb <= LANES
    assert MLA_HEADS == MOBA_HEADS and MLA_WIDTH == MOBA_WIDTH
    wq, wv = MLA_HEADS * QK_PAD, MLA_WIDTH
    out = lambda w: pl.BlockSpec((tm, w), lambda i, grp: (i, grp))
    tab = pl.BlockSpec((tm, LANES), lambda i, grp: (i % (seq // tm), 0))
    vmem = (2 * tm * D * 4 + tm * D * 2
            + (D * n_in + MLA_Q_RANK * wq + 2 * MLA_KV_RANK * MLA_WIDTH) * 2
            + 2 * tm * (2 * wq + wv) * 2
            + 6 * tm * wq * 4
            + 8 * MIB)
    return pl.pallas_call(
        _proj_kernel,
        grid=(T // tm, 2),
        in_specs=[
            pl.BlockSpec((tm, D), lambda i, grp: (i, 0)), _const_spec((1, D)),
            _const_spec(w_lat.shape), _const_spec(w_kpe.shape), _const_spec(w_moba.shape),
            _const_spec((1, MLA_Q_RANK)), _const_spec((1, MLA_KV_RANK)),
            _const_spec((MLA_Q_RANK, wq)), _const_spec((MLA_KV_RANK, MLA_WIDTH)),
            _const_spec((MLA_KV_RANK, MLA_WIDTH)),
            tab, tab, tab, tab,
        ],
        out_specs=[out(wq), out(wq), out(wv)],
        out_shape=[
            jax.ShapeDtypeStruct((T, 2 * wq), BF16),
            jax.ShapeDtypeStruct((T, 2 * wq), BF16),
            jax.ShapeDtypeStruct((T, 2 * wv), BF16),
        ],
        scratch_shapes=[pltpu.VMEM((tm, D), BF16), pltpu.VMEM((nb, MOBA_WIDTH), F32)],
        compiler_params=pltpu.CompilerParams(
            dimension_semantics=("arbitrary", "arbitrary"), vmem_limit_bytes=vmem),
        name="proj",
    )(x, g, w_lat, w_kpe, w_moba, gq, gkv, w_uq, w_uk, w_uv, ca, sa, cm, sm)


LOG2_E = 1.4426950408889634


def _attn_kernel(*refs, t, nsub, nside):
    q_ref, k_ref, v_ref = refs[:3]
    side_in, (o_ref, *side_out) = refs[3:3 + nside], refs[3 + nside:4 + 2 * nside]
    m_ref, acc_ref = refs[4 + 2 * nside:]
    h = pl.program_id(1)
    i = pl.program_id(2)
    scale = jnp.where(h < MLA_HEADS, MLA_QK_DIM ** -0.5, MOBA_HEAD_DIM ** -0.5)
    c2 = (scale * LOG2_E).astype(F32)
    m_ref[...] = jnp.full(m_ref.shape, -jnp.inf, F32)
    acc_ref[...] = jnp.zeros_like(acc_ref)

    def logits(sub, start, tk):
        q = q_ref[sub * t:(sub + 1) * t, :]
        return _dot_nt(q, k_ref[pl.ds(pl.multiple_of(start, t), tk), :])

    def softmax(sub, s, diagonal):
        tk = s.shape[1]
        if diagonal:
            row = lax.broadcasted_iota(jnp.int32, s.shape, 0)
            col = lax.broadcasted_iota(jnp.int32, s.shape, 1)
            s = jnp.where(col <= row, s, NEG_INF)
        chunks = [s[:, c * LANES:(c + 1) * LANES] for c in range(tk // LANES)]
        m_cur = jnp.max(functools.reduce(jnp.maximum, chunks), axis=1, keepdims=True)
        m_prev = m_ref[sub]
        m_new = jnp.maximum(m_prev, jnp.broadcast_to(m_cur, (t, LANES)) * c2)
        alpha = jnp.exp2(m_prev - m_new)
        p = [jnp.exp2(c * c2 - m_new).astype(BF16) for c in chunks]
        m_ref[sub] = m_new
        return p, alpha

    def accumulate(sub, p, alpha, start):
        ones = jnp.ones((t, LANES), BF16)
        n = t // LANES
        pv = [_dot(jnp.concatenate(p[c * n:(c + 1) * n], axis=1),
                   jnp.concatenate([v_ref[pl.ds(pl.multiple_of(start + c * t, t), t), :], ones],
                                   axis=1)) for c in range(len(p) // n)]
        acc_ref[sub] = jnp.concatenate([alpha, alpha], axis=1) * acc_ref[sub] + sum(pv)

    def update(sub, s, start, diagonal):
        accumulate(sub, *softmax(sub, s, diagonal), start)

    def body(j, carry):
        start = j * 2 * t
        ss = [logits(sub, start, 2 * t) for sub in range(nsub)]
        for sub in range(nsub):
            update(sub, ss[sub], start, False)
        return carry

    lax.fori_loop(0, i * (nsub // 2), body, 0)
    for w_ref, wo_ref in zip(side_in, side_out):
        wo_ref[...] = w_ref[...].astype(BF16)
    first = i * nsub
    for kb in range(nsub):
        for sub in range(kb, nsub):
            start = (first + kb) * t
            update(sub, logits(sub, start, t), start, sub == kb)
    for sub in range(nsub):
        acc = acc_ref[sub]
        o_ref[sub * t:(sub + 1) * t, :] = acc[:, :V_DIM] / acc[:, V_DIM:]


def _side_block_rows(rows, steps):
    return next(rb for rb in range(SUBLANES_BF16, rows + 1, SUBLANES_BF16)
                if rows % rb == 0 and rows // rb <= steps)


def _attention(q, k, v, side_weights=(), *, t=512, nsub=4):
    B, S, _ = q.shape
    tq = nsub * t
    assert S % tq == 0 and t % MOBA_BLOCK == 0 and nsub % 2 == 0
    nq = S // tq
    steps = B * HEADS * nq
    kern = functools.partial(_attn_kernel, t=t, nsub=nsub, nside=len(side_weights))
    side_specs, side_bytes = [], 0
    for w in side_weights:
        rb = _side_block_rows(w.shape[0], steps)
        last = w.shape[0] // rb - 1
        side_specs.append(pl.BlockSpec(
            (rb, w.shape[1]),
            lambda b, h, i, last=last: (jnp.minimum((b * HEADS + h) * nq + i, last), 0)))
        side_bytes += 2 * rb * w.shape[1] * (4 + 2)
    vmem = (2 * S * (QK_PAD + V_DIM) * 2 + 2 * tq * QK_PAD * 2 + 2 * tq * V_DIM * 4
            + tq * (LANES + 2 * V_DIM) * 4 + nsub * 6 * t * t * 4 + side_bytes + 8 * MIB)
    out = pl.pallas_call(
        kern,
        grid=(B, HEADS, nq),
        in_specs=[
            pl.BlockSpec((None, tq, QK_PAD), lambda b, h, i: (b, i, h)),
            pl.BlockSpec((None, S, QK_PAD), lambda b, h, i: (b, 0, h)),
            pl.BlockSpec((None, S, V_DIM), lambda b, h, i: (b, 0, h)),
        ] + side_specs,
        out_specs=[pl.BlockSpec((None, tq, V_DIM), lambda b, h, i: (b, i, h))] + side_specs,
        out_shape=[jax.ShapeDtypeStruct((B, S, HEADS * V_DIM), F32)]
        + [jax.ShapeDtypeStruct(w.shape, BF16) for w in side_weights],
        scratch_shapes=[pltpu.VMEM((nsub, t, LANES), F32),
                        pltpu.VMEM((nsub, t, 2 * V_DIM), F32)],
        compiler_params=pltpu.CompilerParams(
            dimension_semantics=("arbitrary", "arbitrary", "arbitrary"), vmem_limit_bytes=vmem),
        name="attention",
    )(q, k, v, *side_weights)
    return out[0], tuple(out[1:])


def _out_kernel(x_ref, am_ref, ga_ref, gm_ref, w_ref, gp_ref, o_ref):
    an = _rms(am_ref[:, :MLA_WIDTH], ga_ref[...]).astype(BF16)
    mn = _rms(am_ref[:, MLA_WIDTH:], gm_ref[...]).astype(BF16)
    y = _dot(jnp.concatenate([an, mn], axis=1), w_ref[...])
    o_ref[...] = x_ref[...] + _rms(y, gp_ref[...])


def _out_proj(x, am, ga, gm, w, gp, *, tm=512):
    T, D = x.shape
    W = MLA_WIDTH + MOBA_WIDTH
    tok = lambda wd: pl.BlockSpec((tm, wd), lambda i: (i, 0))
    vmem = 2 * 2 * tm * D * 4 + 2 * tm * W * 4 + W * D * 2 + 4 * tm * D * 4 + 8 * MIB
    return pl.pallas_call(
        _out_kernel,
        grid=(T // tm,),
        in_specs=[tok(D), tok(W), _const_spec((1, MLA_WIDTH)), _const_spec((1, MOBA_WIDTH)),
                  _const_spec((W, D)), _const_spec((1, D))],
        out_specs=tok(D),
        out_shape=jax.ShapeDtypeStruct((T, D), F32),
        compiler_params=pltpu.CompilerParams(
            dimension_semantics=("parallel",), vmem_limit_bytes=vmem),
        name="out_proj",
    )(x, am, ga, gm, w, gp)


def _spread_rope_cols(w):
    half = MLA_ROPE_DIM // 2
    z = jnp.zeros(w.shape[:-1] + (LANES // 2 - half,), w.dtype)
    return jnp.concatenate([w[..., :half], z, w[..., half:], z], axis=-1)


def _rope_tables(seq):
    step = 64
    assert seq % step == 0
    lo = jnp.arange(step, dtype=F32)[:, None]
    hi = step * jnp.arange(seq // step, dtype=F32)[:, None]

    def inv_freq(dim):
        return 1.0 / (ROPE_THETA ** (jnp.arange(0, dim, 2, dtype=F32) / dim))

    def tables(inv, live):
        inv = jnp.concatenate([inv, inv])[None, :]
        sign = jnp.concatenate([-live, live])[None, :]
        ch, sh = jnp.cos(hi * inv)[:, None, :], jnp.sin(hi * inv)[:, None, :]
        cl, sl = jnp.cos(lo * inv)[None, :, :], jnp.sin(lo * inv)[None, :, :]
        cos = (ch * cl - sh * sl).reshape(seq, LANES)
        sin = (sh * cl + ch * sl).reshape(seq, LANES)
        return cos * jnp.abs(sign), sin * sign

    cm, sm = tables(inv_freq(MOBA_HEAD_DIM), jnp.ones((LANES // 2,), F32))
    pad = jnp.zeros((LANES // 2 - MLA_ROPE_DIM // 2,), F32)
    ca, sa = tables(jnp.concatenate([inv_freq(MLA_ROPE_DIM), pad]),
                    jnp.concatenate([jnp.ones((MLA_ROPE_DIM // 2,), F32), pad]))
    return ca, sa, cm, sm


def kernel(x, ffn1_pre_g, ffn1_w_gate, ffn1_w_up, ffn1_w_down, ffn1_post_g, mix_pre_g, w_in, mla_q_norm_g, mla_kv_norm_g, mla_w_uq, mla_w_ukv, mla_out_g, moba_out_g, w_out, mix_post_g, ffn2_pre_g, ffn2_w_gate, ffn2_w_up, ffn2_w_down, ffn2_post_g):
    B, S, D = x.shape
    depth = w_in.shape[0]
    H = MLA_HEADS
    ca, sa, cm, sm = _rope_tables(S)
    xt = x.reshape(B * S, D)
    for l in range(depth):
        o_kpe = MLA_Q_RANK + MLA_KV_RANK
        w_lat = w_in[l][:, :o_kpe].astype(BF16)
        w_kpe = _spread_rope_cols(w_in[l][:, o_kpe:o_kpe + MLA_ROPE_DIM]).astype(BF16)
        w_moba = w_in[l][:, o_kpe + MLA_ROPE_DIM:].astype(BF16)
        uq = mla_w_uq[l].reshape(MLA_Q_RANK, H, MLA_QK_DIM)
        w_uq = jnp.concatenate(
            [uq[..., :MLA_NOPE_DIM], _spread_rope_cols(uq[..., MLA_NOPE_DIM:])], axis=-1
        ).reshape(MLA_Q_RANK, H * QK_PAD).astype(BF16)
        ukv = mla_w_ukv[l].reshape(MLA_KV_RANK, H, MLA_NOPE_DIM + MLA_V_DIM)
        w_uk = ukv[..., :MLA_NOPE_DIM].reshape(MLA_KV_RANK, MLA_WIDTH).astype(BF16)
        w_uv = ukv[..., MLA_NOPE_DIM:].reshape(MLA_KV_RANK, MLA_WIDTH).astype(BF16)

        g1 = (ffn1_pre_g[l][None], ffn1_post_g[l][None])
        head, wg1, wu1, wd1 = _ffn(xt, g1[0], ffn1_w_gate[l], ffn1_w_up[l], ffn1_w_down[l], g1[1],
                                   tf=256, tiles=1, emit_weights=True)
        xt = _ffn(xt, g1[0], wg1, wu1, wd1, g1[1], tf=512, head=head)
        q, k, v = _proj(xt, mix_pre_g[l][None], w_lat, w_kpe, w_moba, mla_q_norm_g[l][None],
                        mla_kv_norm_g[l][None], w_uq, w_uk, w_uv, ca, sa, cm, sm, seq=S)
        am, (wg2, wu2, wd2, wo) = _attention(
            q.reshape(B, S, -1), k.reshape(B, S, -1), v.reshape(B, S, -1),
            side_weights=(ffn2_w_gate[l], ffn2_w_up[l], ffn2_w_down[l], w_out[l]))
        xt = _out_proj(xt, am.reshape(B * S, -1), mla_out_g[l][None], moba_out_g[l][None],
                       wo, mix_post_g[l][None])
        xt = _ffn(xt, ffn2_pre_g[l][None], wg2, wu2, wd2, ffn2_post_g[l][None], tf=512)
    return xt.reshape(B, S, D)
```

```python
import functools

import jax
import jax.numpy as jnp
from jax import lax
from jax.experimental import pallas as pl
from jax.experimental.pallas import tpu as pltpu

MLA_HEADS = 8
MLA_Q_RANK = 512
MLA_KV_RANK = 256
MLA_NOPE_DIM = 128
MLA_ROPE_DIM = 64
MLA_V_DIM = 128
MLA_QK_DIM = MLA_NOPE_DIM + MLA_ROPE_DIM
MOBA_HEADS = 8
MOBA_HEAD_DIM = 128
MOBA_BLOCK = 256
MOBA_TOPK = 3
MLA_WIDTH = MLA_HEADS * MLA_V_DIM
MOBA_WIDTH = MOBA_HEADS * MOBA_HEAD_DIM
ROPE_THETA = 10000.0
NORM_EPS = 1e-6
NEG_INF = -1e30
MASK_BIAS = -(2.0 ** 100)

LANES = 128
SUBLANES_BF16 = 16
HEADS = MLA_HEADS + MOBA_HEADS
QK_PAD = 2 * LANES
V_DIM = 128
V7X_VMEM_BYTES = 64 * 2 ** 20
SPILL_BYTES = 8 * 2 ** 20

BF16 = jnp.bfloat16
F32 = jnp.float32


def _rms(x, g):
    return x * lax.rsqrt(jnp.mean(x * x, axis=-1, keepdims=True) + NORM_EPS) * g


def _dot(a, b):
    return jnp.dot(a, b, preferred_element_type=F32)


def _dot_nt(a, b):
    return lax.dot_general(a, b, (((1,), (1,)), ((), ())), preferred_element_type=F32)


def _const_spec(shape):
    return pl.BlockSpec(shape, lambda *_: (0,) * len(shape), pipeline_mode=pl.Buffered(1))


def _ffn_kernel(*refs, emit_weights, has_head):
    x_ref, pre_g_ref, wg_ref, wu_ref, wd_ref, post_g_ref = refs[:6]
    rest = list(refs[6:])
    head_ref = rest.pop(0) if has_head else None
    o_ref = rest.pop(0)
    w16_refs = [rest.pop(0) for _ in range(3)] if emit_weights else None
    (xn_ref,) = rest

    i = pl.program_id(0)
    f = pl.program_id(1)
    last = pl.num_programs(1) - 1
    tm = x_ref.shape[0]
    halves = [slice(0, tm // 2), slice(tm // 2, tm)]

    def when_computing(cond):
        return pl.when(cond & (i > 0)) if has_head else pl.when(cond)

    def weights():
        w = [r[...].astype(BF16) for r in (wg_ref, wu_ref, wd_ref)]
        if emit_weights:
            for r16, w16 in zip(w16_refs, w):
                r16[...] = w16
        return w

    def swiglu_down(xn, w):
        h = _dot(xn, w[0])
        u = _dot(xn, w[1])
        a = (h * jax.nn.sigmoid(h)) * u
        return _dot(a.astype(BF16), w[2])

    @when_computing(f == 0)
    def _():
        w = weights()
        for rows in halves:
            xn = _rms(x_ref[rows, :], pre_g_ref[...]).astype(BF16)
            xn_ref[rows, :] = xn
            o_ref[rows, :] = swiglu_down(xn, w)

    @when_computing((f > 0) & (f < last))
    def _():
        o_ref[...] += swiglu_down(xn_ref[...], weights())

    @when_computing(f == last)
    def _():
        w = weights()
        for rows in halves:
            acc = o_ref[rows, :] + swiglu_down(xn_ref[rows, :], w)
            o_ref[rows, :] = x_ref[rows, :] + 0.5 * _rms(acc, post_g_ref[...])

    if has_head:
        @pl.when((i == 0) & (f == last))
        def _():
            o_ref[...] = head_ref[...]


def _ffn(x, pre_g, wg, wu, wd, post_g, *, tf, tm=1024, tiles=None, emit_weights=False,
         head=None):
    T, D = x.shape
    F = wg.shape[1]
    tiles = T // tm if tiles is None else tiles
    assert not emit_weights or tiles == 1
    has_head = head is not None
    wbytes = wg.dtype.itemsize
    vmem = (2 * 2 * tm * D * 4
            + tm * D * 2
            + 2 * 3 * D * tf * wbytes
            + 3 * D * tf * 2 * (3 if emit_weights else 1)
            + has_head * tm * D * 4
            + 4 * tm * tf * 4 + tm * D * 4)
    vmem = min(vmem, V7X_VMEM_BYTES)
    row = (lambda i: jnp.maximum(i, 1)) if has_head else (lambda i: i)
    col = (lambda i, f: jnp.where(i == 0, 0, f)) if has_head else (lambda i, f: f)
    w_specs = [pl.BlockSpec((D, tf), lambda i, f: (0, col(i, f))),
               pl.BlockSpec((D, tf), lambda i, f: (0, col(i, f))),
               pl.BlockSpec((tf, D), lambda i, f: (col(i, f), 0))]
    in_specs = [pl.BlockSpec((tm, D), lambda i, f: (row(i), 0)),
                pl.BlockSpec((1, D), lambda i, f: (0, 0))] + w_specs + [
                pl.BlockSpec((1, D), lambda i, f: (0, 0))]
    out_specs = [pl.BlockSpec((tm, D), lambda i, f: (i, 0))]
    out_shape = [jax.ShapeDtypeStruct((tiles * tm, D), F32)]
    args = [x, pre_g, wg, wu, wd, post_g]
    if has_head:
        in_specs.append(_const_spec((tm, D)))
        args.append(head)
    if emit_weights:
        out_specs += w_specs
        out_shape += [jax.ShapeDtypeStruct(w.shape, BF16) for w in (wg, wu, wd)]
    out = pl.pallas_call(
        functools.partial(_ffn_kernel, emit_weights=emit_weights, has_head=has_head),
        grid=(tiles, F // tf),
        in_specs=in_specs,
        out_specs=out_specs,
        out_shape=out_shape,
        scratch_shapes=[pltpu.VMEM((tm, D), BF16)],
        compiler_params=pltpu.CompilerParams(
            dimension_semantics=("arbitrary", "arbitrary"), vmem_limit_bytes=vmem),
        name="ffn",
    )(*args)
    return out if emit_weights else out[0]


def _rope(x, c, s):
    return x * c + pltpu.roll(x, LANES // 2, axis=1) * s


def _moba_bias(kmean, qh, blk):
    nb = kmean.shape[0]
    gate = _dot_nt(kmean.astype(BF16), qh)
    n = lax.broadcasted_iota(jnp.int32, gate.shape, 0)
    cand = jnp.where(n < blk, gate, -jnp.inf)
    allowed = n == blk
    for _ in range(min(MOBA_TOPK, nb - 1)):
        best = jnp.max(cand, axis=0, keepdims=True)
        first = jnp.min(jnp.where(cand == best, n, nb), axis=0, keepdims=True)
        pick = (n == first) & (best > -jnp.inf)
        allowed = allowed | pick
        cand = jnp.where(pick, -jnp.inf, cand)
    bias = jnp.where(allowed, 0.0, MASK_BIAS)
    bias = jnp.concatenate([bias, jnp.zeros((LANES - nb, bias.shape[1]), F32)], axis=0)
    return bias.T


def _proj_kernel(x_ref, g_ref, w_lat_ref, w_kpe_ref, w_moba_ref, gq_ref, gkv_ref,
                 w_uq_ref, w_uk_ref, w_uv_ref, ca_ref, sa_ref, cm_ref, sm_ref,
                 q_ref, k_ref, v_ref, un_ref, kmean_ref):
    nb = kmean_ref.shape[0]
    L = MOBA_BLOCK
    group = pl.program_id(1)

    @pl.when(group == 0)
    def _():
        un = _rms(x_ref[...], g_ref[...]).astype(BF16)
        un_ref[...] = un
        ca, sa = ca_ref[...], sa_ref[...]

        c_q = _dot(un, w_lat_ref[:, :MLA_Q_RANK])
        c_kv = _dot(un, w_lat_ref[:, MLA_Q_RANK:])
        k_pe = _dot(un, w_kpe_ref[...])

        qn = _rms(c_q, gq_ref[...]).astype(BF16)
        kvn = _rms(c_kv, gkv_ref[...]).astype(BF16)
        q = _dot(qn, w_uq_ref[...])
        k_nope = _dot(kvn, w_uk_ref[...])
        v_ref[...] = _dot(kvn, w_uv_ref[...]).astype(BF16)
        for h in range(MLA_HEADS):
            lo = h * QK_PAD
            q_ref[:, lo:lo + LANES] = q[:, lo:lo + LANES].astype(BF16)
            q_ref[:, lo + LANES:lo + QK_PAD] = _rope(
                q[:, lo + LANES:lo + QK_PAD], ca, sa).astype(BF16)

        k_pe = _rope(k_pe, ca, sa).astype(BF16)
        for h in range(MLA_HEADS):
            lo = h * QK_PAD
            k_ref[:, lo:lo + LANES] = k_nope[:, h * LANES:(h + 1) * LANES].astype(BF16)
            k_ref[:, lo + LANES:lo + QK_PAD] = k_pe

    @pl.when(group == 1)
    def _():
        @pl.when(pl.program_id(0) == 0)
        def _():
            kmean_ref[...] = jnp.zeros_like(kmean_ref)

        un = un_ref[...]
        cm, sm = cm_ref[...], sm_ref[...]
        tm = un.shape[0]
        blk0 = (pl.program_id(0) * (tm // L)) % nb
        km = _dot(un, w_moba_ref[:, MOBA_WIDTH:2 * MOBA_WIDTH])
        qm = _dot(un, w_moba_ref[:, :MOBA_WIDTH])
        lane = lax.broadcasted_iota(jnp.int32, (L, LANES), 1)
        blk_row = lax.broadcasted_iota(jnp.int32, (nb, LANES), 0)
        for h in range(MOBA_HEADS):
            sl = slice(h * LANES, (h + 1) * LANES)
            lo = h * QK_PAD
            qh = _rope(qm[:, sl], cm, sm).astype(BF16)
            kh = _rope(km[:, sl], cm, sm)
            q_ref[:, lo:lo + LANES] = qh
            k_ref[:, lo:lo + LANES] = kh.astype(BF16)
            kmean = kmean_ref[:, sl]
            for part in range(tm // L):
                rows = slice(part * L, (part + 1) * L)
                blk = blk0 + part
                kmean = jnp.where(blk_row == blk, jnp.mean(kh[rows], axis=0, keepdims=True), kmean)
                q_ref[rows, lo + LANES:lo + QK_PAD] = _moba_bias(kmean, qh[rows], blk).astype(BF16)
                k_ref[rows, lo + LANES:lo + QK_PAD] = jnp.where(lane == blk, 1.0, 0.0).astype(BF16)
            kmean_ref[:, sl] = kmean
        v_ref[...] = _dot(un, w_moba_ref[:, 2 * MOBA_WIDTH:]).astype(BF16)


def _proj(x, g, w_lat, w_kpe, w_moba, gq, gkv, w_uq, w_uk, w_uv, ca, sa, cm, sm, *, seq, tm=512):
    T, D = x.shape
    n_in = w_lat.shape[1] + w_kpe.shape[1] + w_moba.shape[1]
    nb = seq // MOBA_BLOCK
    assert seq % tm == 0 and tm % MOBA_BLOCK == 0 and nb % SUBLANES_BF16 == 0 and nb <= LANES
    assert MLA_HEADS == MOBA_HEADS and MLA_WIDTH == MOBA_WIDTH
    wq, wv = MLA_HEADS * QK_PAD, MLA_WIDTH
    out = lambda w: pl.BlockSpec((tm, w), lambda i, grp: (i, grp))
    tab = pl.BlockSpec((tm, LANES), lambda i, grp: (i % (seq // tm), 0))
    vmem = (2 * tm * D * 4 + tm * D * 2
            + (D * n_in + MLA_Q_RANK * wq + 2 * MLA_KV_RANK * MLA_WIDTH) * 2
            + 2 * tm * (2 * wq + wv) * 2
            + 6 * tm * wq * 4
            + SPILL_BYTES)
    return pl.pallas_call(
        _proj_kernel,
        grid=(T // tm, 2),
        in_specs=[
            pl.BlockSpec((tm, D), lambda i, grp: (i, 0)), _const_spec((1, D)),
            _const_spec(w_lat.shape), _const_spec(w_kpe.shape), _const_spec(w_moba.shape),
            _const_spec((1, MLA_Q_RANK)), _const_spec((1, MLA_KV_RANK)),
            _const_spec((MLA_Q_RANK, wq)), _const_spec((MLA_KV_RANK, MLA_WIDTH)),
            _const_spec((MLA_KV_RANK, MLA_WIDTH)),
            tab, tab, tab, tab,
        ],
        out_specs=[out(wq), out(wq), out(wv)],
        out_shape=[
            jax.ShapeDtypeStruct((T, 2 * wq), BF16),
            jax.ShapeDtypeStruct((T, 2 * wq), BF16),
            jax.ShapeDtypeStruct((T, 2 * wv), BF16),
        ],
        scratch_shapes=[pltpu.VMEM((tm, D), BF16), pltpu.VMEM((nb, MOBA_WIDTH), F32)],
        compiler_params=pltpu.CompilerParams(
            dimension_semantics=("arbitrary", "arbitrary"), vmem_limit_bytes=vmem),
        name="proj",
    )(x, g, w_lat, w_kpe, w_moba, gq, gkv, w_uq, w_uk, w_uv, ca, sa, cm, sm)


LOG2_E = 1.4426950408889634


def _attn_kernel(*refs, t, nsub, nside):
    q_ref, k_ref, v_ref = refs[:3]
    side_in, (o_ref, *side_out) = refs[3:3 + nside], refs[3 + nside:4 + 2 * nside]
    m_ref, acc_ref = refs[4 + 2 * nside:]
    h = pl.program_id(1)
    i = pl.program_id(2)
    scale = jnp.where(h < MLA_HEADS, MLA_QK_DIM ** -0.5, MOBA_HEAD_DIM ** -0.5)
    c2 = (scale * LOG2_E).astype(F32)
    m_ref[...] = jnp.full(m_ref.shape, -jnp.inf, F32)
    acc_ref[...] = jnp.zeros_like(acc_ref)

    def logits(sub, start, tk):
        q = q_ref[sub * t:(sub + 1) * t, :]
        return _dot_nt(q, k_ref[pl.ds(pl.multiple_of(start, t), tk), :])

    def softmax(sub, s, diagonal):
        tk = s.shape[1]
        if diagonal:
            row = lax.broadcasted_iota(jnp.int32, s.shape, 0)
            col = lax.broadcasted_iota(jnp.int32, s.shape, 1)
            s = jnp.where(col <= row, s, NEG_INF)
        chunks = [s[:, c * LANES:(c + 1) * LANES] for c in range(tk // LANES)]
        m_cur = jnp.max(functools.reduce(jnp.maximum, chunks), axis=1, keepdims=True)
        m_prev = m_ref[sub]
        m_new = jnp.maximum(m_prev, jnp.broadcast_to(m_cur, (t, LANES)) * c2)
        alpha = jnp.exp2(m_prev - m_new)
        p = [jnp.exp2(c * c2 - m_new).astype(BF16) for c in chunks]
        m_ref[sub] = m_new
        return p, alpha

    def accumulate(sub, p, alpha, start):
        ones = jnp.ones((t, LANES), BF16)
        n = t // LANES
        pv = [_dot(jnp.concatenate(p[c * n:(c + 1) * n], axis=1),
                   jnp.concatenate([v_ref[pl.ds(pl.multiple_of(start + c * t, t), t), :], ones],
                                   axis=1)) for c in range(len(p) // n)]
        acc_ref[sub] = jnp.concatenate([alpha, alpha], axis=1) * acc_ref[sub] + sum(pv)

    def update(sub, s, start, diagonal):
        accumulate(sub, *softmax(sub, s, diagonal), start)

    def body(j, carry):
        start = j * 2 * t
        ss = [logits(sub, start, 2 * t) for sub in range(nsub)]
        for sub in range(nsub):
            update(sub, ss[sub], start, False)
        return carry

    lax.fori_loop(0, i * (nsub // 2), body, 0)
    for w_ref, wo_ref in zip(side_in, side_out):
        wo_ref[...] = w_ref[...].astype(BF16)
    first = i * nsub
    for kb in range(nsub):
        for sub in range(kb, nsub):
            start = (first + kb) * t
            update(sub, logits(sub, start, t), start, sub == kb)
    for sub in range(nsub):
        acc = acc_ref[sub]
        o_ref[sub * t:(sub + 1) * t, :] = acc[:, :V_DIM] / acc[:, V_DIM:]


def _side_block_rows(rows, steps):
    return next(rb for rb in range(SUBLANES_BF16, rows + 1, SUBLANES_BF16)
                if rows % rb == 0 and rows // rb <= steps)


def _attention(q, k, v, side_weights=(), *, t=512, nsub=4):
    B, S, _ = q.shape
    tq = nsub * t
    assert S % tq == 0 and t % MOBA_BLOCK == 0 and nsub % 2 == 0
    nq = S // tq
    steps = B * HEADS * nq
    kern = functools.partial(_attn_kernel, t=t, nsub=nsub, nside=len(side_weights))
    side_specs, side_bytes = [], 0
    for w in side_weights:
        rb = _side_block_rows(w.shape[0], steps)
        last = w.shape[0] // rb - 1
        side_specs.append(pl.BlockSpec(
            (rb, w.shape[1]),
            lambda b, h, i, last=last: (jnp.minimum((b * HEADS + h) * nq + i, last), 0)))
        side_bytes += 2 * rb * w.shape[1] * (4 + 2)
    vmem = (2 * S * (QK_PAD + V_DIM) * 2 + 2 * tq * QK_PAD * 2 + 2 * tq * V_DIM * 4
            + tq * (LANES + 2 * V_DIM) * 4 + nsub * 6 * t * t * 4 + side_bytes + SPILL_BYTES)
    out = pl.pallas_call(
        kern,
        grid=(B, HEADS, nq),
        in_specs=[
            pl.BlockSpec((None, tq, QK_PAD), lambda b, h, i: (b, i, h)),
            pl.BlockSpec((None, S, QK_PAD), lambda b, h, i: (b, 0, h)),
            pl.BlockSpec((None, S, V_DIM), lambda b, h, i: (b, 0, h)),
        ] + side_specs,
        out_specs=[pl.BlockSpec((None, tq, V_DIM), lambda b, h, i: (b, i, h))] + side_specs,
        out_shape=[jax.ShapeDtypeStruct((B, S, HEADS * V_DIM), F32)]
        + [jax.ShapeDtypeStruct(w.shape, BF16) for w in side_weights],
        scratch_shapes=[pltpu.VMEM((nsub, t, LANES), F32),
                        pltpu.VMEM((nsub, t, 2 * V_DIM), F32)],
        compiler_params=pltpu.CompilerParams(
            dimension_semantics=("arbitrary", "arbitrary", "arbitrary"), vmem_limit_bytes=vmem),
        name="attention",
    )(q, k, v, *side_weights)
    return out[0], tuple(out[1:])


def _out_kernel(x_ref, am_ref, ga_ref, gm_ref, w_ref, gp_ref, o_ref):
    an = _rms(am_ref[:, :MLA_WIDTH], ga_ref[...]).astype(BF16)
    mn = _rms(am_ref[:, MLA_WIDTH:], gm_ref[...]).astype(BF16)
    y = _dot(jnp.concatenate([an, mn], axis=1), w_ref[...])
    o_ref[...] = x_ref[...] + _rms(y, gp_ref[...])


def _out_proj(x, am, ga, gm, w, gp, *, tm=512):
    T, D = x.shape
    W = MLA_WIDTH + MOBA_WIDTH
    tok = lambda wd: pl.BlockSpec((tm, wd), lambda i: (i, 0))
    vmem = 2 * 2 * tm * D * 4 + 2 * tm * W * 4 + W * D * 2 + 4 * tm * D * 4 + SPILL_BYTES
    return pl.pallas_call(
        _out_kernel,
        grid=(T // tm,),
        in_specs=[tok(D), tok(W), _const_spec((1, MLA_WIDTH)), _const_spec((1, MOBA_WIDTH)),
                  _const_spec((W, D)), _const_spec((1, D))],
        out_specs=tok(D),
        out_shape=jax.ShapeDtypeStruct((T, D), F32),
        compiler_params=pltpu.CompilerParams(
            dimension_semantics=("parallel",), vmem_limit_bytes=vmem),
        name="out_proj",
    )(x, am, ga, gm, w, gp)


def _spread_rope_cols(w):
    half = MLA_ROPE_DIM // 2
    z = jnp.zeros(w.shape[:-1] + (LANES // 2 - half,), w.dtype)
    return jnp.concatenate([w[..., :half], z, w[..., half:], z], axis=-1)


def _rope_tables(seq):
    step = 64
    assert seq % step == 0
    lo = jnp.arange(step, dtype=F32)[:, None]
    hi = step * jnp.arange(seq // step, dtype=F32)[:, None]

    def inv_freq(dim):
        return 1.0 / (ROPE_THETA ** (jnp.arange(0, dim, 2, dtype=F32) / dim))

    def tables(inv, live):
        inv = jnp.concatenate([inv, inv])[None, :]
        sign = jnp.concatenate([-live, live])[None, :]
        ch, sh = jnp.cos(hi * inv)[:, None, :], jnp.sin(hi * inv)[:, None, :]
        cl, sl = jnp.cos(lo * inv)[None, :, :], jnp.sin(lo * inv)[None, :, :]
        cos = (ch * cl - sh * sl).reshape(seq, LANES)
        sin = (sh * cl + ch * sl).reshape(seq, LANES)
        return cos * jnp.abs(sign), sin * sign

    cm, sm = tables(inv_freq(MOBA_HEAD_DIM), jnp.ones((LANES // 2,), F32))
    pad = jnp.zeros((LANES // 2 - MLA_ROPE_DIM // 2,), F32)
    ca, sa = tables(jnp.concatenate([inv_freq(MLA_ROPE_DIM), pad]),
                    jnp.concatenate([jnp.ones((MLA_ROPE_DIM // 2,), F32), pad]))
    return ca, sa, cm, sm


def kernel(x, ffn1_pre_g, ffn1_w_gate, ffn1_w_up, ffn1_w_down, ffn1_post_g, mix_pre_g, w_in, mla_q_norm_g, mla_kv_norm_g, mla_w_uq, mla_w_ukv, mla_out_g, moba_out_g, w_out, mix_post_g, ffn2_pre_g, ffn2_w_gate, ffn2_w_up, ffn2_w_down, ffn2_post_g):
    B, S, D = x.shape
    depth = w_in.shape[0]
    H = MLA_HEADS
    ca, sa, cm, sm = _rope_tables(S)
    xt = x.reshape(B * S, D)
    for l in range(depth):
        o_kpe = MLA_Q_RANK + MLA_KV_RANK
        w_lat = w_in[l][:, :o_kpe].astype(BF16)
        w_kpe = _spread_rope_cols(w_in[l][:, o_kpe:o_kpe + MLA_ROPE_DIM]).astype(BF16)
        w_moba = w_in[l][:, o_kpe + MLA_ROPE_DIM:].astype(BF16)
        uq = mla_w_uq[l].reshape(MLA_Q_RANK, H, MLA_QK_DIM)
        w_uq = jnp.concatenate(
            [uq[..., :MLA_NOPE_DIM], _spread_rope_cols(uq[..., MLA_NOPE_DIM:])], axis=-1
        ).reshape(MLA_Q_RANK, H * QK_PAD).astype(BF16)
        ukv = mla_w_ukv[l].reshape(MLA_KV_RANK, H, MLA_NOPE_DIM + MLA_V_DIM)
        w_uk = ukv[..., :MLA_NOPE_DIM].reshape(MLA_KV_RANK, MLA_WIDTH).astype(BF16)
        w_uv = ukv[..., MLA_NOPE_DIM:].reshape(MLA_KV_RANK, MLA_WIDTH).astype(BF16)

        g1 = (ffn1_pre_g[l][None], ffn1_post_g[l][None])
        head, wg1, wu1, wd1 = _ffn(xt, g1[0], ffn1_w_gate[l], ffn1_w_up[l], ffn1_w_down[l], g1[1],
                                   tf=256, tiles=1, emit_weights=True)
        xt = _ffn(xt, g1[0], wg1, wu1, wd1, g1[1], tf=512, head=head)
        q, k, v = _proj(xt, mix_pre_g[l][None], w_lat, w_kpe, w_moba, mla_q_norm_g[l][None],
                        mla_kv_norm_g[l][None], w_uq, w_uk, w_uv, ca, sa, cm, sm, seq=S)
        am, (wg2, wu2, wd2, wo) = _attention(
            q.reshape(B, S, -1), k.reshape(B, S, -1), v.reshape(B, S, -1),
            side_weights=(ffn2_w_gate[l], ffn2_w_up[l], ffn2_w_down[l], w_out[l]))
        xt = _out_proj(xt, am.reshape(B * S, -1), mla_out_g[l][None], moba_out_g[l][None],
                       wo, mix_post_g[l][None])
        xt = _ffn(xt, ffn2_pre_g[l][None], wg2, wu2, wd2, ffn2_post_g[l][None], tf=512)
    return xt.reshape(B, S, D)
```

```python
import functools

import jax
import jax.numpy as jnp
from jax import lax
from jax.experimental import pallas as pl
from jax.experimental.pallas import tpu as pltpu

MLA_HEADS = 8
MLA_Q_RANK = 512
MLA_KV_RANK = 256
MLA_NOPE_DIM = 128
MLA_ROPE_DIM = 64
MLA_V_DIM = 128
MLA_QK_DIM = MLA_NOPE_DIM + MLA_ROPE_DIM
MOBA_HEADS = 8
MOBA_HEAD_DIM = 128
MOBA_BLOCK = 256
MOBA_TOPK = 3
MLA_WIDTH = MLA_HEADS * MLA_V_DIM
MOBA_WIDTH = MOBA_HEADS * MOBA_HEAD_DIM
ROPE_THETA = 10000.0
NORM_EPS = 1e-6
NEG_INF = -1e30
MASK_BIAS = -(2.0 ** 100)

LANES = 128
SUBLANES_BF16 = 16
HEADS = MLA_HEADS + MOBA_HEADS
QK_PAD = 2 * LANES
V_DIM = 128
V7X_VMEM_BYTES = 64 * 2 ** 20
SPILL_BYTES = 8 * 2 ** 20

BF16 = jnp.bfloat16
F32 = jnp.float32


def _rms(x, g):
    return x * lax.rsqrt(jnp.mean(x * x, axis=-1, keepdims=True) + NORM_EPS) * g


def _dot(a, b):
    return jnp.dot(a, b, preferred_element_type=F32)


def _dot_nt(a, b):
    return lax.dot_general(a, b, (((1,), (1,)), ((), ())), preferred_element_type=F32)


def _const_spec(shape):
    return pl.BlockSpec(shape, lambda *_: (0,) * len(shape), pipeline_mode=pl.Buffered(1))


def _ffn_kernel(*refs, emit_weights, has_head):
    x_ref, pre_g_ref, wg_ref, wu_ref, wd_ref, post_g_ref = refs[:6]
    rest = list(refs[6:])
    head_ref = rest.pop(0) if has_head else None
    o_ref = rest.pop(0)
    w16_refs = [rest.pop(0) for _ in range(3)] if emit_weights else None
    (xn_ref,) = rest

    i = pl.program_id(0)
    f = pl.program_id(1)
    last = pl.num_programs(1) - 1
    tm = x_ref.shape[0]
    halves = [slice(0, tm // 2), slice(tm // 2, tm)]

    def when_computing(cond):
        return pl.when(cond & (i > 0)) if has_head else pl.when(cond)

    def weights():
        w = [r[...].astype(BF16) for r in (wg_ref, wu_ref, wd_ref)]
        if emit_weights:
            for r16, w16 in zip(w16_refs, w):
                r16[...] = w16
        return w

    def swiglu_down(xn, w):
        h = _dot(xn, w[0])
        u = _dot(xn, w[1])
        a = (h * jax.nn.sigmoid(h)) * u
        return _dot(a.astype(BF16), w[2])

    @when_computing(f == 0)
    def _():
        w = weights()
        for rows in halves:
            xn = _rms(x_ref[rows, :], pre_g_ref[...]).astype(BF16)
            xn_ref[rows, :] = xn
            o_ref[rows, :] = swiglu_down(xn, w)

    @when_computing((f > 0) & (f < last))
    def _():
        o_ref[...] += swiglu_down(xn_ref[...], weights())

    @when_computing(f == last)
    def _():
        w = weights()
        for rows in halves:
            acc = o_ref[rows, :] + swiglu_down(xn_ref[rows, :], w)
            o_ref[rows, :] = x_ref[rows, :] + 0.5 * _rms(acc, post_g_ref[...])

    if has_head:
        @pl.when((i == 0) & (f == last))
        def _():
            o_ref[...] = head_ref[...]


def _ffn(x, pre_g, wg, wu, wd, post_g, *, tf, tm=1024, tiles=None, emit_weights=False,
         head=None):
    T, D = x.shape
    F = wg.shape[1]
    tiles = T // tm if tiles is None else tiles
    assert not emit_weights or tiles == 1
    has_head = head is not None
    wbytes = wg.dtype.itemsize
    vmem = (2 * 2 * tm * D * 4
            + tm * D * 2
            + 2 * 3 * D * tf * wbytes
            + 3 * D * tf * 2 * (3 if emit_weights else 1)
            + has_head * tm * D * 4
            + 4 * tm * tf * 4 + tm * D * 4)
    vmem = min(vmem, V7X_VMEM_BYTES)
    row = (lambda i: jnp.maximum(i, 1)) if has_head else (lambda i: i)
    col = (lambda i, f: jnp.where(i == 0, 0, f)) if has_head else (lambda i, f: f)
    w_specs = [pl.BlockSpec((D, tf), lambda i, f: (0, col(i, f))),
               pl.BlockSpec((D, tf), lambda i, f: (0, col(i, f))),
               pl.BlockSpec((tf, D), lambda i, f: (col(i, f), 0))]
    in_specs = [pl.BlockSpec((tm, D), lambda i, f: (row(i), 0)),
                pl.BlockSpec((1, D), lambda i, f: (0, 0))] + w_specs + [
                pl.BlockSpec((1, D), lambda i, f: (0, 0))]
    out_specs = [pl.BlockSpec((tm, D), lambda i, f: (i, 0))]
    out_shape = [jax.ShapeDtypeStruct((tiles * tm, D), F32)]
    args = [x, pre_g, wg, wu, wd, post_g]
    if has_head:
        in_specs.append(_const_spec((tm, D)))
        args.append(head)
    if emit_weights:
        out_specs += w_specs
        out_shape += [jax.ShapeDtypeStruct(w.shape, BF16) for w in (wg, wu, wd)]
    out = pl.pallas_call(
        functools.partial(_ffn_kernel, emit_weights=emit_weights, has_head=has_head),
        grid=(tiles, F // tf),
        in_specs=in_specs,
        out_specs=out_specs,
        out_shape=out_shape,
        scratch_shapes=[pltpu.VMEM((tm, D), BF16)],
        compiler_params=pltpu.CompilerParams(
            dimension_semantics=("arbitrary", "arbitrary"), vmem_limit_bytes=vmem),
        name="ffn",
    )(*args)
    return out if emit_weights else out[0]


def _rope(x, c, s):
    return x * c + pltpu.roll(x, LANES // 2, axis=1) * s


def _moba_bias(kmean, qh, blk):
    nb = kmean.shape[0]
    gate = _dot_nt(kmean.astype(BF16), qh)
    n = lax.broadcasted_iota(jnp.int32, gate.shape, 0)
    cand = jnp.where(n < blk, gate, -jnp.inf)
    allowed = n == blk
    for _ in range(min(MOBA_TOPK, nb - 1)):
        best = jnp.max(cand, axis=0, keepdims=True)
        first = jnp.min(jnp.where(cand == best, n, nb), axis=0, keepdims=True)
        pick = (n == first) & (best > -jnp.inf)
        allowed = allowed | pick
        cand = jnp.where(pick, -jnp.inf, cand)
    bias = jnp.where(allowed, 0.0, MASK_BIAS)
    bias = jnp.concatenate([bias, jnp.zeros((LANES - nb, bias.shape[1]), F32)], axis=0)
    return bias.T


def _proj_kernel(x_ref, g_ref, w_lat_ref, w_kpe_ref, w_moba_ref, gq_ref, gkv_ref,
                 w_uq_ref, w_uk_ref, w_uv_ref, ca_ref, sa_ref, cm_ref, sm_ref,
                 q_ref, k_ref, v_ref, un_ref, kmean_ref):
    nb = kmean_ref.shape[0]
    L = MOBA_BLOCK
    group = pl.program_id(1)

    @pl.when(group == 0)
    def _():
        un = _rms(x_ref[...], g_ref[...]).astype(BF16)
        un_ref[...] = un
        ca, sa = ca_ref[...], sa_ref[...]

        c_q = _dot_nt(un, w_lat_ref[:MLA_Q_RANK, :])
        c_kv = _dot_nt(un, w_lat_ref[MLA_Q_RANK:, :])
        k_pe = _dot_nt(un, w_kpe_ref[...])

        qn = _rms(c_q, gq_ref[...]).astype(BF16)
        kvn = _rms(c_kv, gkv_ref[...]).astype(BF16)
        q = _dot(qn, w_uq_ref[...])
        k_nope = _dot(kvn, w_uk_ref[...])
        v_ref[...] = _dot(kvn, w_uv_ref[...]).astype(BF16)
        for h in range(MLA_HEADS):
            lo = h * QK_PAD
            q_ref[:, lo:lo + LANES] = q[:, lo:lo + LANES].astype(BF16)
            q_ref[:, lo + LANES:lo + QK_PAD] = _rope(
                q[:, lo + LANES:lo + QK_PAD], ca, sa).astype(BF16)

        k_pe = _rope(k_pe, ca, sa).astype(BF16)
        for h in range(MLA_HEADS):
            lo = h * QK_PAD
            k_ref[:, lo:lo + LANES] = k_nope[:, h * LANES:(h + 1) * LANES].astype(BF16)
            k_ref[:, lo + LANES:lo + QK_PAD] = k_pe

    @pl.when(group == 1)
    def _():
        @pl.when(pl.program_id(0) == 0)
        def _():
            kmean_ref[...] = jnp.zeros_like(kmean_ref)

        un = un_ref[...]
        cm, sm = cm_ref[...], sm_ref[...]
        tm = un.shape[0]
        blk0 = (pl.program_id(0) * (tm // L)) % nb
        km = _dot_nt(un, w_moba_ref[MOBA_WIDTH:2 * MOBA_WIDTH, :])
        qm = _dot_nt(un, w_moba_ref[:MOBA_WIDTH, :])
        lane = lax.broadcasted_iota(jnp.int32, (L, LANES), 1)
        blk_row = lax.broadcasted_iota(jnp.int32, (nb, LANES), 0)
        for h in range(MOBA_HEADS):
            sl = slice(h * LANES, (h + 1) * LANES)
            lo = h * QK_PAD
            qh = _rope(qm[:, sl], cm, sm).astype(BF16)
            kh = _rope(km[:, sl], cm, sm)
            q_ref[:, lo:lo + LANES] = qh
            k_ref[:, lo:lo + LANES] = kh.astype(BF16)
            kmean = kmean_ref[:, sl]
            for part in range(tm // L):
                rows = slice(part * L, (part + 1) * L)
                blk = blk0 + part
                kmean = jnp.where(blk_row == blk, jnp.mean(kh[rows], axis=0, keepdims=True), kmean)
                q_ref[rows, lo + LANES:lo + QK_PAD] = _moba_bias(kmean, qh[rows], blk).astype(BF16)
                k_ref[rows, lo + LANES:lo + QK_PAD] = jnp.where(lane == blk, 1.0, 0.0).astype(BF16)
            kmean_ref[:, sl] = kmean
        v_ref[...] = _dot_nt(un, w_moba_ref[2 * MOBA_WIDTH:, :]).astype(BF16)


def _proj(x, g, w_lat, w_kpe, w_moba, gq, gkv, w_uq, w_uk, w_uv, ca, sa, cm, sm, *, seq, tm=512):
    T, D = x.shape
    n_in = w_lat.shape[0] + w_kpe.shape[0] + w_moba.shape[0]
    nb = seq // MOBA_BLOCK
    assert seq % tm == 0 and tm % MOBA_BLOCK == 0 and nb % SUBLANES_BF16 == 0 and nb <= LANES
    assert MLA_HEADS == MOBA_HEADS and MLA_WIDTH == MOBA_WIDTH
    wq, wv = MLA_HEADS * QK_PAD, MLA_WIDTH
    out = lambda w: pl.BlockSpec((tm, w), lambda i, grp: (i, grp))
    tab = pl.BlockSpec((tm, LANES), lambda i, grp: (i % (seq // tm), 0))
    vmem = (2 * tm * D * 4 + tm * D * 2
            + (D * n_in + MLA_Q_RANK * wq + 2 * MLA_KV_RANK * MLA_WIDTH) * 2
            + 2 * tm * (2 * wq + wv) * 2
            + 6 * tm * wq * 4
            + SPILL_BYTES)
    return pl.pallas_call(
        _proj_kernel,
        grid=(T // tm, 2),
        in_specs=[
            pl.BlockSpec((tm, D), lambda i, grp: (i, 0)), _const_spec((1, D)),
            _const_spec(w_lat.shape), _const_spec(w_kpe.shape), _const_spec(w_moba.shape),
            _const_spec((1, MLA_Q_RANK)), _const_spec((1, MLA_KV_RANK)),
            _const_spec((MLA_Q_RANK, wq)), _const_spec((MLA_KV_RANK, MLA_WIDTH)),
            _const_spec((MLA_KV_RANK, MLA_WIDTH)),
            tab, tab, tab, tab,
        ],
        out_specs=[out(wq), out(wq), out(wv)],
        out_shape=[
            jax.ShapeDtypeStruct((T, 2 * wq), BF16),
            jax.ShapeDtypeStruct((T, 2 * wq), BF16),
            jax.ShapeDtypeStruct((T, 2 * wv), BF16),
        ],
        scratch_shapes=[pltpu.VMEM((tm, D), BF16), pltpu.VMEM((nb, MOBA_WIDTH), F32)],
        compiler_params=pltpu.CompilerParams(
            dimension_semantics=("arbitrary", "arbitrary"), vmem_limit_bytes=vmem),
        name="proj",
    )(x, g, w_lat, w_kpe, w_moba, gq, gkv, w_uq, w_uk, w_uv, ca, sa, cm, sm)


LOG2_E = 1.4426950408889634


def _attn_kernel(*refs, t, nsub, nside):
    q_ref, k_ref, v_ref = refs[:3]
    side_in, (o_ref, *side_out) = refs[3:3 + nside], refs[3 + nside:4 + 2 * nside]
    m_ref, acc_ref = refs[4 + 2 * nside:]
    h = pl.program_id(1)
    i = pl.program_id(2)
    scale = jnp.where(h < MLA_HEADS, MLA_QK_DIM ** -0.5, MOBA_HEAD_DIM ** -0.5)
    c2 = (scale * LOG2_E).astype(F32)
    m_ref[...] = jnp.full(m_ref.shape, -jnp.inf, F32)
    acc_ref[...] = jnp.zeros_like(acc_ref)

    def logits(sub, start, tk):
        q = q_ref[sub * t:(sub + 1) * t, :]
        return _dot_nt(q, k_ref[pl.ds(pl.multiple_of(start, t), tk), :])

    def softmax(sub, s, diagonal):
        tk = s.shape[1]
        if diagonal:
            row = lax.broadcasted_iota(jnp.int32, s.shape, 0)
            col = lax.broadcasted_iota(jnp.int32, s.shape, 1)
            s = jnp.where(col <= row, s, NEG_INF)
        chunks = [s[:, c * LANES:(c + 1) * LANES] for c in range(tk // LANES)]
        m_cur = jnp.max(functools.reduce(jnp.maximum, chunks), axis=1, keepdims=True)
        m_prev = m_ref[sub]
        m_new = jnp.maximum(m_prev, jnp.broadcast_to(m_cur, (t, LANES)) * c2)
        alpha = jnp.exp2(m_prev - m_new)
        p = [jnp.exp2(c * c2 - m_new).astype(BF16) for c in chunks]
        m_ref[sub] = m_new
        return p, alpha

    def accumulate(sub, p, alpha, start):
        ones = jnp.ones((t, LANES), BF16)
        n = t // LANES
        pv = [_dot(jnp.concatenate(p[c * n:(c + 1) * n], axis=1),
                   jnp.concatenate([v_ref[pl.ds(pl.multiple_of(start + c * t, t), t), :], ones],
                                   axis=1)) for c in range(len(p) // n)]
        acc_ref[sub] = jnp.concatenate([alpha, alpha], axis=1) * acc_ref[sub] + sum(pv)

    def update(sub, s, start, diagonal):
        accumulate(sub, *softmax(sub, s, diagonal), start)

    def body(j, carry):
        start = j * 2 * t
        ss = [logits(sub, start, 2 * t) for sub in range(nsub)]
        for sub in range(nsub):
            update(sub, ss[sub], start, False)
        return carry

    lax.fori_loop(0, i * (nsub // 2), body, 0)
    for w_ref, wo_ref in zip(side_in, side_out):
        wo_ref[...] = w_ref[...].astype(BF16)
    first = i * nsub
    for kb in range(nsub):
        for sub in range(kb, nsub):
            start = (first + kb) * t
            update(sub, logits(sub, start, t), start, sub == kb)
    for sub in range(nsub):
        acc = acc_ref[sub]
        o_ref[sub * t:(sub + 1) * t, :] = acc[:, :V_DIM] / acc[:, V_DIM:]


def _side_block_rows(rows, steps):
    return next(rb for rb in range(SUBLANES_BF16, rows + 1, SUBLANES_BF16)
                if rows % rb == 0 and rows // rb <= steps)


def _attention(q, k, v, side_weights=(), *, t=512, nsub=4):
    B, S, _ = q.shape
    tq = nsub * t
    assert S % tq == 0 and t % MOBA_BLOCK == 0 and nsub % 2 == 0
    nq = S // tq
    steps = B * HEADS * nq
    kern = functools.partial(_attn_kernel, t=t, nsub=nsub, nside=len(side_weights))
    side_specs, side_bytes = [], 0
    for w in side_weights:
        rb = _side_block_rows(w.shape[0], steps)
        last = w.shape[0] // rb - 1
        side_specs.append(pl.BlockSpec(
            (rb, w.shape[1]),
            lambda b, h, i, last=last: (jnp.minimum((b * HEADS + h) * nq + i, last), 0)))
        side_bytes += 2 * rb * w.shape[1] * (4 + 2)
    vmem = (2 * S * (QK_PAD + V_DIM) * 2 + 2 * tq * QK_PAD * 2 + 2 * tq * V_DIM * 4
            + tq * (LANES + 2 * V_DIM) * 4 + nsub * 6 * t * t * 4 + side_bytes + SPILL_BYTES)
    out = pl.pallas_call(
        kern,
        grid=(B, HEADS, nq),
        in_specs=[
            pl.BlockSpec((None, tq, QK_PAD), lambda b, h, i: (b, i, h)),
            pl.BlockSpec((None, S, QK_PAD), lambda b, h, i: (b, 0, h)),
            pl.BlockSpec((None, S, V_DIM), lambda b, h, i: (b, 0, h)),
        ] + side_specs,
        out_specs=[pl.BlockSpec((None, tq, V_DIM), lambda b, h, i: (b, i, h))] + side_specs,
        out_shape=[jax.ShapeDtypeStruct((B, S, HEADS * V_DIM), F32)]
        + [jax.ShapeDtypeStruct(w.shape, BF16) for w in side_weights],
        scratch_shapes=[pltpu.VMEM((nsub, t, LANES), F32),
                        pltpu.VMEM((nsub, t, 2 * V_DIM), F32)],
        compiler_params=pltpu.CompilerParams(
            dimension_semantics=("arbitrary", "arbitrary", "arbitrary"), vmem_limit_bytes=vmem),
        name="attention",
    )(q, k, v, *side_weights)
    return out[0], tuple(out[1:])


def _out_kernel(x_ref, am_ref, ga_ref, gm_ref, w_ref, gp_ref, o_ref):
    an = _rms(am_ref[:, :MLA_WIDTH], ga_ref[...]).astype(BF16)
    mn = _rms(am_ref[:, MLA_WIDTH:], gm_ref[...]).astype(BF16)
    y = _dot(jnp.concatenate([an, mn], axis=1), w_ref[...])
    o_ref[...] = x_ref[...] + _rms(y, gp_ref[...])


def _out_proj(x, am, ga, gm, w, gp, *, tm=512):
    T, D = x.shape
    W = MLA_WIDTH + MOBA_WIDTH
    tok = lambda wd: pl.BlockSpec((tm, wd), lambda i: (i, 0))
    vmem = 2 * 2 * tm * D * 4 + 2 * tm * W * 4 + W * D * 2 + 4 * tm * D * 4 + SPILL_BYTES
    return pl.pallas_call(
        _out_kernel,
        grid=(T // tm,),
        in_specs=[tok(D), tok(W), _const_spec((1, MLA_WIDTH)), _const_spec((1, MOBA_WIDTH)),
                  _const_spec((W, D)), _const_spec((1, D))],
        out_specs=tok(D),
        out_shape=jax.ShapeDtypeStruct((T, D), F32),
        compiler_params=pltpu.CompilerParams(
            dimension_semantics=("parallel",), vmem_limit_bytes=vmem),
        name="out_proj",
    )(x, am, ga, gm, w, gp)


def _spread_rope_cols(w):
    half = MLA_ROPE_DIM // 2
    z = jnp.zeros(w.shape[:-1] + (LANES // 2 - half,), w.dtype)
    return jnp.concatenate([w[..., :half], z, w[..., half:], z], axis=-1)


def _rope_tables(seq):
    step = 64
    assert seq % step == 0
    lo = jnp.arange(step, dtype=F32)[:, None]
    hi = step * jnp.arange(seq // step, dtype=F32)[:, None]

    def inv_freq(dim):
        return 1.0 / (ROPE_THETA ** (jnp.arange(0, dim, 2, dtype=F32) / dim))

    def tables(inv, live):
        inv = jnp.concatenate([inv, inv])[None, :]
        sign = jnp.concatenate([-live, live])[None, :]
        ch, sh = jnp.cos(hi * inv)[:, None, :], jnp.sin(hi * inv)[:, None, :]
        cl, sl = jnp.cos(lo * inv)[None, :, :], jnp.sin(lo * inv)[None, :, :]
        cos = (ch * cl - sh * sl).reshape(seq, LANES)
        sin = (sh * cl + ch * sl).reshape(seq, LANES)
        return cos * jnp.abs(sign), sin * sign

    cm, sm = tables(inv_freq(MOBA_HEAD_DIM), jnp.ones((LANES // 2,), F32))
    pad = jnp.zeros((LANES // 2 - MLA_ROPE_DIM // 2,), F32)
    ca, sa = tables(jnp.concatenate([inv_freq(MLA_ROPE_DIM), pad]),
                    jnp.concatenate([jnp.ones((MLA_ROPE_DIM // 2,), F32), pad]))
    return ca, sa, cm, sm


def kernel(x, ffn1_pre_g, ffn1_w_gate, ffn1_w_up, ffn1_w_down, ffn1_post_g, mix_pre_g, w_in, mla_q_norm_g, mla_kv_norm_g, mla_w_uq, mla_w_ukv, mla_out_g, moba_out_g, w_out, mix_post_g, ffn2_pre_g, ffn2_w_gate, ffn2_w_up, ffn2_w_down, ffn2_post_g):
    B, S, D = x.shape
    depth = w_in.shape[0]
    H = MLA_HEADS
    ca, sa, cm, sm = _rope_tables(S)
    xt = x.reshape(B * S, D)
    for l in range(depth):
        o_kpe = MLA_Q_RANK + MLA_KV_RANK
        w_in_t = w_in[l].T
        w_lat = w_in_t[:o_kpe].astype(BF16)
        w_kpe = _spread_rope_cols(w_in_t[o_kpe:o_kpe + MLA_ROPE_DIM].T).T.astype(BF16)
        w_moba = w_in_t[o_kpe + MLA_ROPE_DIM:].astype(BF16)
        uq = mla_w_uq[l].reshape(MLA_Q_RANK, H, MLA_QK_DIM)
        w_uq = jnp.concatenate(
            [uq[..., :MLA_NOPE_DIM], _spread_rope_cols(uq[..., MLA_NOPE_DIM:])], axis=-1
        ).reshape(MLA_Q_RANK, H * QK_PAD).astype(BF16)
        ukv = mla_w_ukv[l].reshape(MLA_KV_RANK, H, MLA_NOPE_DIM + MLA_V_DIM)
        w_uk = ukv[..., :MLA_NOPE_DIM].reshape(MLA_KV_RANK, MLA_WIDTH).astype(BF16)
        w_uv = ukv[..., MLA_NOPE_DIM:].reshape(MLA_KV_RANK, MLA_WIDTH).astype(BF16)

        g1 = (ffn1_pre_g[l][None], ffn1_post_g[l][None])
        head, wg1, wu1, wd1 = _ffn(xt, g1[0], ffn1_w_gate[l], ffn1_w_up[l], ffn1_w_down[l], g1[1],
                                   tf=256, tiles=1, emit_weights=True)
        xt = _ffn(xt, g1[0], wg1, wu1, wd1, g1[1], tf=512, head=head)
        q, k, v = _proj(xt, mix_pre_g[l][None], w_lat, w_kpe, w_moba, mla_q_norm_g[l][None],
                        mla_kv_norm_g[l][None], w_uq, w_uk, w_uv, ca, sa, cm, sm, seq=S)
        am, (wg2, wu2, wd2, wo) = _attention(
            q.reshape(B, S, -1), k.reshape(B, S, -1), v.reshape(B, S, -1),
            side_weights=(ffn2_w_gate[l], ffn2_w_up[l], ffn2_w_down[l], w_out[l]))
        xt = _out_proj(xt, am.reshape(B * S, -1), mla_out_g[l][None], moba_out_g[l][None],
                       wo, mix_post_g[l][None])
        xt = _ffn(xt, ffn2_pre_g[l][None], wg2, wu2, wd2, ffn2_post_g[l][None], tf=512)
    return xt.reshape(B, S, D)
```

```python
import functools

import jax
import jax.numpy as jnp
from jax import lax
from jax.experimental import pallas as pl
from jax.experimental.pallas import tpu as pltpu

MLA_HEADS = 8
MLA_Q_RANK = 512
MLA_KV_RANK = 256
MLA_NOPE_DIM = 128
MLA_ROPE_DIM = 64
MLA_V_DIM = 128
MLA_QK_DIM = MLA_NOPE_DIM + MLA_ROPE_DIM
MOBA_HEADS = 8
MOBA_HEAD_DIM = 128
MOBA_BLOCK = 256
MOBA_TOPK = 3
MLA_WIDTH = MLA_HEADS * MLA_V_DIM
MOBA_WIDTH = MOBA_HEADS * MOBA_HEAD_DIM
ROPE_THETA = 10000.0
NORM_EPS = 1e-6
NEG_INF = -1e30
MASK_BIAS = -(2.0 ** 100)

LANES = 128
SUBLANES_BF16 = 16
HEADS = MLA_HEADS + MOBA_HEADS
QK_PAD = 2 * LANES
V_DIM = 128
V7X_VMEM_BYTES = 64 * 2 ** 20
SPILL_BYTES = 8 * 2 ** 20

BF16 = jnp.bfloat16
F32 = jnp.float32


def _rms(x, g):
    return x * lax.rsqrt(jnp.mean(x * x, axis=-1, keepdims=True) + NORM_EPS) * g


def _dot(a, b):
    return jnp.dot(a, b, preferred_element_type=F32)


def _dot_nt(a, b):
    return lax.dot_general(a, b, (((1,), (1,)), ((), ())), preferred_element_type=F32)


def _const_spec(shape):
    return pl.BlockSpec(shape, lambda *_: (0,) * len(shape), pipeline_mode=pl.Buffered(1))


def _ffn_kernel(*refs, emit_weights, has_head):
    x_ref, pre_g_ref, wg_ref, wu_ref, wd_ref, post_g_ref = refs[:6]
    rest = list(refs[6:])
    head_ref = rest.pop(0) if has_head else None
    o_ref = rest.pop(0)
    w16_refs = [rest.pop(0) for _ in range(3)] if emit_weights else None
    (xn_ref,) = rest

    i = pl.program_id(0)
    f = pl.program_id(1)
    last = pl.num_programs(1) - 1
    tm = x_ref.shape[0]
    halves = [slice(0, tm // 2), slice(tm // 2, tm)]

    def when_computing(cond):
        return pl.when(cond & (i > 0)) if has_head else pl.when(cond)

    def weights():
        w = [r[...].astype(BF16) for r in (wg_ref, wu_ref, wd_ref)]
        if emit_weights:
            for r16, w16 in zip(w16_refs, w):
                r16[...] = w16
        return w

    def swiglu_down(xn, w):
        h = _dot(xn, w[0])
        u = _dot(xn, w[1])
        a = (h * jax.nn.sigmoid(h)) * u
        return _dot(a.astype(BF16), w[2])

    @when_computing(f == 0)
    def _():
        w = weights()
        for rows in halves:
            xn = _rms(x_ref[rows, :], pre_g_ref[...]).astype(BF16)
            xn_ref[rows, :] = xn
            o_ref[rows, :] = swiglu_down(xn, w)

    @when_computing((f > 0) & (f < last))
    def _():
        o_ref[...] += swiglu_down(xn_ref[...], weights())

    @when_computing(f == last)
    def _():
        w = weights()
        for rows in halves:
            acc = o_ref[rows, :] + swiglu_down(xn_ref[rows, :], w)
            o_ref[rows, :] = x_ref[rows, :] + 0.5 * _rms(acc, post_g_ref[...])

    if has_head:
        @pl.when((i == 0) & (f == last))
        def _():
            o_ref[...] = head_ref[...]


def _ffn(x, pre_g, wg, wu, wd, post_g, *, tf, tm=1024, tiles=None, emit_weights=False,
         head=None):
    T, D = x.shape
    F = wg.shape[1]
    tiles = T // tm if tiles is None else tiles
    assert not emit_weights or tiles == 1
    has_head = head is not None
    wbytes = wg.dtype.itemsize
    vmem = (2 * 2 * tm * D * 4
            + tm * D * 2
            + 2 * 3 * D * tf * wbytes
            + 3 * D * tf * 2 * (3 if emit_weights else 1)
            + has_head * tm * D * 4
            + 4 * tm * tf * 4 + tm * D * 4)
    vmem = min(vmem, V7X_VMEM_BYTES)
    row = (lambda i: jnp.maximum(i, 1)) if has_head else (lambda i: i)
    col = (lambda i, f: jnp.where(i == 0, 0, f)) if has_head else (lambda i, f: f)
    w_specs = [pl.BlockSpec((D, tf), lambda i, f: (0, col(i, f))),
               pl.BlockSpec((D, tf), lambda i, f: (0, col(i, f))),
               pl.BlockSpec((tf, D), lambda i, f: (col(i, f), 0))]
    in_specs = [pl.BlockSpec((tm, D), lambda i, f: (row(i), 0)),
                pl.BlockSpec((1, D), lambda i, f: (0, 0))] + w_specs + [
                pl.BlockSpec((1, D), lambda i, f: (0, 0))]
    out_specs = [pl.BlockSpec((tm, D), lambda i, f: (i, 0))]
    out_shape = [jax.ShapeDtypeStruct((tiles * tm, D), F32)]
    args = [x, pre_g, wg, wu, wd, post_g]
    if has_head:
        in_specs.append(_const_spec((tm, D)))
        args.append(head)
    if emit_weights:
        out_specs += w_specs
        out_shape += [jax.ShapeDtypeStruct(w.shape, BF16) for w in (wg, wu, wd)]
    out = pl.pallas_call(
        functools.partial(_ffn_kernel, emit_weights=emit_weights, has_head=has_head),
        grid=(tiles, F // tf),
        in_specs=in_specs,
        out_specs=out_specs,
        out_shape=out_shape,
        scratch_shapes=[pltpu.VMEM((tm, D), BF16)],
        compiler_params=pltpu.CompilerParams(
            dimension_semantics=("arbitrary", "arbitrary"), vmem_limit_bytes=vmem),
        name="ffn",
    )(*args)
    return out if emit_weights else out[0]


def _rope(x, c, s):
    return x * c + pltpu.roll(x, LANES // 2, axis=1) * s


def _moba_bias(kmean, qh, blk):
    nb = kmean.shape[0]
    gate = _dot_nt(kmean.astype(BF16), qh)
    n = lax.broadcasted_iota(jnp.int32, gate.shape, 0)
    cand = jnp.where(n < blk, gate, -jnp.inf)
    allowed = n == blk
    for _ in range(min(MOBA_TOPK, nb - 1)):
        best = jnp.max(cand, axis=0, keepdims=True)
        first = jnp.min(jnp.where(cand == best, n, nb), axis=0, keepdims=True)
        pick = (n == first) & (best > -jnp.inf)
        allowed = allowed | pick
        cand = jnp.where(pick, -jnp.inf, cand)
    bias = jnp.where(allowed, 0.0, MASK_BIAS)
    bias = jnp.concatenate([bias, jnp.zeros((LANES - nb, bias.shape[1]), F32)], axis=0)
    return bias.T


def _proj_kernel(x_ref, g_ref, w_lat_ref, w_kpe_ref, w_moba_ref, gq_ref, gkv_ref,
                 w_uq_ref, w_uk_ref, w_uv_ref, ca_ref, sa_ref, cm_ref, sm_ref,
                 q_ref, k_ref, v_ref, un_ref, kmean_ref):
    nb = kmean_ref.shape[0]
    L = MOBA_BLOCK
    group = pl.program_id(1)

    @pl.when(group == 0)
    def _():
        un = _rms(x_ref[...], g_ref[...]).astype(BF16)
        un_ref[...] = un
        ca, sa = ca_ref[...], sa_ref[...]

        c_q = _dot_nt(un, w_lat_ref[:MLA_Q_RANK, :])
        c_kv = _dot_nt(un, w_lat_ref[MLA_Q_RANK:, :])
        k_pe = _dot_nt(un, w_kpe_ref[...])

        qn = _rms(c_q, gq_ref[...]).astype(BF16)
        kvn = _rms(c_kv, gkv_ref[...]).astype(BF16)
        q = _dot(qn, w_uq_ref[...])
        k_nope = _dot(kvn, w_uk_ref[...])
        v_ref[...] = _dot(kvn, w_uv_ref[...]).astype(BF16)
        for h in range(MLA_HEADS):
            lo = h * QK_PAD
            q_ref[:, lo:lo + LANES] = q[:, lo:lo + LANES].astype(BF16)
            q_ref[:, lo + LANES:lo + QK_PAD] = _rope(
                q[:, lo + LANES:lo + QK_PAD], ca, sa).astype(BF16)

        k_pe = _rope(k_pe, ca, sa).astype(BF16)
        for h in range(MLA_HEADS):
            lo = h * QK_PAD
            k_ref[:, lo:lo + LANES] = k_nope[:, h * LANES:(h + 1) * LANES].astype(BF16)
            k_ref[:, lo + LANES:lo + QK_PAD] = k_pe

    @pl.when(group == 1)
    def _():
        @pl.when(pl.program_id(0) == 0)
        def _():
            kmean_ref[...] = jnp.zeros_like(kmean_ref)

        un = un_ref[...]
        cm, sm = cm_ref[...], sm_ref[...]
        tm = un.shape[0]
        blk0 = (pl.program_id(0) * (tm // L)) % nb
        km = _dot_nt(un, w_moba_ref[MOBA_WIDTH:2 * MOBA_WIDTH, :])
        qm = _dot_nt(un, w_moba_ref[:MOBA_WIDTH, :])
        lane = lax.broadcasted_iota(jnp.int32, (L, LANES), 1)
        blk_row = lax.broadcasted_iota(jnp.int32, (nb, LANES), 0)
        for h in range(MOBA_HEADS):
            sl = slice(h * LANES, (h + 1) * LANES)
            lo = h * QK_PAD
            qh = _rope(qm[:, sl], cm, sm).astype(BF16)
            kh = _rope(km[:, sl], cm, sm)
            q_ref[:, lo:lo + LANES] = qh
            k_ref[:, lo:lo + LANES] = kh.astype(BF16)
            kmean = kmean_ref[:, sl]
            for part in range(tm // L):
                rows = slice(part * L, (part + 1) * L)
                blk = blk0 + part
                kmean = jnp.where(blk_row == blk, jnp.mean(kh[rows], axis=0, keepdims=True), kmean)
                q_ref[rows, lo + LANES:lo + QK_PAD] = _moba_bias(kmean, qh[rows], blk).astype(BF16)
                k_ref[rows, lo + LANES:lo + QK_PAD] = jnp.where(lane == blk, 1.0, 0.0).astype(BF16)
            kmean_ref[:, sl] = kmean
        v_ref[...] = _dot_nt(un, w_moba_ref[2 * MOBA_WIDTH:, :]).astype(BF16)


def _proj(x, g, w_in_t, w_kpe, gq, gkv, w_uq, w_uk, w_uv, ca, sa, cm, sm, *, seq, tm=512):
    T, D = x.shape
    n_lat = MLA_Q_RANK + MLA_KV_RANK
    n_moba = 3 * MOBA_WIDTH
    assert w_in_t.shape == (n_lat + MLA_ROPE_DIM + n_moba, D)
    n_in = n_lat + w_kpe.shape[0] + n_moba

    def rows(start, size):
        return pl.BlockSpec((pl.Element(size), pl.Element(D)), lambda i, grp: (start, 0),
                            pipeline_mode=pl.Buffered(1))
    nb = seq // MOBA_BLOCK
    assert seq % tm == 0 and tm % MOBA_BLOCK == 0 and nb % SUBLANES_BF16 == 0 and nb <= LANES
    assert MLA_HEADS == MOBA_HEADS and MLA_WIDTH == MOBA_WIDTH
    wq, wv = MLA_HEADS * QK_PAD, MLA_WIDTH
    out = lambda w: pl.BlockSpec((tm, w), lambda i, grp: (i, grp))
    tab = pl.BlockSpec((tm, LANES), lambda i, grp: (i % (seq // tm), 0))
    vmem = (2 * tm * D * 4 + tm * D * 2
            + (D * n_in + MLA_Q_RANK * wq + 2 * MLA_KV_RANK * MLA_WIDTH) * 2
            + 2 * tm * (2 * wq + wv) * 2
            + 6 * tm * wq * 4
            + SPILL_BYTES)
    return pl.pallas_call(
        _proj_kernel,
        grid=(T // tm, 2),
        in_specs=[
            pl.BlockSpec((tm, D), lambda i, grp: (i, 0)), _const_spec((1, D)),
            rows(0, n_lat), _const_spec(w_kpe.shape), rows(n_lat + MLA_ROPE_DIM, n_moba),
            _const_spec((1, MLA_Q_RANK)), _const_spec((1, MLA_KV_RANK)),
            _const_spec((MLA_Q_RANK, wq)), _const_spec((MLA_KV_RANK, MLA_WIDTH)),
            _const_spec((MLA_KV_RANK, MLA_WIDTH)),
            tab, tab, tab, tab,
        ],
        out_specs=[out(wq), out(wq), out(wv)],
        out_shape=[
            jax.ShapeDtypeStruct((T, 2 * wq), BF16),
            jax.ShapeDtypeStruct((T, 2 * wq), BF16),
            jax.ShapeDtypeStruct((T, 2 * wv), BF16),
        ],
        scratch_shapes=[pltpu.VMEM((tm, D), BF16), pltpu.VMEM((nb, MOBA_WIDTH), F32)],
        compiler_params=pltpu.CompilerParams(
            dimension_semantics=("arbitrary", "arbitrary"), vmem_limit_bytes=vmem),
        name="proj",
    )(x, g, w_in_t, w_kpe, w_in_t, gq, gkv, w_uq, w_uk, w_uv, ca, sa, cm, sm)


LOG2_E = 1.4426950408889634


def _attn_kernel(*refs, t, nsub, nside):
    q_ref, k_ref, v_ref = refs[:3]
    side_in, (o_ref, *side_out) = refs[3:3 + nside], refs[3 + nside:4 + 2 * nside]
    m_ref, acc_ref = refs[4 + 2 * nside:]
    h = pl.program_id(1)
    i = pl.program_id(2)
    scale = jnp.where(h < MLA_HEADS, MLA_QK_DIM ** -0.5, MOBA_HEAD_DIM ** -0.5)
    c2 = (scale * LOG2_E).astype(F32)
    m_ref[...] = jnp.full(m_ref.shape, -jnp.inf, F32)
    acc_ref[...] = jnp.zeros_like(acc_ref)

    def logits(sub, start, tk):
        q = q_ref[sub * t:(sub + 1) * t, :]
        return _dot_nt(q, k_ref[pl.ds(pl.multiple_of(start, t), tk), :])

    def softmax(sub, s, diagonal):
        tk = s.shape[1]
        if diagonal:
            row = lax.broadcasted_iota(jnp.int32, s.shape, 0)
            col = lax.broadcasted_iota(jnp.int32, s.shape, 1)
            s = jnp.where(col <= row, s, NEG_INF)
        chunks = [s[:, c * LANES:(c + 1) * LANES] for c in range(tk // LANES)]
        m_cur = jnp.max(functools.reduce(jnp.maximum, chunks), axis=1, keepdims=True)
        m_prev = m_ref[sub]
        m_new = jnp.maximum(m_prev, jnp.broadcast_to(m_cur, (t, LANES)) * c2)
        alpha = jnp.exp2(m_prev - m_new)
        p = [jnp.exp2(c * c2 - m_new).astype(BF16) for c in chunks]
        m_ref[sub] = m_new
        return p, alpha

    def accumulate(sub, p, alpha, start):
        ones = jnp.ones((t, LANES), BF16)
        n = t // LANES
        pv = [_dot(jnp.concatenate(p[c * n:(c + 1) * n], axis=1),
                   jnp.concatenate([v_ref[pl.ds(pl.multiple_of(start + c * t, t), t), :], ones],
                                   axis=1)) for c in range(len(p) // n)]
        acc_ref[sub] = jnp.concatenate([alpha, alpha], axis=1) * acc_ref[sub] + sum(pv)

    def update(sub, s, start, diagonal):
        accumulate(sub, *softmax(sub, s, diagonal), start)

    def body(j, carry):
        start = j * 2 * t
        ss = [logits(sub, start, 2 * t) for sub in range(nsub)]
        for sub in range(nsub):
            update(sub, ss[sub], start, False)
        return carry

    lax.fori_loop(0, i * (nsub // 2), body, 0)
    for w_ref, wo_ref in zip(side_in, side_out):
        wo_ref[...] = w_ref[...].astype(BF16)
    first = i * nsub
    for kb in range(nsub):
        for sub in range(kb, nsub):
            start = (first + kb) * t
            update(sub, logits(sub, start, t), start, sub == kb)
    for sub in range(nsub):
        acc = acc_ref[sub]
        o_ref[sub * t:(sub + 1) * t, :] = acc[:, :V_DIM] / acc[:, V_DIM:]


def _side_block_rows(rows, steps):
    return next(rb for rb in range(SUBLANES_BF16, rows + 1, SUBLANES_BF16)
                if rows % rb == 0 and rows // rb <= steps)


def _attention(q, k, v, side_weights=(), *, t=512, nsub=4):
    B, S, _ = q.shape
    tq = nsub * t
    assert S % tq == 0 and t % MOBA_BLOCK == 0 and nsub % 2 == 0
    nq = S // tq
    steps = B * HEADS * nq
    kern = functools.partial(_attn_kernel, t=t, nsub=nsub, nside=len(side_weights))
    side_specs, side_bytes = [], 0
    for w in side_weights:
        rb = _side_block_rows(w.shape[0], steps)
        last = w.shape[0] // rb - 1
        side_specs.append(pl.BlockSpec(
            (rb, w.shape[1]),
            lambda b, h, i, last=last: (jnp.minimum((b * HEADS + h) * nq + i, last), 0)))
        side_bytes += 2 * rb * w.shape[1] * (4 + 2)
    vmem = (2 * S * (QK_PAD + V_DIM) * 2 + 2 * tq * QK_PAD * 2 + 2 * tq * V_DIM * 4
            + tq * (LANES + 2 * V_DIM) * 4 + nsub * 6 * t * t * 4 + side_bytes + SPILL_BYTES)
    out = pl.pallas_call(
        kern,
        grid=(B, HEADS, nq),
        in_specs=[
            pl.BlockSpec((None, tq, QK_PAD), lambda b, h, i: (b, i, h)),
            pl.BlockSpec((None, S, QK_PAD), lambda b, h, i: (b, 0, h)),
            pl.BlockSpec((None, S, V_DIM), lambda b, h, i: (b, 0, h)),
        ] + side_specs,
        out_specs=[pl.BlockSpec((None, tq, V_DIM), lambda b, h, i: (b, i, h))] + side_specs,
        out_shape=[jax.ShapeDtypeStruct((B, S, HEADS * V_DIM), F32)]
        + [jax.ShapeDtypeStruct(w.shape, BF16) for w in side_weights],
        scratch_shapes=[pltpu.VMEM((nsub, t, LANES), F32),
                        pltpu.VMEM((nsub, t, 2 * V_DIM), F32)],
        compiler_params=pltpu.CompilerParams(
            dimension_semantics=("arbitrary", "arbitrary", "arbitrary"), vmem_limit_bytes=vmem),
        name="attention",
    )(q, k, v, *side_weights)
    return out[0], tuple(out[1:])


def _out_kernel(x_ref, am_ref, ga_ref, gm_ref, w_ref, gp_ref, o_ref):
    an = _rms(am_ref[:, :MLA_WIDTH], ga_ref[...]).astype(BF16)
    mn = _rms(am_ref[:, MLA_WIDTH:], gm_ref[...]).astype(BF16)
    y = _dot(jnp.concatenate([an, mn], axis=1), w_ref[...])
    o_ref[...] = x_ref[...] + _rms(y, gp_ref[...])


def _out_proj(x, am, ga, gm, w, gp, *, tm=512):
    T, D = x.shape
    W = MLA_WIDTH + MOBA_WIDTH
    tok = lambda wd: pl.BlockSpec((tm, wd), lambda i: (i, 0))
    vmem = 2 * 2 * tm * D * 4 + 2 * tm * W * 4 + W * D * 2 + 4 * tm * D * 4 + SPILL_BYTES
    return pl.pallas_call(
        _out_kernel,
        grid=(T // tm,),
        in_specs=[tok(D), tok(W), _const_spec((1, MLA_WIDTH)), _const_spec((1, MOBA_WIDTH)),
                  _const_spec((W, D)), _const_spec((1, D))],
        out_specs=tok(D),
        out_shape=jax.ShapeDtypeStruct((T, D), F32),
        compiler_params=pltpu.CompilerParams(
            dimension_semantics=("parallel",), vmem_limit_bytes=vmem),
        name="out_proj",
    )(x, am, ga, gm, w, gp)


def _spread_rope_cols(w):
    half = MLA_ROPE_DIM // 2
    z = jnp.zeros(w.shape[:-1] + (LANES // 2 - half,), w.dtype)
    return jnp.concatenate([w[..., :half], z, w[..., half:], z], axis=-1)


def _rope_tables(seq):
    step = 64
    assert seq % step == 0
    lo = jnp.arange(step, dtype=F32)[:, None]
    hi = step * jnp.arange(seq // step, dtype=F32)[:, None]

    def inv_freq(dim):
        return 1.0 / (ROPE_THETA ** (jnp.arange(0, dim, 2, dtype=F32) / dim))

    def tables(inv, live):
        inv = jnp.concatenate([inv, inv])[None, :]
        sign = jnp.concatenate([-live, live])[None, :]
        ch, sh = jnp.cos(hi * inv)[:, None, :], jnp.sin(hi * inv)[:, None, :]
        cl, sl = jnp.cos(lo * inv)[None, :, :], jnp.sin(lo * inv)[None, :, :]
        cos = (ch * cl - sh * sl).reshape(seq, LANES)
        sin = (sh * cl + ch * sl).reshape(seq, LANES)
        return cos * jnp.abs(sign), sin * sign

    cm, sm = tables(inv_freq(MOBA_HEAD_DIM), jnp.ones((LANES // 2,), F32))
    pad = jnp.zeros((LANES // 2 - MLA_ROPE_DIM // 2,), F32)
    ca, sa = tables(jnp.concatenate([inv_freq(MLA_ROPE_DIM), pad]),
                    jnp.concatenate([jnp.ones((MLA_ROPE_DIM // 2,), F32), pad]))
    return ca, sa, cm, sm


def kernel(x, ffn1_pre_g, ffn1_w_gate, ffn1_w_up, ffn1_w_down, ffn1_post_g, mix_pre_g, w_in, mla_q_norm_g, mla_kv_norm_g, mla_w_uq, mla_w_ukv, mla_out_g, moba_out_g, w_out, mix_post_g, ffn2_pre_g, ffn2_w_gate, ffn2_w_up, ffn2_w_down, ffn2_post_g):
    B, S, D = x.shape
    depth = w_in.shape[0]
    H = MLA_HEADS
    ca, sa, cm, sm = _rope_tables(S)
    xt = x.reshape(B * S, D)
    for l in range(depth):
        o_kpe = MLA_Q_RANK + MLA_KV_RANK
        w_in_t = w_in[l].T.astype(BF16)
        w_kpe = _spread_rope_cols(w_in_t[o_kpe:o_kpe + MLA_ROPE_DIM].T).T
        uq = mla_w_uq[l].reshape(MLA_Q_RANK, H, MLA_QK_DIM)
        w_uq = jnp.concatenate(
            [uq[..., :MLA_NOPE_DIM], _spread_rope_cols(uq[..., MLA_NOPE_DIM:])], axis=-1
        ).reshape(MLA_Q_RANK, H * QK_PAD).astype(BF16)
        ukv = mla_w_ukv[l].reshape(MLA_KV_RANK, H, MLA_NOPE_DIM + MLA_V_DIM)
        w_uk = ukv[..., :MLA_NOPE_DIM].reshape(MLA_KV_RANK, MLA_WIDTH).astype(BF16)
        w_uv = ukv[..., MLA_NOPE_DIM:].reshape(MLA_KV_RANK, MLA_WIDTH).astype(BF16)

        g1 = (ffn1_pre_g[l][None], ffn1_post_g[l][None])
        head, wg1, wu1, wd1 = _ffn(xt, g1[0], ffn1_w_gate[l], ffn1_w_up[l], ffn1_w_down[l], g1[1],
                                   tf=256, tiles=1, emit_weights=True)
        xt = _ffn(xt, g1[0], wg1, wu1, wd1, g1[1], tf=512, head=head)
        q, k, v = _proj(xt, mix_pre_g[l][None], w_in_t, w_kpe, mla_q_norm_g[l][None],
                        mla_kv_norm_g[l][None], w_uq, w_uk, w_uv, ca, sa, cm, sm, seq=S)
        am, (wg2, wu2, wd2, wo) = _attention(
            q.reshape(B, S, -1), k.reshape(B, S, -1), v.reshape(B, S, -1),
            side_weights=(ffn2_w_gate[l], ffn2_w_up[l], ffn2_w_down[l], w_out[l]))
        xt = _out_proj(xt, am.reshape(B * S, -1), mla_out_g[l][None], moba_out_g[l][None],
                       wo, mix_post_g[l][None])
        xt = _ffn(xt, ffn2_pre_g[l][None], wg2, wu2, wd2, ffn2_post_g[l][None], tf=512)
    return xt.reshape(B, S, D)
```

```python
import functools

import jax
import jax.numpy as jnp
from jax import lax
from jax.experimental import pallas as pl
from jax.experimental.pallas import tpu as pltpu

MLA_HEADS = 8
MLA_Q_RANK = 512
MLA_KV_RANK = 256
MLA_NOPE_DIM = 128
MLA_ROPE_DIM = 64
MLA_V_DIM = 128
MLA_QK_DIM = MLA_NOPE_DIM + MLA_ROPE_DIM
MOBA_HEADS = 8
MOBA_HEAD_DIM = 128
MOBA_BLOCK = 256
MOBA_TOPK = 3
MLA_WIDTH = MLA_HEADS * MLA_V_DIM
MOBA_WIDTH = MOBA_HEADS * MOBA_HEAD_DIM
ROPE_THETA = 10000.0
NORM_EPS = 1e-6
NEG_INF = -1e30
MASK_BIAS = -(2.0 ** 100)

LANES = 128
SUBLANES_BF16 = 16
HEADS = MLA_HEADS + MOBA_HEADS
QK_PAD = 2 * LANES
V_DIM = 128
V7X_VMEM_BYTES = 64 * 2 ** 20
SPILL_BYTES = 8 * 2 ** 20

BF16 = jnp.bfloat16
F32 = jnp.float32


def _rms(x, g):
    return x * lax.rsqrt(jnp.mean(x * x, axis=-1, keepdims=True) + NORM_EPS) * g


def _dot(a, b):
    return jnp.dot(a, b, preferred_element_type=F32)


def _dot_nt(a, b):
    return lax.dot_general(a, b, (((1,), (1,)), ((), ())), preferred_element_type=F32)


def _const_spec(shape):
    return pl.BlockSpec(shape, lambda *_: (0,) * len(shape), pipeline_mode=pl.Buffered(1))


def _ffn_kernel(*refs, emit_weights, has_head):
    x_ref, pre_g_ref, wg_ref, wu_ref, wd_ref, post_g_ref = refs[:6]
    rest = list(refs[6:])
    head_ref = rest.pop(0) if has_head else None
    o_ref = rest.pop(0)
    w16_refs = [rest.pop(0) for _ in range(3)] if emit_weights else None
    (xn_ref,) = rest

    i = pl.program_id(0)
    f = pl.program_id(1)
    last = pl.num_programs(1) - 1
    tm = x_ref.shape[0]
    halves = [slice(0, tm // 2), slice(tm // 2, tm)]

    def when_computing(cond):
        return pl.when(cond & (i > 0)) if has_head else pl.when(cond)

    def weights():
        w = [r[...].astype(BF16) for r in (wg_ref, wu_ref, wd_ref)]
        if emit_weights:
            for r16, w16 in zip(w16_refs, w):
                r16[...] = w16
        return w

    def swiglu_down(xn, w):
        h = _dot(xn, w[0])
        u = _dot(xn, w[1])
        a = (h * jax.nn.sigmoid(h)) * u
        return _dot(a.astype(BF16), w[2])

    @when_computing(f == 0)
    def _():
        w = weights()
        for rows in halves:
            xn = _rms(x_ref[rows, :], pre_g_ref[...]).astype(BF16)
            xn_ref[rows, :] = xn
            o_ref[rows, :] = swiglu_down(xn, w)

    @when_computing((f > 0) & (f < last))
    def _():
        o_ref[...] += swiglu_down(xn_ref[...], weights())

    @when_computing(f == last)
    def _():
        w = weights()
        for rows in halves:
            acc = o_ref[rows, :] + swiglu_down(xn_ref[rows, :], w)
            o_ref[rows, :] = x_ref[rows, :] + 0.5 * _rms(acc, post_g_ref[...])

    if has_head:
        @pl.when((i == 0) & (f == last))
        def _():
            o_ref[...] = head_ref[...]


def _ffn(x, pre_g, wg, wu, wd, post_g, *, tf, tm=1024, tiles=None, emit_weights=False,
         head=None):
    T, D = x.shape
    F = wg.shape[1]
    tiles = T // tm if tiles is None else tiles
    assert not emit_weights or tiles == 1
    has_head = head is not None
    wbytes = wg.dtype.itemsize
    vmem = (2 * 2 * tm * D * 4
            + tm * D * 2
            + 2 * 3 * D * tf * wbytes
            + 3 * D * tf * 2 * (3 if emit_weights else 1)
            + has_head * tm * D * 4
            + 4 * tm * tf * 4 + tm * D * 4)
    vmem = min(vmem, V7X_VMEM_BYTES)
    row = (lambda i: jnp.maximum(i, 1)) if has_head else (lambda i: i)
    col = (lambda i, f: jnp.where(i == 0, 0, f)) if has_head else (lambda i, f: f)
    w_specs = [pl.BlockSpec((D, tf), lambda i, f: (0, col(i, f))),
               pl.BlockSpec((D, tf), lambda i, f: (0, col(i, f))),
               pl.BlockSpec((tf, D), lambda i, f: (col(i, f), 0))]
    in_specs = [pl.BlockSpec((tm, D), lambda i, f: (row(i), 0)),
                pl.BlockSpec((1, D), lambda i, f: (0, 0))] + w_specs + [
                pl.BlockSpec((1, D), lambda i, f: (0, 0))]
    out_specs = [pl.BlockSpec((tm, D), lambda i, f: (i, 0))]
    out_shape = [jax.ShapeDtypeStruct((tiles * tm, D), F32)]
    args = [x, pre_g, wg, wu, wd, post_g]
    if has_head:
        in_specs.append(_const_spec((tm, D)))
        args.append(head)
    if emit_weights:
        out_specs += w_specs
        out_shape += [jax.ShapeDtypeStruct(w.shape, BF16) for w in (wg, wu, wd)]
    out = pl.pallas_call(
        functools.partial(_ffn_kernel, emit_weights=emit_weights, has_head=has_head),
        grid=(tiles, F // tf),
        in_specs=in_specs,
        out_specs=out_specs,
        out_shape=out_shape,
        scratch_shapes=[pltpu.VMEM((tm, D), BF16)],
        compiler_params=pltpu.CompilerParams(
            dimension_semantics=("arbitrary", "arbitrary"), vmem_limit_bytes=vmem),
        name="ffn",
    )(*args)
    return out if emit_weights else out[0]


def _rope(x, c, s):
    return x * c + pltpu.roll(x, LANES // 2, axis=1) * s


def _moba_bias(kmean, qh, blk):
    nb = kmean.shape[0]
    gate = _dot_nt(kmean.astype(BF16), qh)
    n = lax.broadcasted_iota(jnp.int32, gate.shape, 0)
    cand = jnp.where(n < blk, gate, -jnp.inf)
    allowed = n == blk
    for _ in range(min(MOBA_TOPK, nb - 1)):
        best = jnp.max(cand, axis=0, keepdims=True)
        first = jnp.min(jnp.where(cand == best, n, nb), axis=0, keepdims=True)
        pick = (n == first) & (best > -jnp.inf)
        allowed = allowed | pick
        cand = jnp.where(pick, -jnp.inf, cand)
    bias = jnp.where(allowed, 0.0, MASK_BIAS)
    bias = jnp.concatenate([bias, jnp.zeros((LANES - nb, bias.shape[1]), F32)], axis=0)
    return bias.T


def _proj_kernel(x_ref, g_ref, w_lat_ref, w_kpe_ref, w_moba_ref, gq_ref, gkv_ref,
                 w_uq_ref, w_uk_ref, w_uv_ref, ca_ref, sa_ref, cm_ref, sm_ref,
                 q_ref, k_ref, v_ref, un_ref, kmean_ref):
    nb = kmean_ref.shape[0]
    L = MOBA_BLOCK
    group = pl.program_id(1)

    @pl.when(group == 0)
    def _():
        un = _rms(x_ref[...], g_ref[...]).astype(BF16)
        un_ref[...] = un
        ca, sa = ca_ref[...], sa_ref[...]

        c_q = _dot_nt(un, w_lat_ref[:MLA_Q_RANK, :])
        c_kv = _dot_nt(un, w_lat_ref[MLA_Q_RANK:, :])
        k_pe = _dot_nt(un, w_kpe_ref[...])

        qn = _rms(c_q, gq_ref[...]).astype(BF16)
        kvn = _rms(c_kv, gkv_ref[...]).astype(BF16)
        q = _dot(qn, w_uq_ref[...])
        k_nope = _dot(kvn, w_uk_ref[...])
        v_ref[...] = _dot(kvn, w_uv_ref[...]).astype(BF16)
        for h in range(MLA_HEADS):
            lo = h * QK_PAD
            q_ref[:, lo:lo + LANES] = q[:, lo:lo + LANES].astype(BF16)
            q_ref[:, lo + LANES:lo + QK_PAD] = _rope(
                q[:, lo + LANES:lo + QK_PAD], ca, sa).astype(BF16)

        k_pe = _rope(k_pe, ca, sa).astype(BF16)
        for h in range(MLA_HEADS):
            lo = h * QK_PAD
            k_ref[:, lo:lo + LANES] = k_nope[:, h * LANES:(h + 1) * LANES].astype(BF16)
            k_ref[:, lo + LANES:lo + QK_PAD] = k_pe

    @pl.when(group == 1)
    def _():
        @pl.when(pl.program_id(0) == 0)
        def _():
            kmean_ref[...] = jnp.zeros_like(kmean_ref)

        un = un_ref[...]
        cm, sm = cm_ref[...], sm_ref[...]
        tm = un.shape[0]
        blk0 = (pl.program_id(0) * (tm // L)) % nb
        km = _dot_nt(un, w_moba_ref[MOBA_WIDTH:2 * MOBA_WIDTH, :])
        qm = _dot_nt(un, w_moba_ref[:MOBA_WIDTH, :])
        lane = lax.broadcasted_iota(jnp.int32, (L, LANES), 1)
        blk_row = lax.broadcasted_iota(jnp.int32, (nb, LANES), 0)
        for h in range(MOBA_HEADS):
            sl = slice(h * LANES, (h + 1) * LANES)
            lo = h * QK_PAD
            qh = _rope(qm[:, sl], cm, sm).astype(BF16)
            kh = _rope(km[:, sl], cm, sm)
            q_ref[:, lo:lo + LANES] = qh
            k_ref[:, lo:lo + LANES] = kh.astype(BF16)
            kmean = kmean_ref[:, sl]
            for part in range(tm // L):
                rows = slice(part * L, (part + 1) * L)
                blk = blk0 + part
                kmean = jnp.where(blk_row == blk, jnp.mean(kh[rows], axis=0, keepdims=True), kmean)
                q_ref[rows, lo + LANES:lo + QK_PAD] = _moba_bias(kmean, qh[rows], blk).astype(BF16)
                k_ref[rows, lo + LANES:lo + QK_PAD] = jnp.where(lane == blk, 1.0, 0.0).astype(BF16)
            kmean_ref[:, sl] = kmean
        v_ref[...] = _dot_nt(un, w_moba_ref[2 * MOBA_WIDTH:, :]).astype(BF16)


def _proj(x, g, w_lat, w_kpe, w_moba, gq, gkv, w_uq, w_uk, w_uv, ca, sa, cm, sm, *, seq, tm=512):
    T, D = x.shape
    n_in = w_lat.shape[0] + w_kpe.shape[0] + w_moba.shape[0]
    nb = seq // MOBA_BLOCK
    assert seq % tm == 0 and tm % MOBA_BLOCK == 0 and nb % SUBLANES_BF16 == 0 and nb <= LANES
    assert MLA_HEADS == MOBA_HEADS and MLA_WIDTH == MOBA_WIDTH
    wq, wv = MLA_HEADS * QK_PAD, MLA_WIDTH
    out = lambda w: pl.BlockSpec((tm, w), lambda i, grp: (i, grp))
    tab = pl.BlockSpec((tm, LANES), lambda i, grp: (i % (seq // tm), 0))
    vmem = (2 * tm * D * 4 + tm * D * 2
            + (D * n_in + MLA_Q_RANK * wq + 2 * MLA_KV_RANK * MLA_WIDTH) * 2
            + 2 * tm * (2 * wq + wv) * 2
            + 6 * tm * wq * 4
            + SPILL_BYTES)
    return pl.pallas_call(
        _proj_kernel,
        grid=(T // tm, 2),
        in_specs=[
            pl.BlockSpec((tm, D), lambda i, grp: (i, 0)), _const_spec((1, D)),
            _const_spec(w_lat.shape), _const_spec(w_kpe.shape), _const_spec(w_moba.shape),
            _const_spec((1, MLA_Q_RANK)), _const_spec((1, MLA_KV_RANK)),
            _const_spec((MLA_Q_RANK, wq)), _const_spec((MLA_KV_RANK, MLA_WIDTH)),
            _const_spec((MLA_KV_RANK, MLA_WIDTH)),
            tab, tab, tab, tab,
        ],
        out_specs=[out(wq), out(wq), out(wv)],
        out_shape=[
            jax.ShapeDtypeStruct((T, 2 * wq), BF16),
            jax.ShapeDtypeStruct((T, 2 * wq), BF16),
            jax.ShapeDtypeStruct((T, 2 * wv), BF16),
        ],
        scratch_shapes=[pltpu.VMEM((tm, D), BF16), pltpu.VMEM((nb, MOBA_WIDTH), F32)],
        compiler_params=pltpu.CompilerParams(
            dimension_semantics=("arbitrary", "arbitrary"), vmem_limit_bytes=vmem),
        name="proj",
    )(x, g, w_lat, w_kpe, w_moba, gq, gkv, w_uq, w_uk, w_uv, ca, sa, cm, sm)


LOG2_E = 1.4426950408889634


def _attn_kernel(*refs, t, nsub, hp, nside):
    q_ref, k_ref, v_ref = refs[:3]
    side_in, (o_ref, *side_out) = refs[3:3 + nside], refs[3 + nside:4 + 2 * nside]
    m_ref, acc_ref = refs[4 + 2 * nside:]
    h = pl.program_id(1)
    i = pl.program_id(2)
    scale = jnp.where(h < MLA_HEADS // hp, MLA_QK_DIM ** -0.5, MOBA_HEAD_DIM ** -0.5)
    c2 = (scale * LOG2_E).astype(F32)
    m_ref[...] = jnp.full(m_ref.shape, -jnp.inf, F32)
    acc_ref[...] = jnp.zeros_like(acc_ref)
    chains = [(g, sub) for sub in range(nsub) for g in range(hp)]

    def logits(g, sub, start, tk):
        cols = slice(g * QK_PAD, (g + 1) * QK_PAD)
        q = q_ref[sub * t:(sub + 1) * t, cols]
        return _dot_nt(q, k_ref[pl.ds(pl.multiple_of(start, t), tk), cols])

    def softmax(sub, s, diagonal):
        tk = s.shape[1]
        if diagonal:
            row = lax.broadcasted_iota(jnp.int32, s.shape, 0)
            col = lax.broadcasted_iota(jnp.int32, s.shape, 1)
            s = jnp.where(col <= row, s, NEG_INF)
        chunks = [s[:, c * LANES:(c + 1) * LANES] for c in range(tk // LANES)]
        m_cur = jnp.max(functools.reduce(jnp.maximum, chunks), axis=1, keepdims=True)
        m_prev = m_ref[sub]
        m_new = jnp.maximum(m_prev, jnp.broadcast_to(m_cur, (t, LANES)) * c2)
        alpha = jnp.exp2(m_prev - m_new)
        p = [jnp.exp2(c * c2 - m_new).astype(BF16) for c in chunks]
        m_ref[sub] = m_new
        return p, alpha

    def accumulate(sub, g, p, alpha, start):
        ones = jnp.ones((t, LANES), BF16)
        n = t // LANES
        cols = slice(g * V_DIM, (g + 1) * V_DIM)
        pv = [_dot(jnp.concatenate(p[c * n:(c + 1) * n], axis=1),
                   jnp.concatenate(
                       [v_ref[pl.ds(pl.multiple_of(start + c * t, t), t), cols], ones], axis=1))
              for c in range(len(p) // n)]
        acc_ref[sub] = jnp.concatenate([alpha, alpha], axis=1) * acc_ref[sub] + sum(pv)

    def update(g, sub, s, start, diagonal):
        chain = g * nsub + sub
        accumulate(chain, g, *softmax(chain, s, diagonal), start)

    def body(j, carry):
        start = j * 2 * t
        ss = [logits(g, sub, start, 2 * t) for g, sub in chains]
        for (g, sub), s in zip(chains, ss):
            update(g, sub, s, start, False)
        return carry

    lax.fori_loop(0, i * (nsub // 2), body, 0)
    for w_ref, wo_ref in zip(side_in, side_out):
        wo_ref[...] = w_ref[...].astype(BF16)
    first = i * nsub
    for kb in range(nsub):
        for sub in range(kb, nsub):
            for g in range(hp):
                start = (first + kb) * t
                update(g, sub, logits(g, sub, start, t), start, sub == kb)
    for g, sub in chains:
        acc = acc_ref[g * nsub + sub]
        o_ref[sub * t:(sub + 1) * t, g * V_DIM:(g + 1) * V_DIM] = acc[:, :V_DIM] / acc[:, V_DIM:]


def _side_block_rows(rows, steps):
    return next(rb for rb in range(SUBLANES_BF16, rows + 1, SUBLANES_BF16)
                if rows % rb == 0 and rows // rb <= steps)


def _attention(q, k, v, side_weights=(), *, t=512, nsub=4, hp=2):
    B, S, _ = q.shape
    tq = nsub * t
    assert S % tq == 0 and t % MOBA_BLOCK == 0 and nsub % 2 == 0
    assert MLA_HEADS % hp == 0 and MOBA_HEADS % hp == 0
    nq, ng = S // tq, HEADS // hp
    steps = B * ng * nq
    kern = functools.partial(_attn_kernel, t=t, nsub=nsub, hp=hp, nside=len(side_weights))
    side_specs, side_bytes = [], 0
    for w in side_weights:
        rb = _side_block_rows(w.shape[0], steps)
        last = w.shape[0] // rb - 1
        side_specs.append(pl.BlockSpec(
            (rb, w.shape[1]),
            lambda b, h, i, last=last: (jnp.minimum((b * ng + h) * nq + i, last), 0)))
        side_bytes += 2 * rb * w.shape[1] * (4 + 2)
    vmem = (hp * (2 * S * (QK_PAD + V_DIM) * 2 + 2 * tq * QK_PAD * 2 + 2 * tq * V_DIM * 4
                  + tq * (LANES + 2 * V_DIM) * 4 + nsub * 6 * t * t * 4)
            + side_bytes + SPILL_BYTES)
    vmem = min(vmem, V7X_VMEM_BYTES)
    out = pl.pallas_call(
        kern,
        grid=(B, ng, nq),
        in_specs=[
            pl.BlockSpec((None, tq, hp * QK_PAD), lambda b, h, i: (b, i, h)),
            pl.BlockSpec((None, S, hp * QK_PAD), lambda b, h, i: (b, 0, h)),
            pl.BlockSpec((None, S, hp * V_DIM), lambda b, h, i: (b, 0, h)),
        ] + side_specs,
        out_specs=[pl.BlockSpec((None, tq, hp * V_DIM), lambda b, h, i: (b, i, h))] + side_specs,
        out_shape=[jax.ShapeDtypeStruct((B, S, HEADS * V_DIM), F32)]
        + [jax.ShapeDtypeStruct(w.shape, BF16) for w in side_weights],
        scratch_shapes=[pltpu.VMEM((hp * nsub, t, LANES), F32),
                        pltpu.VMEM((hp * nsub, t, 2 * V_DIM), F32)],
        compiler_params=pltpu.CompilerParams(
            dimension_semantics=("arbitrary", "arbitrary", "arbitrary"), vmem_limit_bytes=vmem),
        name="attention",
    )(q, k, v, *side_weights)
    return out[0], tuple(out[1:])


def _out_kernel(x_ref, am_ref, ga_ref, gm_ref, w_ref, gp_ref, o_ref):
    an = _rms(am_ref[:, :MLA_WIDTH], ga_ref[...]).astype(BF16)
    mn = _rms(am_ref[:, MLA_WIDTH:], gm_ref[...]).astype(BF16)
    y = _dot(jnp.concatenate([an, mn], axis=1), w_ref[...])
    o_ref[...] = x_ref[...] + _rms(y, gp_ref[...])


def _out_proj(x, am, ga, gm, w, gp, *, tm=512):
    T, D = x.shape
    W = MLA_WIDTH + MOBA_WIDTH
    tok = lambda wd: pl.BlockSpec((tm, wd), lambda i: (i, 0))
    vmem = 2 * 2 * tm * D * 4 + 2 * tm * W * 4 + W * D * 2 + 4 * tm * D * 4 + SPILL_BYTES
    return pl.pallas_call(
        _out_kernel,
        grid=(T // tm,),
        in_specs=[tok(D), tok(W), _const_spec((1, MLA_WIDTH)), _const_spec((1, MOBA_WIDTH)),
                  _const_spec((W, D)), _const_spec((1, D))],
        out_specs=tok(D),
        out_shape=jax.ShapeDtypeStruct((T, D), F32),
        compiler_params=pltpu.CompilerParams(
            dimension_semantics=("parallel",), vmem_limit_bytes=vmem),
        name="out_proj",
    )(x, am, ga, gm, w, gp)


def _spread_rope_cols(w):
    half = MLA_ROPE_DIM // 2
    z = jnp.zeros(w.shape[:-1] + (LANES // 2 - half,), w.dtype)
    return jnp.concatenate([w[..., :half], z, w[..., half:], z], axis=-1)


def _rope_tables(seq):
    step = 64
    assert seq % step == 0
    lo = jnp.arange(step, dtype=F32)[:, None]
    hi = step * jnp.arange(seq // step, dtype=F32)[:, None]

    def inv_freq(dim):
        return 1.0 / (ROPE_THETA ** (jnp.arange(0, dim, 2, dtype=F32) / dim))

    def tables(inv, live):
        inv = jnp.concatenate([inv, inv])[None, :]
        sign = jnp.concatenate([-live, live])[None, :]
        ch, sh = jnp.cos(hi * inv)[:, None, :], jnp.sin(hi * inv)[:, None, :]
        cl, sl = jnp.cos(lo * inv)[None, :, :], jnp.sin(lo * inv)[None, :, :]
        cos = (ch * cl - sh * sl).reshape(seq, LANES)
        sin = (sh * cl + ch * sl).reshape(seq, LANES)
        return cos * jnp.abs(sign), sin * sign

    cm, sm = tables(inv_freq(MOBA_HEAD_DIM), jnp.ones((LANES // 2,), F32))
    pad = jnp.zeros((LANES // 2 - MLA_ROPE_DIM // 2,), F32)
    ca, sa = tables(jnp.concatenate([inv_freq(MLA_ROPE_DIM), pad]),
                    jnp.concatenate([jnp.ones((MLA_ROPE_DIM // 2,), F32), pad]))
    return ca, sa, cm, sm


def kernel(x, ffn1_pre_g, ffn1_w_gate, ffn1_w_up, ffn1_w_down, ffn1_post_g, mix_pre_g, w_in, mla_q_norm_g, mla_kv_norm_g, mla_w_uq, mla_w_ukv, mla_out_g, moba_out_g, w_out, mix_post_g, ffn2_pre_g, ffn2_w_gate, ffn2_w_up, ffn2_w_down, ffn2_post_g):
    B, S, D = x.shape
    depth = w_in.shape[0]
    H = MLA_HEADS
    ca, sa, cm, sm = _rope_tables(S)
    xt = x.reshape(B * S, D)
    for l in range(depth):
        o_kpe = MLA_Q_RANK + MLA_KV_RANK
        w_in_t = w_in[l].T
        w_lat = w_in_t[:o_kpe].astype(BF16)
        w_kpe = _spread_rope_cols(w_in_t[o_kpe:o_kpe + MLA_ROPE_DIM].T).T.astype(BF16)
        w_moba = w_in_t[o_kpe + MLA_ROPE_DIM:].astype(BF16)
        uq = mla_w_uq[l].reshape(MLA_Q_RANK, H, MLA_QK_DIM)
        w_uq = jnp.concatenate(
            [uq[..., :MLA_NOPE_DIM], _spread_rope_cols(uq[..., MLA_NOPE_DIM:])], axis=-1
        ).reshape(MLA_Q_RANK, H * QK_PAD).astype(BF16)
        ukv = mla_w_ukv[l].reshape(MLA_KV_RANK, H, MLA_NOPE_DIM + MLA_V_DIM)
        w_uk = ukv[..., :MLA_NOPE_DIM].reshape(MLA_KV_RANK, MLA_WIDTH).astype(BF16)
        w_uv = ukv[..., MLA_NOPE_DIM:].reshape(MLA_KV_RANK, MLA_WIDTH).astype(BF16)

        g1 = (ffn1_pre_g[l][None], ffn1_post_g[l][None])
        head, wg1, wu1, wd1 = _ffn(xt, g1[0], ffn1_w_gate[l], ffn1_w_up[l], ffn1_w_down[l], g1[1],
                                   tf=256, tiles=1, emit_weights=True)
        xt = _ffn(xt, g1[0], wg1, wu1, wd1, g1[1], tf=512, head=head)
        q, k, v = _proj(xt, mix_pre_g[l][None], w_lat, w_kpe, w_moba, mla_q_norm_g[l][None],
                        mla_kv_norm_g[l][None], w_uq, w_uk, w_uv, ca, sa, cm, sm, seq=S)
        am, (wg2, wu2, wd2, wo) = _attention(
            q.reshape(B, S, -1), k.reshape(B, S, -1), v.reshape(B, S, -1),
            side_weights=(ffn2_w_gate[l], ffn2_w_up[l], ffn2_w_down[l], w_out[l]))
        xt = _out_proj(xt, am.reshape(B * S, -1), mla_out_g[l][None], moba_out_g[l][None],
                       wo, mix_post_g[l][None])
        xt = _ffn(xt, ffn2_pre_g[l][None], wg2, wu2, wd2, ffn2_post_g[l][None], tf=512)
    return xt.reshape(B, S, D)
```

```python
import functools

import jax
import jax.numpy as jnp
from jax import lax
from jax.experimental import pallas as pl
from jax.experimental.pallas import tpu as pltpu

MLA_HEADS = 8
MLA_Q_RANK = 512
MLA_KV_RANK = 256
MLA_NOPE_DIM = 128
MLA_ROPE_DIM = 64
MLA_V_DIM = 128
MLA_QK_DIM = MLA_NOPE_DIM + MLA_ROPE_DIM
MOBA_HEADS = 8
MOBA_HEAD_DIM = 128
MOBA_BLOCK = 256
MOBA_TOPK = 3
MLA_WIDTH = MLA_HEADS * MLA_V_DIM
MOBA_WIDTH = MOBA_HEADS * MOBA_HEAD_DIM
ROPE_THETA = 10000.0
NORM_EPS = 1e-6
NEG_INF = -1e30
MASK_BIAS = -(2.0 ** 100)

LANES = 128
SUBLANES_BF16 = 16
HEADS = MLA_HEADS + MOBA_HEADS
QK_PAD = 2 * LANES
V_DIM = 128
V7X_VMEM_BYTES = 64 * 2 ** 20
SPILL_BYTES = 8 * 2 ** 20

BF16 = jnp.bfloat16
F32 = jnp.float32


def _rms(x, g):
    return x * lax.rsqrt(jnp.mean(x * x, axis=-1, keepdims=True) + NORM_EPS) * g


def _dot(a, b):
    return jnp.dot(a, b, preferred_element_type=F32)


def _dot_nt(a, b):
    return lax.dot_general(a, b, (((1,), (1,)), ((), ())), preferred_element_type=F32)


def _const_spec(shape):
    return pl.BlockSpec(shape, lambda *_: (0,) * len(shape), pipeline_mode=pl.Buffered(1))


def _ffn_kernel(*refs, emit_weights, has_head):
    x_ref, pre_g_ref, wg_ref, wu_ref, wd_ref, post_g_ref = refs[:6]
    rest = list(refs[6:])
    head_ref = rest.pop(0) if has_head else None
    o_ref = rest.pop(0)
    w16_refs = [rest.pop(0) for _ in range(3)] if emit_weights else None
    (xn_ref,) = rest

    i = pl.program_id(0)
    f = pl.program_id(1)
    last = pl.num_programs(1) - 1
    tm = x_ref.shape[0]
    halves = [slice(0, tm // 2), slice(tm // 2, tm)]

    def when_computing(cond):
        return pl.when(cond & (i > 0)) if has_head else pl.when(cond)

    def weights():
        w = [r[...].astype(BF16) for r in (wg_ref, wu_ref, wd_ref)]
        if emit_weights:
            for r16, w16 in zip(w16_refs, w):
                r16[...] = w16
        return w

    def swiglu_down(xn, w):
        h = _dot(xn, w[0])
        u = _dot(xn, w[1])
        a = (h * jax.nn.sigmoid(h)) * u
        return _dot(a.astype(BF16), w[2])

    @when_computing(f == 0)
    def _():
        w = weights()
        for rows in halves:
            xn = _rms(x_ref[rows, :], pre_g_ref[...]).astype(BF16)
            xn_ref[rows, :] = xn
            o_ref[rows, :] = swiglu_down(xn, w)

    @when_computing((f > 0) & (f < last))
    def _():
        o_ref[...] += swiglu_down(xn_ref[...], weights())

    @when_computing(f == last)
    def _():
        w = weights()
        for rows in halves:
            acc = o_ref[rows, :] + swiglu_down(xn_ref[rows, :], w)
            o_ref[rows, :] = x_ref[rows, :] + 0.5 * _rms(acc, post_g_ref[...])

    if has_head:
        @pl.when((i == 0) & (f == last))
        def _():
            o_ref[...] = head_ref[...]


def _ffn(x, pre_g, wg, wu, wd, post_g, *, tf, tm=1024, tiles=None, emit_weights=False,
         head=None):
    T, D = x.shape
    F = wg.shape[1]
    tiles = T // tm if tiles is None else tiles
    assert not emit_weights or tiles == 1
    has_head = head is not None
    wbytes = wg.dtype.itemsize
    vmem = (2 * 2 * tm * D * 4
            + tm * D * 2
            + 2 * 3 * D * tf * wbytes
            + 3 * D * tf * 2 * (3 if emit_weights else 1)
            + has_head * tm * D * 4
            + 4 * tm * tf * 4 + tm * D * 4)
    vmem = min(vmem, V7X_VMEM_BYTES)
    row = (lambda i: jnp.maximum(i, 1)) if has_head else (lambda i: i)
    col = (lambda i, f: jnp.where(i == 0, 0, f)) if has_head else (lambda i, f: f)
    w_specs = [pl.BlockSpec((D, tf), lambda i, f: (0, col(i, f))),
               pl.BlockSpec((D, tf), lambda i, f: (0, col(i, f))),
               pl.BlockSpec((tf, D), lambda i, f: (col(i, f), 0))]
    in_specs = [pl.BlockSpec((tm, D), lambda i, f: (row(i), 0)),
                pl.BlockSpec((1, D), lambda i, f: (0, 0))] + w_specs + [
                pl.BlockSpec((1, D), lambda i, f: (0, 0))]
    out_specs = [pl.BlockSpec((tm, D), lambda i, f: (i, 0))]
    out_shape = [jax.ShapeDtypeStruct((tiles * tm, D), F32)]
    args = [x, pre_g, wg, wu, wd, post_g]
    if has_head:
        in_specs.append(_const_spec((tm, D)))
        args.append(head)
    if emit_weights:
        out_specs += w_specs
        out_shape += [jax.ShapeDtypeStruct(w.shape, BF16) for w in (wg, wu, wd)]
    out = pl.pallas_call(
        functools.partial(_ffn_kernel, emit_weights=emit_weights, has_head=has_head),
        grid=(tiles, F // tf),
        in_specs=in_specs,
        out_specs=out_specs,
        out_shape=out_shape,
        scratch_shapes=[pltpu.VMEM((tm, D), BF16)],
        compiler_params=pltpu.CompilerParams(
            dimension_semantics=("arbitrary", "arbitrary"), vmem_limit_bytes=vmem),
        name="ffn",
    )(*args)
    return out if emit_weights else out[0]


def _rope(x, c, s):
    return x * c + pltpu.roll(x, LANES // 2, axis=1) * s


def _moba_bias(kmean, qh, blk):
    nb = kmean.shape[0]
    gate = _dot_nt(kmean.astype(BF16), qh)
    n = lax.broadcasted_iota(jnp.int32, gate.shape, 0)
    cand = jnp.where(n < blk, gate, -jnp.inf)
    allowed = n == blk
    for _ in range(min(MOBA_TOPK, nb - 1)):
        best = jnp.max(cand, axis=0, keepdims=True)
        first = jnp.min(jnp.where(cand == best, n, nb), axis=0, keepdims=True)
        pick = (n == first) & (best > -jnp.inf)
        allowed = allowed | pick
        cand = jnp.where(pick, -jnp.inf, cand)
    bias = jnp.where(allowed, 0.0, MASK_BIAS)
    bias = jnp.concatenate([bias, jnp.zeros((LANES - nb, bias.shape[1]), F32)], axis=0)
    return bias.T


def _proj_kernel(x_ref, g_ref, w_lat_ref, w_kpe_ref, w_moba_ref, gq_ref, gkv_ref,
                 w_uq_ref, w_uk_ref, w_uv_ref, ca_ref, sa_ref, cm_ref, sm_ref,
                 q_ref, k_ref, v_ref, kmean_ref):
    nb = kmean_ref.shape[0]
    L = MOBA_BLOCK

    @pl.when(pl.program_id(0) == 0)
    def _():
        kmean_ref[...] = jnp.zeros_like(kmean_ref)

    un = _rms(x_ref[...], g_ref[...]).astype(BF16)
    ca, sa, cm, sm = ca_ref[...], sa_ref[...], cm_ref[...], sm_ref[...]
    tm = un.shape[0]
    blk0 = (pl.program_id(0) * (tm // L)) % nb

    c_q = _dot_nt(un, w_lat_ref[:MLA_Q_RANK, :])
    c_kv = _dot_nt(un, w_lat_ref[MLA_Q_RANK:, :])
    k_pe = _dot_nt(un, w_kpe_ref[...])
    km = _dot_nt(un, w_moba_ref[MOBA_WIDTH:2 * MOBA_WIDTH, :])
    qm = _dot_nt(un, w_moba_ref[:MOBA_WIDTH, :])

    qn = _rms(c_q, gq_ref[...]).astype(BF16)
    kvn = _rms(c_kv, gkv_ref[...]).astype(BF16)
    q = _dot(qn, w_uq_ref[...])
    k_nope = _dot(kvn, w_uk_ref[...])
    v_ref[:, :MLA_WIDTH] = _dot(kvn, w_uv_ref[...]).astype(BF16)
    k_pe = _rope(k_pe, ca, sa).astype(BF16)
    for h in range(MLA_HEADS):
        lo = h * QK_PAD
        q_ref[:, lo:lo + LANES] = q[:, lo:lo + LANES].astype(BF16)
        q_ref[:, lo + LANES:lo + QK_PAD] = _rope(q[:, lo + LANES:lo + QK_PAD], ca, sa).astype(BF16)
        k_ref[:, lo:lo + LANES] = k_nope[:, h * LANES:(h + 1) * LANES].astype(BF16)
        k_ref[:, lo + LANES:lo + QK_PAD] = k_pe

    lane = lax.broadcasted_iota(jnp.int32, (L, LANES), 1)
    blk_row = lax.broadcasted_iota(jnp.int32, (nb, LANES), 0)
    for h in range(MOBA_HEADS):
        sl = slice(h * LANES, (h + 1) * LANES)
        lo = (MLA_HEADS + h) * QK_PAD
        qh = _rope(qm[:, sl], cm, sm).astype(BF16)
        kh = _rope(km[:, sl], cm, sm)
        q_ref[:, lo:lo + LANES] = qh
        k_ref[:, lo:lo + LANES] = kh.astype(BF16)
        kmean = kmean_ref[:, sl]
        for part in range(tm // L):
            rows = slice(part * L, (part + 1) * L)
            blk = blk0 + part
            kmean = jnp.where(blk_row == blk, jnp.mean(kh[rows], axis=0, keepdims=True), kmean)
            q_ref[rows, lo + LANES:lo + QK_PAD] = _moba_bias(kmean, qh[rows], blk).astype(BF16)
            k_ref[rows, lo + LANES:lo + QK_PAD] = jnp.where(lane == blk, 1.0, 0.0).astype(BF16)
        kmean_ref[:, sl] = kmean
    v_ref[:, MLA_WIDTH:] = _dot_nt(un, w_moba_ref[2 * MOBA_WIDTH:, :]).astype(BF16)


def _proj(x, g, w_lat, w_kpe, w_moba, gq, gkv, w_uq, w_uk, w_uv, ca, sa, cm, sm, *, seq, tm=512):
    T, D = x.shape
    n_in = w_lat.shape[0] + w_kpe.shape[0] + w_moba.shape[0]
    nb = seq // MOBA_BLOCK
    assert seq % tm == 0 and tm % MOBA_BLOCK == 0 and nb % SUBLANES_BF16 == 0 and nb <= LANES
    wq, wv = HEADS * QK_PAD, HEADS * V_DIM
    tok = lambda w: pl.BlockSpec((tm, w), lambda i: (i, 0))
    tab = pl.BlockSpec((tm, LANES), lambda i: (i % (seq // tm), 0))
    vmem = (2 * tm * D * 4
            + (D * n_in + MLA_Q_RANK * MLA_HEADS * QK_PAD + 2 * MLA_KV_RANK * MLA_WIDTH) * 2
            + 2 * tm * (2 * wq + wv) * 2
            + 8 * tm * MLA_HEADS * QK_PAD * 4
            + SPILL_BYTES)
    vmem = min(vmem, V7X_VMEM_BYTES)
    return pl.pallas_call(
        _proj_kernel,
        grid=(T // tm,),
        in_specs=[
            tok(D), _const_spec((1, D)),
            _const_spec(w_lat.shape), _const_spec(w_kpe.shape), _const_spec(w_moba.shape),
            _const_spec((1, MLA_Q_RANK)), _const_spec((1, MLA_KV_RANK)),
            _const_spec((MLA_Q_RANK, MLA_HEADS * QK_PAD)), _const_spec((MLA_KV_RANK, MLA_WIDTH)),
            _const_spec((MLA_KV_RANK, MLA_WIDTH)),
            tab, tab, tab, tab,
        ],
        out_specs=[tok(wq), tok(wq), tok(wv)],
        out_shape=[
            jax.ShapeDtypeStruct((T, wq), BF16),
            jax.ShapeDtypeStruct((T, wq), BF16),
            jax.ShapeDtypeStruct((T, wv), BF16),
        ],
        scratch_shapes=[pltpu.VMEM((nb, MOBA_WIDTH), F32)],
        compiler_params=pltpu.CompilerParams(
            dimension_semantics=("arbitrary",), vmem_limit_bytes=vmem),
        name="proj",
    )(x, g, w_lat, w_kpe, w_moba, gq, gkv, w_uq, w_uk, w_uv, ca, sa, cm, sm)


LOG2_E = 1.4426950408889634


def _attn_kernel(*refs, t, nsub, hp, nside):
    q_ref, k_ref, v_ref = refs[:3]
    side_in, (o_ref, *side_out) = refs[3:3 + nside], refs[3 + nside:4 + 2 * nside]
    m_ref, acc_ref = refs[4 + 2 * nside:]
    h = pl.program_id(1)
    i = pl.program_id(2)
    scale = jnp.where(h < MLA_HEADS // hp, MLA_QK_DIM ** -0.5, MOBA_HEAD_DIM ** -0.5)
    c2 = (scale * LOG2_E).astype(F32)
    m_ref[...] = jnp.full(m_ref.shape, -jnp.inf, F32)
    acc_ref[...] = jnp.zeros_like(acc_ref)
    chains = [(g, sub) for sub in range(nsub) for g in range(hp)]

    def logits(g, sub, start, tk):
        cols = slice(g * QK_PAD, (g + 1) * QK_PAD)
        q = q_ref[sub * t:(sub + 1) * t, cols]
        return _dot_nt(q, k_ref[pl.ds(pl.multiple_of(start, t), tk), cols])

    def softmax(sub, s, diagonal):
        tk = s.shape[1]
        if diagonal:
            row = lax.broadcasted_iota(jnp.int32, s.shape, 0)
            col = lax.broadcasted_iota(jnp.int32, s.shape, 1)
            s = jnp.where(col <= row, s, NEG_INF)
        chunks = [s[:, c * LANES:(c + 1) * LANES] for c in range(tk // LANES)]
        m_cur = jnp.max(functools.reduce(jnp.maximum, chunks), axis=1, keepdims=True)
        m_prev = m_ref[sub]
        m_new = jnp.maximum(m_prev, jnp.broadcast_to(m_cur, (t, LANES)) * c2)
        alpha = jnp.exp2(m_prev - m_new)
        p = [jnp.exp2(c * c2 - m_new).astype(BF16) for c in chunks]
        m_ref[sub] = m_new
        return p, alpha

    def accumulate(sub, g, p, alpha, start):
        ones = jnp.ones((t, LANES), BF16)
        n = t // LANES
        cols = slice(g * V_DIM, (g + 1) * V_DIM)
        pv = [_dot(jnp.concatenate(p[c * n:(c + 1) * n], axis=1),
                   jnp.concatenate(
                       [v_ref[pl.ds(pl.multiple_of(start + c * t, t), t), cols], ones], axis=1))
              for c in range(len(p) // n)]
        acc_ref[sub] = jnp.concatenate([alpha, alpha], axis=1) * acc_ref[sub] + sum(pv)

    def update(g, sub, s, start, diagonal):
        chain = g * nsub + sub
        accumulate(chain, g, *softmax(chain, s, diagonal), start)

    def body(j, carry):
        start = j * 2 * t
        ss = [logits(g, sub, start, 2 * t) for g, sub in chains]
        for (g, sub), s in zip(chains, ss):
            update(g, sub, s, start, False)
        return carry

    lax.fori_loop(0, i * (nsub // 2), body, 0)
    for w_ref, wo_ref in zip(side_in, side_out):
        wo_ref[...] = w_ref[...].astype(BF16)
    first = i * nsub
    for kb in range(nsub):
        for sub in range(kb, nsub):
            for g in range(hp):
                start = (first + kb) * t
                update(g, sub, logits(g, sub, start, t), start, sub == kb)
    for g, sub in chains:
        acc = acc_ref[g * nsub + sub]
        o_ref[sub * t:(sub + 1) * t, g * V_DIM:(g + 1) * V_DIM] = acc[:, :V_DIM] / acc[:, V_DIM:]


def _side_block_rows(rows, steps):
    return next(rb for rb in range(SUBLANES_BF16, rows + 1, SUBLANES_BF16)
                if rows % rb == 0 and rows // rb <= steps)


def _attention(q, k, v, side_weights=(), *, t=512, nsub=4, hp=2):
    B, S, _ = q.shape
    tq = nsub * t
    assert S % tq == 0 and t % MOBA_BLOCK == 0 and nsub % 2 == 0
    assert MLA_HEADS % hp == 0 and MOBA_HEADS % hp == 0
    nq, ng = S // tq, HEADS // hp
    steps = B * ng * nq
    kern = functools.partial(_attn_kernel, t=t, nsub=nsub, hp=hp, nside=len(side_weights))
    side_specs, side_bytes = [], 0
    for w in side_weights:
        rb = _side_block_rows(w.shape[0], steps)
        last = w.shape[0] // rb - 1
        side_specs.append(pl.BlockSpec(
            (rb, w.shape[1]),
            lambda b, h, i, last=last: (jnp.minimum((b * ng + h) * nq + i, last), 0)))
        side_bytes += 2 * rb * w.shape[1] * (4 + 2)
    vmem = (hp * (2 * S * (QK_PAD + V_DIM) * 2 + 2 * tq * QK_PAD * 2 + 2 * tq * V_DIM * 4
                  + tq * (LANES + 2 * V_DIM) * 4 + nsub * 6 * t * t * 4)
            + side_bytes + SPILL_BYTES)
    vmem = min(vmem, V7X_VMEM_BYTES)
    out = pl.pallas_call(
        kern,
        grid=(B, ng, nq),
        in_specs=[
            pl.BlockSpec((None, tq, hp * QK_PAD), lambda b, h, i: (b, i, h)),
            pl.BlockSpec((None, S, hp * QK_PAD), lambda b, h, i: (b, 0, h)),
            pl.BlockSpec((None, S, hp * V_DIM), lambda b, h, i: (b, 0, h)),
        ] + side_specs,
        out_specs=[pl.BlockSpec((None, tq, hp * V_DIM), lambda b, h, i: (b, i, h))] + side_specs,
        out_shape=[jax.ShapeDtypeStruct((B, S, HEADS * V_DIM), F32)]
        + [jax.ShapeDtypeStruct(w.shape, BF16) for w in side_weights],
        scratch_shapes=[pltpu.VMEM((hp * nsub, t, LANES), F32),
                        pltpu.VMEM((hp * nsub, t, 2 * V_DIM), F32)],
        compiler_params=pltpu.CompilerParams(
            dimension_semantics=("arbitrary", "arbitrary", "arbitrary"), vmem_limit_bytes=vmem),
        name="attention",
    )(q, k, v, *side_weights)
    return out[0], tuple(out[1:])


def _out_kernel(x_ref, am_ref, ga_ref, gm_ref, w_ref, gp_ref, o_ref):
    an = _rms(am_ref[:, :MLA_WIDTH], ga_ref[...]).astype(BF16)
    mn = _rms(am_ref[:, MLA_WIDTH:], gm_ref[...]).astype(BF16)
    y = _dot(jnp.concatenate([an, mn], axis=1), w_ref[...])
    o_ref[...] = x_ref[...] + _rms(y, gp_ref[...])


def _out_proj(x, am, ga, gm, w, gp, *, tm=512):
    T, D = x.shape
    W = MLA_WIDTH + MOBA_WIDTH
    tok = lambda wd: pl.BlockSpec((tm, wd), lambda i: (i, 0))
    vmem = 2 * 2 * tm * D * 4 + 2 * tm * W * 4 + W * D * 2 + 4 * tm * D * 4 + SPILL_BYTES
    return pl.pallas_call(
        _out_kernel,
        grid=(T // tm,),
        in_specs=[tok(D), tok(W), _const_spec((1, MLA_WIDTH)), _const_spec((1, MOBA_WIDTH)),
                  _const_spec((W, D)), _const_spec((1, D))],
        out_specs=tok(D),
        out_shape=jax.ShapeDtypeStruct((T, D), F32),
        compiler_params=pltpu.CompilerParams(
            dimension_semantics=("parallel",), vmem_limit_bytes=vmem),
        name="out_proj",
    )(x, am, ga, gm, w, gp)


def _spread_rope_cols(w):
    half = MLA_ROPE_DIM // 2
    z = jnp.zeros(w.shape[:-1] + (LANES // 2 - half,), w.dtype)
    return jnp.concatenate([w[..., :half], z, w[..., half:], z], axis=-1)


def _rope_tables(seq):
    step = 64
    assert seq % step == 0
    lo = jnp.arange(step, dtype=F32)[:, None]
    hi = step * jnp.arange(seq // step, dtype=F32)[:, None]

    def inv_freq(dim):
        return 1.0 / (ROPE_THETA ** (jnp.arange(0, dim, 2, dtype=F32) / dim))

    def tables(inv, live):
        inv = jnp.concatenate([inv, inv])[None, :]
        sign = jnp.concatenate([-live, live])[None, :]
        ch, sh = jnp.cos(hi * inv)[:, None, :], jnp.sin(hi * inv)[:, None, :]
        cl, sl = jnp.cos(lo * inv)[None, :, :], jnp.sin(lo * inv)[None, :, :]
        cos = (ch * cl - sh * sl).reshape(seq, LANES)
        sin = (sh * cl + ch * sl).reshape(seq, LANES)
        return cos * jnp.abs(sign), sin * sign

    cm, sm = tables(inv_freq(MOBA_HEAD_DIM), jnp.ones((LANES // 2,), F32))
    pad = jnp.zeros((LANES // 2 - MLA_ROPE_DIM // 2,), F32)
    ca, sa = tables(jnp.concatenate([inv_freq(MLA_ROPE_DIM), pad]),
                    jnp.concatenate([jnp.ones((MLA_ROPE_DIM // 2,), F32), pad]))
    return ca, sa, cm, sm


def kernel(x, ffn1_pre_g, ffn1_w_gate, ffn1_w_up, ffn1_w_down, ffn1_post_g, mix_pre_g, w_in, mla_q_norm_g, mla_kv_norm_g, mla_w_uq, mla_w_ukv, mla_out_g, moba_out_g, w_out, mix_post_g, ffn2_pre_g, ffn2_w_gate, ffn2_w_up, ffn2_w_down, ffn2_post_g):
    B, S, D = x.shape
    depth = w_in.shape[0]
    H = MLA_HEADS
    ca, sa, cm, sm = _rope_tables(S)
    xt = x.reshape(B * S, D)
    for l in range(depth):
        o_kpe = MLA_Q_RANK + MLA_KV_RANK
        w_in_t = w_in[l].T
        w_lat = w_in_t[:o_kpe].astype(BF16)
        w_kpe = _spread_rope_cols(w_in_t[o_kpe:o_kpe + MLA_ROPE_DIM].T).T.astype(BF16)
        w_moba = w_in_t[o_kpe + MLA_ROPE_DIM:].astype(BF16)
        uq = mla_w_uq[l].reshape(MLA_Q_RANK, H, MLA_QK_DIM)
        w_uq = jnp.concatenate(
            [uq[..., :MLA_NOPE_DIM], _spread_rope_cols(uq[..., MLA_NOPE_DIM:])], axis=-1
        ).reshape(MLA_Q_RANK, H * QK_PAD).astype(BF16)
        ukv = mla_w_ukv[l].reshape(MLA_KV_RANK, H, MLA_NOPE_DIM + MLA_V_DIM)
        w_uk = ukv[..., :MLA_NOPE_DIM].reshape(MLA_KV_RANK, MLA_WIDTH).astype(BF16)
        w_uv = ukv[..., MLA_NOPE_DIM:].reshape(MLA_KV_RANK, MLA_WIDTH).astype(BF16)

        g1 = (ffn1_pre_g[l][None], ffn1_post_g[l][None])
        head, wg1, wu1, wd1 = _ffn(xt, g1[0], ffn1_w_gate[l], ffn1_w_up[l], ffn1_w_down[l], g1[1],
                                   tf=256, tiles=1, emit_weights=True)
        xt = _ffn(xt, g1[0], wg1, wu1, wd1, g1[1], tf=512, head=head)
        q, k, v = _proj(xt, mix_pre_g[l][None], w_lat, w_kpe, w_moba, mla_q_norm_g[l][None],
                        mla_kv_norm_g[l][None], w_uq, w_uk, w_uv, ca, sa, cm, sm, seq=S)
        am, (wg2, wu2, wd2, wo) = _attention(
            q.reshape(B, S, -1), k.reshape(B, S, -1), v.reshape(B, S, -1),
            side_weights=(ffn2_w_gate[l], ffn2_w_up[l], ffn2_w_down[l], w_out[l]))
        xt = _out_proj(xt, am.reshape(B * S, -1), mla_out_g[l][None], moba_out_g[l][None],
                       wo, mix_post_g[l][None])
        xt = _ffn(xt, ffn2_pre_g[l][None], wg2, wu2, wd2, ffn2_post_g[l][None], tf=512)
    return xt.reshape(B, S, D)
```

```python
import functools

import jax
import jax.numpy as jnp
from jax import lax
from jax.experimental import pallas as pl
from jax.experimental.pallas import tpu as pltpu

MLA_HEADS = 8
MLA_Q_RANK = 512
MLA_KV_RANK = 256
MLA_NOPE_DIM = 128
MLA_ROPE_DIM = 64
MLA_V_DIM = 128
MLA_QK_DIM = MLA_NOPE_DIM + MLA_ROPE_DIM
MOBA_HEADS = 8
MOBA_HEAD_DIM = 128
MOBA_BLOCK = 256
MOBA_TOPK = 3
MLA_WIDTH = MLA_HEADS * MLA_V_DIM
MOBA_WIDTH = MOBA_HEADS * MOBA_HEAD_DIM
ROPE_THETA = 10000.0
NORM_EPS = 1e-6
NEG_INF = -1e30
MASK_BIAS = -(2.0 ** 100)

LANES = 128
SUBLANES_BF16 = 16
HEADS = MLA_HEADS + MOBA_HEADS
QK_PAD = 2 * LANES
V_DIM = 128
V7X_VMEM_BYTES = 64 * 2 ** 20
SPILL_BYTES = 8 * 2 ** 20

BF16 = jnp.bfloat16
F32 = jnp.float32


def _rms(x, g):
    return x * lax.rsqrt(jnp.mean(x * x, axis=-1, keepdims=True) + NORM_EPS) * g


def _dot(a, b):
    return jnp.dot(a, b, preferred_element_type=F32)


def _dot_nt(a, b):
    return lax.dot_general(a, b, (((1,), (1,)), ((), ())), preferred_element_type=F32)


def _const_spec(shape):
    return pl.BlockSpec(shape, lambda *_: (0,) * len(shape), pipeline_mode=pl.Buffered(1))


def _ffn_kernel(*refs, emit_weights, has_head):
    x_ref, pre_g_ref, wg_ref, wu_ref, wd_ref, post_g_ref = refs[:6]
    rest = list(refs[6:])
    head_ref = rest.pop(0) if has_head else None
    o_ref = rest.pop(0)
    w16_refs = [rest.pop(0) for _ in range(3)] if emit_weights else None
    (xn_ref,) = rest

    i = pl.program_id(0)
    f = pl.program_id(1)
    last = pl.num_programs(1) - 1
    tm = x_ref.shape[0]
    halves = [slice(0, tm // 2), slice(tm // 2, tm)]

    def when_computing(cond):
        return pl.when(cond & (i > 0)) if has_head else pl.when(cond)

    def load(r):
        if len(r.shape) == 2:
            return r[...]
        return jnp.concatenate([r[j] for j in range(r.shape[0])], axis=1)

    def weights():
        w = [load(r).astype(BF16) for r in (wg_ref, wu_ref, wd_ref)]
        if emit_weights:
            for r16, w16 in zip(w16_refs, w):
                r16[...] = w16
        return w

    def swiglu_down(xn, w):
        h = _dot(xn, w[0])
        u = _dot(xn, w[1])
        a = (h * jax.nn.sigmoid(h)) * u
        return _dot(a.astype(BF16), w[2])

    @when_computing(f == 0)
    def _():
        w = weights()
        for rows in halves:
            xn = _rms(x_ref[rows, :], pre_g_ref[...]).astype(BF16)
            xn_ref[rows, :] = xn
            o_ref[rows, :] = swiglu_down(xn, w)

    @when_computing((f > 0) & (f < last))
    def _():
        o_ref[...] += swiglu_down(xn_ref[...], weights())

    @when_computing(f == last)
    def _():
        w = weights()
        for rows in halves:
            acc = o_ref[rows, :] + swiglu_down(xn_ref[rows, :], w)
            o_ref[rows, :] = x_ref[rows, :] + 0.5 * _rms(acc, post_g_ref[...])

    if has_head:
        @pl.when((i == 0) & (f == last))
        def _():
            o_ref[...] = head_ref[...]


def _ffn(x, pre_g, wg, wu, wd, post_g, *, tf, tm=1024, tiles=None, emit_weights=False,
         head=None):
    T, D = x.shape
    blocked = wg.ndim == 3
    F = wg.shape[0] * wg.shape[2] if blocked else wg.shape[1]
    tiles = T // tm if tiles is None else tiles
    assert not emit_weights or (tiles == 1 and not blocked)
    has_head = head is not None
    wbytes = wg.dtype.itemsize
    vmem = (2 * 2 * tm * D * 4
            + tm * D * 2
            + 2 * 3 * D * tf * wbytes
            + 3 * D * tf * 2 * (3 if emit_weights else 1)
            + has_head * tm * D * 4
            + 4 * tm * tf * 4 + tm * D * 4)
    vmem = min(vmem, V7X_VMEM_BYTES)
    row = (lambda i: jnp.maximum(i, 1)) if has_head else (lambda i: i)
    col = (lambda i, f: jnp.where(i == 0, 0, f)) if has_head else (lambda i, f: f)
    if blocked:
        gate_up = pl.BlockSpec((tf // wg.shape[2], D, wg.shape[2]), lambda i, f: (col(i, f), 0, 0))
    else:
        gate_up = pl.BlockSpec((D, tf), lambda i, f: (0, col(i, f)))
    w_specs = [gate_up, gate_up, pl.BlockSpec((tf, D), lambda i, f: (col(i, f), 0))]
    in_specs = [pl.BlockSpec((tm, D), lambda i, f: (row(i), 0)),
                pl.BlockSpec((1, D), lambda i, f: (0, 0))] + w_specs + [
                pl.BlockSpec((1, D), lambda i, f: (0, 0))]
    out_specs = [pl.BlockSpec((tm, D), lambda i, f: (i, 0))]
    out_shape = [jax.ShapeDtypeStruct((tiles * tm, D), F32)]
    args = [x, pre_g, wg, wu, wd, post_g]
    if has_head:
        in_specs.append(_const_spec((tm, D)))
        args.append(head)
    if emit_weights:
        col_block = pl.BlockSpec((None, D, tf), lambda i, f: (f, 0, 0))
        out_specs += [col_block, col_block, w_specs[2]]
        out_shape += [jax.ShapeDtypeStruct((F // tf, D, tf), BF16)] * 2 + [
            jax.ShapeDtypeStruct(wd.shape, BF16)]
    out = pl.pallas_call(
        functools.partial(_ffn_kernel, emit_weights=emit_weights, has_head=has_head),
        grid=(tiles, F // tf),
        in_specs=in_specs,
        out_specs=out_specs,
        out_shape=out_shape,
        scratch_shapes=[pltpu.VMEM((tm, D), BF16)],
        compiler_params=pltpu.CompilerParams(
            dimension_semantics=("arbitrary", "arbitrary"), vmem_limit_bytes=vmem),
        name="ffn",
    )(*args)
    return out if emit_weights else out[0]


def _rope(x, c, s):
    return x * c + pltpu.roll(x, LANES // 2, axis=1) * s


def _moba_bias(kmean, qh, blk):
    nb = kmean.shape[0]
    gate = _dot_nt(kmean.astype(BF16), qh)
    n = lax.broadcasted_iota(jnp.int32, gate.shape, 0)
    cand = jnp.where(n < blk, gate, -jnp.inf)
    allowed = n == blk
    for _ in range(min(MOBA_TOPK, nb - 1)):
        best = jnp.max(cand, axis=0, keepdims=True)
        first = jnp.min(jnp.where(cand == best, n, nb), axis=0, keepdims=True)
        pick = (n == first) & (best > -jnp.inf)
        allowed = allowed | pick
        cand = jnp.where(pick, -jnp.inf, cand)
    bias = jnp.where(allowed, 0.0, MASK_BIAS)
    bias = jnp.concatenate([bias, jnp.zeros((LANES - nb, bias.shape[1]), F32)], axis=0)
    return bias.T


def _proj_kernel(x_ref, g_ref, w_lat_ref, w_kpe_ref, w_moba_ref, gq_ref, gkv_ref,
                 w_uq_ref, w_uk_ref, w_uv_ref, ca_ref, sa_ref, cm_ref, sm_ref,
                 q_ref, k_ref, v_ref, kmean_ref):
    nb = kmean_ref.shape[0]
    L = MOBA_BLOCK

    @pl.when(pl.program_id(0) == 0)
    def _():
        kmean_ref[...] = jnp.zeros_like(kmean_ref)

    un = _rms(x_ref[...], g_ref[...]).astype(BF16)
    ca, sa, cm, sm = ca_ref[...], sa_ref[...], cm_ref[...], sm_ref[...]
    tm = un.shape[0]
    blk0 = (pl.program_id(0) * (tm // L)) % nb

    c_q = _dot_nt(un, w_lat_ref[:MLA_Q_RANK, :])
    c_kv = _dot_nt(un, w_lat_ref[MLA_Q_RANK:, :])
    k_pe = _dot_nt(un, w_kpe_ref[...])
    km = _dot_nt(un, w_moba_ref[MOBA_WIDTH:2 * MOBA_WIDTH, :])
    qm = _dot_nt(un, w_moba_ref[:MOBA_WIDTH, :])

    qn = _rms(c_q, gq_ref[...]).astype(BF16)
    kvn = _rms(c_kv, gkv_ref[...]).astype(BF16)
    q = _dot(qn, w_uq_ref[...])
    k_nope = _dot(kvn, w_uk_ref[...])
    v_ref[:, :MLA_WIDTH] = _dot(kvn, w_uv_ref[...]).astype(BF16)
    k_pe = _rope(k_pe, ca, sa).astype(BF16)
    for h in range(MLA_HEADS):
        lo = h * QK_PAD
        q_ref[:, lo:lo + LANES] = q[:, lo:lo + LANES].astype(BF16)
        q_ref[:, lo + LANES:lo + QK_PAD] = _rope(q[:, lo + LANES:lo + QK_PAD], ca, sa).astype(BF16)
        k_ref[:, lo:lo + LANES] = k_nope[:, h * LANES:(h + 1) * LANES].astype(BF16)
        k_ref[:, lo + LANES:lo + QK_PAD] = k_pe

    lane = lax.broadcasted_iota(jnp.int32, (L, LANES), 1)
    blk_row = lax.broadcasted_iota(jnp.int32, (nb, LANES), 0)
    for h in range(MOBA_HEADS):
        sl = slice(h * LANES, (h + 1) * LANES)
        lo = (MLA_HEADS + h) * QK_PAD
        qh = _rope(qm[:, sl], cm, sm).astype(BF16)
        kh = _rope(km[:, sl], cm, sm)
        q_ref[:, lo:lo + LANES] = qh
        k_ref[:, lo:lo + LANES] = kh.astype(BF16)
        kmean = kmean_ref[:, sl]
        for part in range(tm // L):
            rows = slice(part * L, (part + 1) * L)
            blk = blk0 + part
            kmean = jnp.where(blk_row == blk, jnp.mean(kh[rows], axis=0, keepdims=True), kmean)
            q_ref[rows, lo + LANES:lo + QK_PAD] = _moba_bias(kmean, qh[rows], blk).astype(BF16)
            k_ref[rows, lo + LANES:lo + QK_PAD] = jnp.where(lane == blk, 1.0, 0.0).astype(BF16)
        kmean_ref[:, sl] = kmean
    v_ref[:, MLA_WIDTH:] = _dot_nt(un, w_moba_ref[2 * MOBA_WIDTH:, :]).astype(BF16)


def _proj(x, g, w_lat, w_kpe, w_moba, gq, gkv, w_uq, w_uk, w_uv, ca, sa, cm, sm, *, seq, tm=512):
    T, D = x.shape
    n_in = w_lat.shape[0] + w_kpe.shape[0] + w_moba.shape[0]
    nb = seq // MOBA_BLOCK
    assert seq % tm == 0 and tm % MOBA_BLOCK == 0 and nb % SUBLANES_BF16 == 0 and nb <= LANES
    wq, wv = HEADS * QK_PAD, HEADS * V_DIM
    tok = lambda w: pl.BlockSpec((tm, w), lambda i: (i, 0))
    tab = pl.BlockSpec((tm, LANES), lambda i: (i % (seq // tm), 0))
    vmem = (2 * tm * D * 4
            + (D * n_in + MLA_Q_RANK * MLA_HEADS * QK_PAD + 2 * MLA_KV_RANK * MLA_WIDTH) * 2
            + 2 * tm * (2 * wq + wv) * 2
            + 8 * tm * MLA_HEADS * QK_PAD * 4
            + SPILL_BYTES)
    vmem = min(vmem, V7X_VMEM_BYTES)
    return pl.pallas_call(
        _proj_kernel,
        grid=(T // tm,),
        in_specs=[
            tok(D), _const_spec((1, D)),
            _const_spec(w_lat.shape), _const_spec(w_kpe.shape), _const_spec(w_moba.shape),
            _const_spec((1, MLA_Q_RANK)), _const_spec((1, MLA_KV_RANK)),
            _const_spec((MLA_Q_RANK, MLA_HEADS * QK_PAD)), _const_spec((MLA_KV_RANK, MLA_WIDTH)),
            _const_spec((MLA_KV_RANK, MLA_WIDTH)),
            tab, tab, tab, tab,
        ],
        out_specs=[tok(wq), tok(wq), tok(wv)],
        out_shape=[
            jax.ShapeDtypeStruct((T, wq), BF16),
            jax.ShapeDtypeStruct((T, wq), BF16),
            jax.ShapeDtypeStruct((T, wv), BF16),
        ],
        scratch_shapes=[pltpu.VMEM((nb, MOBA_WIDTH), F32)],
        compiler_params=pltpu.CompilerParams(
            dimension_semantics=("arbitrary",), vmem_limit_bytes=vmem),
        name="proj",
    )(x, g, w_lat, w_kpe, w_moba, gq, gkv, w_uq, w_uk, w_uv, ca, sa, cm, sm)


LOG2_E = 1.4426950408889634


def _attn_kernel(*refs, t, nsub, hp, nside):
    q_ref, k_ref, v_ref = refs[:3]
    side_in, (o_ref, *side_out) = refs[3:3 + nside], refs[3 + nside:4 + 2 * nside]
    m_ref, acc_ref = refs[4 + 2 * nside:]
    h = pl.program_id(1)
    i = pl.program_id(2)
    scale = jnp.where(h < MLA_HEADS // hp, MLA_QK_DIM ** -0.5, MOBA_HEAD_DIM ** -0.5)
    c2 = (scale * LOG2_E).astype(F32)
    m_ref[...] = jnp.full(m_ref.shape, -jnp.inf, F32)
    acc_ref[...] = jnp.zeros_like(acc_ref)
    chains = [(g, sub) for sub in range(nsub) for g in range(hp)]

    def logits(g, sub, start, tk):
        cols = slice(g * QK_PAD, (g + 1) * QK_PAD)
        q = q_ref[sub * t:(sub + 1) * t, cols]
        return _dot_nt(q, k_ref[pl.ds(pl.multiple_of(start, t), tk), cols])

    def softmax(sub, s, diagonal):
        tk = s.shape[1]
        if diagonal:
            row = lax.broadcasted_iota(jnp.int32, s.shape, 0)
            col = lax.broadcasted_iota(jnp.int32, s.shape, 1)
            s = jnp.where(col <= row, s, NEG_INF)
        chunks = [s[:, c * LANES:(c + 1) * LANES] for c in range(tk // LANES)]
        m_cur = jnp.max(functools.reduce(jnp.maximum, chunks), axis=1, keepdims=True)
        m_prev = m_ref[sub]
        m_new = jnp.maximum(m_prev, jnp.broadcast_to(m_cur, (t, LANES)) * c2)
        alpha = jnp.exp2(m_prev - m_new)
        p = [jnp.exp2(c * c2 - m_new).astype(BF16) for c in chunks]
        m_ref[sub] = m_new
        return p, alpha

    def accumulate(sub, g, p, alpha, start):
        ones = jnp.ones((t, LANES), BF16)
        n = t // LANES
        cols = slice(g * V_DIM, (g + 1) * V_DIM)
        pv = [_dot(jnp.concatenate(p[c * n:(c + 1) * n], axis=1),
                   jnp.concatenate(
                       [v_ref[pl.ds(pl.multiple_of(start + c * t, t), t), cols], ones], axis=1))
              for c in range(len(p) // n)]
        acc_ref[sub] = jnp.concatenate([alpha, alpha], axis=1) * acc_ref[sub] + sum(pv)

    def update(g, sub, s, start, diagonal):
        chain = g * nsub + sub
        accumulate(chain, g, *softmax(chain, s, diagonal), start)

    def body(j, carry):
        start = j * 2 * t
        ss = [logits(g, sub, start, 2 * t) for g, sub in chains]
        for (g, sub), s in zip(chains, ss):
            update(g, sub, s, start, False)
        return carry

    lax.fori_loop(0, i * (nsub // 2), body, 0)
    for w_ref, wo_ref in zip(side_in, side_out):
        wo_ref[...] = w_ref[...].astype(BF16)
    first = i * nsub
    for kb in range(nsub):
        for sub in range(kb, nsub):
            for g in range(hp):
                start = (first + kb) * t
                update(g, sub, logits(g, sub, start, t), start, sub == kb)
    for g, sub in chains:
        acc = acc_ref[g * nsub + sub]
        o_ref[sub * t:(sub + 1) * t, g * V_DIM:(g + 1) * V_DIM] = acc[:, :V_DIM] / acc[:, V_DIM:]


def _side_block_rows(rows, steps):
    return next(rb for rb in range(SUBLANES_BF16, rows + 1, SUBLANES_BF16)
                if rows % rb == 0 and rows // rb <= steps)


def _attention(q, k, v, side_weights=(), *, t=512, nsub=4, hp=2):
    B, S, _ = q.shape
    tq = nsub * t
    assert S % tq == 0 and t % MOBA_BLOCK == 0 and nsub % 2 == 0
    assert MLA_HEADS % hp == 0 and MOBA_HEADS % hp == 0
    nq, ng = S // tq, HEADS // hp
    steps = B * ng * nq
    kern = functools.partial(_attn_kernel, t=t, nsub=nsub, hp=hp, nside=len(side_weights))
    side_specs, side_bytes = [], 0
    for w in side_weights:
        rb = _side_block_rows(w.shape[0], steps)
        last = w.shape[0] // rb - 1
        side_specs.append(pl.BlockSpec(
            (rb, w.shape[1]),
            lambda b, h, i, last=last: (jnp.minimum((b * ng + h) * nq + i, last), 0)))
        side_bytes += 2 * rb * w.shape[1] * (4 + 2)
    vmem = (hp * (2 * S * (QK_PAD + V_DIM) * 2 + 2 * tq * QK_PAD * 2 + 2 * tq * V_DIM * 4
                  + tq * (LANES + 2 * V_DIM) * 4 + nsub * 6 * t * t * 4)
            + side_bytes + SPILL_BYTES)
    vmem = min(vmem, V7X_VMEM_BYTES)
    out = pl.pallas_call(
        kern,
        grid=(B, ng, nq),
        in_specs=[
            pl.BlockSpec((None, tq, hp * QK_PAD), lambda b, h, i: (b, i, h)),
            pl.BlockSpec((None, S, hp * QK_PAD), lambda b, h, i: (b, 0, h)),
            pl.BlockSpec((None, S, hp * V_DIM), lambda b, h, i: (b, 0, h)),
        ] + side_specs,
        out_specs=[pl.BlockSpec((None, tq, hp * V_DIM), lambda b, h, i: (b, i, h))] + side_specs,
        out_shape=[jax.ShapeDtypeStruct((B, S, HEADS * V_DIM), F32)]
        + [jax.ShapeDtypeStruct(w.shape, BF16) for w in side_weights],
        scratch_shapes=[pltpu.VMEM((hp * nsub, t, LANES), F32),
                        pltpu.VMEM((hp * nsub, t, 2 * V_DIM), F32)],
        compiler_params=pltpu.CompilerParams(
            dimension_semantics=("arbitrary", "arbitrary", "arbitrary"), vmem_limit_bytes=vmem),
        name="attention",
    )(q, k, v, *side_weights)
    return out[0], tuple(out[1:])


def _out_kernel(x_ref, am_ref, ga_ref, gm_ref, w_ref, gp_ref, o_ref):
    an = _rms(am_ref[:, :MLA_WIDTH], ga_ref[...]).astype(BF16)
    mn = _rms(am_ref[:, MLA_WIDTH:], gm_ref[...]).astype(BF16)
    y = _dot(jnp.concatenate([an, mn], axis=1), w_ref[...])
    o_ref[...] = x_ref[...] + _rms(y, gp_ref[...])


def _out_proj(x, am, ga, gm, w, gp, *, tm=512):
    T, D = x.shape
    W = MLA_WIDTH + MOBA_WIDTH
    tok = lambda wd: pl.BlockSpec((tm, wd), lambda i: (i, 0))
    vmem = 2 * 2 * tm * D * 4 + 2 * tm * W * 4 + W * D * 2 + 4 * tm * D * 4 + SPILL_BYTES
    return pl.pallas_call(
        _out_kernel,
        grid=(T // tm,),
        in_specs=[tok(D), tok(W), _const_spec((1, MLA_WIDTH)), _const_spec((1, MOBA_WIDTH)),
                  _const_spec((W, D)), _const_spec((1, D))],
        out_specs=tok(D),
        out_shape=jax.ShapeDtypeStruct((T, D), F32),
        compiler_params=pltpu.CompilerParams(
            dimension_semantics=("parallel",), vmem_limit_bytes=vmem),
        name="out_proj",
    )(x, am, ga, gm, w, gp)


def _spread_rope_cols(w):
    half = MLA_ROPE_DIM // 2
    z = jnp.zeros(w.shape[:-1] + (LANES // 2 - half,), w.dtype)
    return jnp.concatenate([w[..., :half], z, w[..., half:], z], axis=-1)


def _rope_tables(seq):
    step = 64
    assert seq % step == 0
    lo = jnp.arange(step, dtype=F32)[:, None]
    hi = step * jnp.arange(seq // step, dtype=F32)[:, None]

    def inv_freq(dim):
        return 1.0 / (ROPE_THETA ** (jnp.arange(0, dim, 2, dtype=F32) / dim))

    def tables(inv, live):
        inv = jnp.concatenate([inv, inv])[None, :]
        sign = jnp.concatenate([-live, live])[None, :]
        ch, sh = jnp.cos(hi * inv)[:, None, :], jnp.sin(hi * inv)[:, None, :]
        cl, sl = jnp.cos(lo * inv)[None, :, :], jnp.sin(lo * inv)[None, :, :]
        cos = (ch * cl - sh * sl).reshape(seq, LANES)
        sin = (sh * cl + ch * sl).reshape(seq, LANES)
        return cos * jnp.abs(sign), sin * sign

    cm, sm = tables(inv_freq(MOBA_HEAD_DIM), jnp.ones((LANES // 2,), F32))
    pad = jnp.zeros((LANES // 2 - MLA_ROPE_DIM // 2,), F32)
    ca, sa = tables(jnp.concatenate([inv_freq(MLA_ROPE_DIM), pad]),
                    jnp.concatenate([jnp.ones((MLA_ROPE_DIM // 2,), F32), pad]))
    return ca, sa, cm, sm


def kernel(x, ffn1_pre_g, ffn1_w_gate, ffn1_w_up, ffn1_w_down, ffn1_post_g, mix_pre_g, w_in, mla_q_norm_g, mla_kv_norm_g, mla_w_uq, mla_w_ukv, mla_out_g, moba_out_g, w_out, mix_post_g, ffn2_pre_g, ffn2_w_gate, ffn2_w_up, ffn2_w_down, ffn2_post_g):
    B, S, D = x.shape
    depth = w_in.shape[0]
    H = MLA_HEADS
    ca, sa, cm, sm = _rope_tables(S)
    xt = x.reshape(B * S, D)
    for l in range(depth):
        o_kpe = MLA_Q_RANK + MLA_KV_RANK
        w_in_t = w_in[l].T
        w_lat = w_in_t[:o_kpe].astype(BF16)
        w_kpe = _spread_rope_cols(w_in_t[o_kpe:o_kpe + MLA_ROPE_DIM].T).T.astype(BF16)
        w_moba = w_in_t[o_kpe + MLA_ROPE_DIM:].astype(BF16)
        uq = mla_w_uq[l].reshape(MLA_Q_RANK, H, MLA_QK_DIM)
        w_uq = jnp.concatenate(
            [uq[..., :MLA_NOPE_DIM], _spread_rope_cols(uq[..., MLA_NOPE_DIM:])], axis=-1
        ).reshape(MLA_Q_RANK, H * QK_PAD).astype(BF16)
        ukv = mla_w_ukv[l].reshape(MLA_KV_RANK, H, MLA_NOPE_DIM + MLA_V_DIM)
        w_uk = ukv[..., :MLA_NOPE_DIM].reshape(MLA_KV_RANK, MLA_WIDTH).astype(BF16)
        w_uv = ukv[..., MLA_NOPE_DIM:].reshape(MLA_KV_RANK, MLA_WIDTH).astype(BF16)

        g1 = (ffn1_pre_g[l][None], ffn1_post_g[l][None])
        head, wg1, wu1, wd1 = _ffn(xt, g1[0], ffn1_w_gate[l], ffn1_w_up[l], ffn1_w_down[l], g1[1],
                                   tf=256, tiles=1, emit_weights=True)
        xt = _ffn(xt, g1[0], wg1, wu1, wd1, g1[1], tf=512, head=head)
        q, k, v = _proj(xt, mix_pre_g[l][None], w_lat, w_kpe, w_moba, mla_q_norm_g[l][None],
                        mla_kv_norm_g[l][None], w_uq, w_uk, w_uv, ca, sa, cm, sm, seq=S)
        am, (wg2, wu2, wd2, wo) = _attention(
            q.reshape(B, S, -1), k.reshape(B, S, -1), v.reshape(B, S, -1),
            side_weights=(ffn2_w_gate[l], ffn2_w_up[l], ffn2_w_down[l], w_out[l]))
        xt = _out_proj(xt, am.reshape(B * S, -1), mla_out_g[l][None], moba_out_g[l][None],
                       wo, mix_post_g[l][None])
        xt = _ffn(xt, ffn2_pre_g[l][None], wg2, wu2, wd2, ffn2_post_g[l][None], tf=512)
    return xt.reshape(B, S, D)
```

```python
import functools

import jax
import jax.numpy as jnp
from jax import lax
from jax.experimental import pallas as pl
from jax.experimental.pallas import tpu as pltpu

MLA_HEADS = 8
MLA_Q_RANK = 512
MLA_KV_RANK = 256
MLA_NOPE_DIM = 128
MLA_ROPE_DIM = 64
MLA_V_DIM = 128
MLA_QK_DIM = MLA_NOPE_DIM + MLA_ROPE_DIM
MOBA_HEADS = 8
MOBA_HEAD_DIM = 128
MOBA_BLOCK = 256
MOBA_TOPK = 3
MLA_WIDTH = MLA_HEADS * MLA_V_DIM
MOBA_WIDTH = MOBA_HEADS * MOBA_HEAD_DIM
ROPE_THETA = 10000.0
NORM_EPS = 1e-6
NEG_INF = -1e30
MASK_BIAS = -(2.0 ** 100)

LANES = 128
SUBLANES_BF16 = 16
HEADS = MLA_HEADS + MOBA_HEADS
QK_PAD = 2 * LANES
V_DIM = 128
V7X_VMEM_BYTES = 64 * 2 ** 20
SPILL_BYTES = 8 * 2 ** 20

BF16 = jnp.bfloat16
F32 = jnp.float32


def _rms(x, g):
    return x * lax.rsqrt(jnp.mean(x * x, axis=-1, keepdims=True) + NORM_EPS) * g


def _dot(a, b):
    return jnp.dot(a, b, preferred_element_type=F32)


def _dot_nt(a, b):
    return lax.dot_general(a, b, (((1,), (1,)), ((), ())), preferred_element_type=F32)


def _const_spec(shape):
    return pl.BlockSpec(shape, lambda *_: (0,) * len(shape), pipeline_mode=pl.Buffered(1))


def _ffn_kernel(*refs, emit_weights, has_head):
    x_ref, pre_g_ref, wg_ref, wu_ref, wd_ref, post_g_ref = refs[:6]
    rest = list(refs[6:])
    head_ref = rest.pop(0) if has_head else None
    o_ref = rest.pop(0)
    w16_refs = [rest.pop(0) for _ in range(3)] if emit_weights else None
    (xn_ref,) = rest

    i = pl.program_id(0)
    f = pl.program_id(1)
    last = pl.num_programs(1) - 1
    tm = x_ref.shape[0]
    halves = [slice(0, tm // 2), slice(tm // 2, tm)]

    def when_computing(cond):
        return pl.when(cond & (i > 0)) if has_head else pl.when(cond)

    def weights():
        w = [r[...].astype(BF16) for r in (wg_ref, wu_ref, wd_ref)]
        if emit_weights:
            for r16, w16 in zip(w16_refs, w):
                r16[...] = w16
        return w

    def swiglu_down(xn, w):
        h = _dot(xn, w[0])
        u = _dot(xn, w[1])
        a = (h * jax.nn.sigmoid(h)) * u
        return _dot(a.astype(BF16), w[2])

    @when_computing(f == 0)
    def _():
        w = weights()
        for rows in halves:
            xn = _rms(x_ref[rows, :], pre_g_ref[...]).astype(BF16)
            xn_ref[rows, :] = xn
            o_ref[rows, :] = swiglu_down(xn, w)

    @when_computing((f > 0) & (f < last))
    def _():
        o_ref[...] += swiglu_down(xn_ref[...], weights())

    @when_computing(f == last)
    def _():
        w = weights()
        for rows in halves:
            acc = o_ref[rows, :] + swiglu_down(xn_ref[rows, :], w)
            o_ref[rows, :] = x_ref[rows, :] + 0.5 * _rms(acc, post_g_ref[...])

    if has_head:
        @pl.when((i == 0) & (f == last))
        def _():
            o_ref[...] = head_ref[...]


def _ffn(x, pre_g, wg, wu, wd, post_g, *, tf, tm=1024, tiles=None, emit_weights=False,
         head=None):
    T, D = x.shape
    F = wg.shape[1]
    tiles = T // tm if tiles is None else tiles
    assert not emit_weights or tiles == 1
    has_head = head is not None
    wbytes = wg.dtype.itemsize
    vmem = (2 * 2 * tm * D * 4
            + tm * D * 2
            + 2 * 3 * D * tf * wbytes
            + 3 * D * tf * 2 * (3 if emit_weights else 1)
            + has_head * tm * D * 4
            + 4 * tm * tf * 4 + tm * D * 4)
    vmem = min(vmem, V7X_VMEM_BYTES)
    row = (lambda i: jnp.maximum(i, 1)) if has_head else (lambda i: i)
    col = (lambda i, f: jnp.where(i == 0, 0, f)) if has_head else (lambda i, f: f)
    w_specs = [pl.BlockSpec((D, tf), lambda i, f: (0, col(i, f))),
               pl.BlockSpec((D, tf), lambda i, f: (0, col(i, f))),
               pl.BlockSpec((tf, D), lambda i, f: (col(i, f), 0))]
    in_specs = [pl.BlockSpec((tm, D), lambda i, f: (row(i), 0)),
                pl.BlockSpec((1, D), lambda i, f: (0, 0))] + w_specs + [
                pl.BlockSpec((1, D), lambda i, f: (0, 0))]
    out_specs = [pl.BlockSpec((tm, D), lambda i, f: (i, 0))]
    out_shape = [jax.ShapeDtypeStruct((tiles * tm, D), F32)]
    args = [x, pre_g, wg, wu, wd, post_g]
    if has_head:
        in_specs.append(_const_spec((tm, D)))
        args.append(head)
    if emit_weights:
        out_specs += w_specs
        out_shape += [jax.ShapeDtypeStruct(w.shape, BF16) for w in (wg, wu, wd)]
    out = pl.pallas_call(
        functools.partial(_ffn_kernel, emit_weights=emit_weights, has_head=has_head),
        grid=(tiles, F // tf),
        in_specs=in_specs,
        out_specs=out_specs,
        out_shape=out_shape,
        scratch_shapes=[pltpu.VMEM((tm, D), BF16)],
        compiler_params=pltpu.CompilerParams(
            dimension_semantics=("arbitrary", "arbitrary"), vmem_limit_bytes=vmem),
        name="ffn",
    )(*args)
    return out if emit_weights else out[0]


def _ffn_piped_kernel(x_ref, pre_g_ref, post_g_ref, wg_hbm, wu_hbm, wd_hbm, o_ref, xn_ref, *, tf):
    D, F = wg_hbm.shape
    xn_ref[...] = _rms(x_ref[...], pre_g_ref[...]).astype(BF16)
    o_ref[...] = jnp.zeros_like(o_ref)

    def step(wg_ref, wu_ref, wd_ref):
        xn = xn_ref[...]
        h = _dot(xn, wg_ref[...])
        u = _dot(xn, wu_ref[...])
        a = (h * jax.nn.sigmoid(h)) * u
        o_ref[...] += _dot(a.astype(BF16), wd_ref[...])

    pltpu.emit_pipeline(
        step, grid=(F // tf,),
        in_specs=[pl.BlockSpec((D, tf), lambda f: (0, f)),
                  pl.BlockSpec((D, tf), lambda f: (0, f)),
                  pl.BlockSpec((tf, D), lambda f: (f, 0))],
    )(wg_hbm, wu_hbm, wd_hbm)
    o_ref[...] = x_ref[...] + 0.5 * _rms(o_ref[...], post_g_ref[...])


def _ffn_piped(x, pre_g, wg, wu, wd, post_g, *, tf=512, tm=1024):
    T, D = x.shape
    assert wg.dtype == BF16
    vmem = min(2 * 2 * tm * D * 4 + tm * D * 2 + 2 * 3 * D * tf * 2 + 4 * tm * tf * 4
               + tm * D * 4 + SPILL_BYTES, V7X_VMEM_BYTES)
    tile = pl.BlockSpec((tm, D), lambda i: (i, 0))
    vec = pl.BlockSpec((1, D), lambda i: (0, 0))
    hbm = pl.BlockSpec(memory_space=pl.ANY)
    return pl.pallas_call(
        functools.partial(_ffn_piped_kernel, tf=tf),
        grid=(T // tm,),
        in_specs=[tile, vec, vec, hbm, hbm, hbm],
        out_specs=tile,
        out_shape=jax.ShapeDtypeStruct((T, D), F32),
        scratch_shapes=[pltpu.VMEM((tm, D), BF16)],
        compiler_params=pltpu.CompilerParams(
            dimension_semantics=("arbitrary",), vmem_limit_bytes=vmem),
        name="ffn_piped",
    )(x, pre_g, post_g, wg, wu, wd)


def _rope(x, c, s):
    return x * c + pltpu.roll(x, LANES // 2, axis=1) * s


def _moba_bias(kmean, qh, blk):
    nb = kmean.shape[0]
    gate = _dot_nt(kmean.astype(BF16), qh)
    n = lax.broadcasted_iota(jnp.int32, gate.shape, 0)
    cand = jnp.where(n < blk, gate, -jnp.inf)
    allowed = n == blk
    for _ in range(min(MOBA_TOPK, nb - 1)):
        best = jnp.max(cand, axis=0, keepdims=True)
        first = jnp.min(jnp.where(cand == best, n, nb), axis=0, keepdims=True)
        pick = (n == first) & (best > -jnp.inf)
        allowed = allowed | pick
        cand = jnp.where(pick, -jnp.inf, cand)
    bias = jnp.where(allowed, 0.0, MASK_BIAS)
    bias = jnp.concatenate([bias, jnp.zeros((LANES - nb, bias.shape[1]), F32)], axis=0)
    return bias.T


def _proj_kernel(x_ref, g_ref, w_lat_ref, w_kpe_ref, w_moba_ref, gq_ref, gkv_ref,
                 w_uq_ref, w_uk_ref, w_uv_ref, ca_ref, sa_ref, cm_ref, sm_ref,
                 q_ref, k_ref, v_ref, kmean_ref):
    nb = kmean_ref.shape[0]
    L = MOBA_BLOCK

    @pl.when(pl.program_id(0) == 0)
    def _():
        kmean_ref[...] = jnp.zeros_like(kmean_ref)

    un = _rms(x_ref[...], g_ref[...]).astype(BF16)
    ca, sa, cm, sm = ca_ref[...], sa_ref[...], cm_ref[...], sm_ref[...]
    tm = un.shape[0]
    blk0 = (pl.program_id(0) * (tm // L)) % nb

    c_q = _dot_nt(un, w_lat_ref[:MLA_Q_RANK, :])
    c_kv = _dot_nt(un, w_lat_ref[MLA_Q_RANK:, :])
    k_pe = _dot_nt(un, w_kpe_ref[...])
    km = _dot_nt(un, w_moba_ref[MOBA_WIDTH:2 * MOBA_WIDTH, :])
    qm = _dot_nt(un, w_moba_ref[:MOBA_WIDTH, :])

    qn = _rms(c_q, gq_ref[...]).astype(BF16)
    kvn = _rms(c_kv, gkv_ref[...]).astype(BF16)
    q = _dot(qn, w_uq_ref[...])
    k_nope = _dot(kvn, w_uk_ref[...])
    v_ref[:, :MLA_WIDTH] = _dot(kvn, w_uv_ref[...]).astype(BF16)
    k_pe = _rope(k_pe, ca, sa).astype(BF16)
    for h in range(MLA_HEADS):
        lo = h * QK_PAD
        q_ref[:, lo:lo + LANES] = q[:, lo:lo + LANES].astype(BF16)
        q_ref[:, lo + LANES:lo + QK_PAD] = _rope(q[:, lo + LANES:lo + QK_PAD], ca, sa).astype(BF16)
        k_ref[:, lo:lo + LANES] = k_nope[:, h * LANES:(h + 1) * LANES].astype(BF16)
        k_ref[:, lo + LANES:lo + QK_PAD] = k_pe

    lane = lax.broadcasted_iota(jnp.int32, (L, LANES), 1)
    blk_row = lax.broadcasted_iota(jnp.int32, (nb, LANES), 0)
    for h in range(MOBA_HEADS):
        sl = slice(h * LANES, (h + 1) * LANES)
        lo = (MLA_HEADS + h) * QK_PAD
        qh = _rope(qm[:, sl], cm, sm).astype(BF16)
        kh = _rope(km[:, sl], cm, sm)
        q_ref[:, lo:lo + LANES] = qh
        k_ref[:, lo:lo + LANES] = kh.astype(BF16)
        kmean = kmean_ref[:, sl]
        for part in range(tm // L):
            rows = slice(part * L, (part + 1) * L)
            blk = blk0 + part
            kmean = jnp.where(blk_row == blk, jnp.mean(kh[rows], axis=0, keepdims=True), kmean)
            q_ref[rows, lo + LANES:lo + QK_PAD] = _moba_bias(kmean, qh[rows], blk).astype(BF16)
            k_ref[rows, lo + LANES:lo + QK_PAD] = jnp.where(lane == blk, 1.0, 0.0).astype(BF16)
        kmean_ref[:, sl] = kmean
    v_ref[:, MLA_WIDTH:] = _dot_nt(un, w_moba_ref[2 * MOBA_WIDTH:, :]).astype(BF16)


def _proj(x, g, w_lat, w_kpe, w_moba, gq, gkv, w_uq, w_uk, w_uv, ca, sa, cm, sm, *, seq, tm=512):
    T, D = x.shape
    n_in = w_lat.shape[0] + w_kpe.shape[0] + w_moba.shape[0]
    nb = seq // MOBA_BLOCK
    assert seq % tm == 0 and tm % MOBA_BLOCK == 0 and nb % SUBLANES_BF16 == 0 and nb <= LANES
    wq, wv = HEADS * QK_PAD, HEADS * V_DIM
    tok = lambda w: pl.BlockSpec((tm, w), lambda i: (i, 0))
    tab = pl.BlockSpec((tm, LANES), lambda i: (i % (seq // tm), 0))
    vmem = (2 * tm * D * 4
            + (D * n_in + MLA_Q_RANK * MLA_HEADS * QK_PAD + 2 * MLA_KV_RANK * MLA_WIDTH) * 2
            + 2 * tm * (2 * wq + wv) * 2
            + 8 * tm * MLA_HEADS * QK_PAD * 4
            + SPILL_BYTES)
    vmem = min(vmem, V7X_VMEM_BYTES)
    return pl.pallas_call(
        _proj_kernel,
        grid=(T // tm,),
        in_specs=[
            tok(D), _const_spec((1, D)),
            _const_spec(w_lat.shape), _const_spec(w_kpe.shape), _const_spec(w_moba.shape),
            _const_spec((1, MLA_Q_RANK)), _const_spec((1, MLA_KV_RANK)),
            _const_spec((MLA_Q_RANK, MLA_HEADS * QK_PAD)), _const_spec((MLA_KV_RANK, MLA_WIDTH)),
            _const_spec((MLA_KV_RANK, MLA_WIDTH)),
            tab, tab, tab, tab,
        ],
        out_specs=[tok(wq), tok(wq), tok(wv)],
        out_shape=[
            jax.ShapeDtypeStruct((T, wq), BF16),
            jax.ShapeDtypeStruct((T, wq), BF16),
            jax.ShapeDtypeStruct((T, wv), BF16),
        ],
        scratch_shapes=[pltpu.VMEM((nb, MOBA_WIDTH), F32)],
        compiler_params=pltpu.CompilerParams(
            dimension_semantics=("arbitrary",), vmem_limit_bytes=vmem),
        name="proj",
    )(x, g, w_lat, w_kpe, w_moba, gq, gkv, w_uq, w_uk, w_uv, ca, sa, cm, sm)


LOG2_E = 1.4426950408889634


def _attn_kernel(*refs, t, nsub, hp, nside):
    q_ref, k_ref, v_ref = refs[:3]
    side_in, (o_ref, *side_out) = refs[3:3 + nside], refs[3 + nside:4 + 2 * nside]
    m_ref, acc_ref = refs[4 + 2 * nside:]
    h = pl.program_id(1)
    i = pl.program_id(2)
    scale = jnp.where(h < MLA_HEADS // hp, MLA_QK_DIM ** -0.5, MOBA_HEAD_DIM ** -0.5)
    c2 = (scale * LOG2_E).astype(F32)
    m_ref[...] = jnp.full(m_ref.shape, -jnp.inf, F32)
    acc_ref[...] = jnp.zeros_like(acc_ref)
    chains = [(g, sub) for sub in range(nsub) for g in range(hp)]

    def logits(g, sub, start, tk):
        cols = slice(g * QK_PAD, (g + 1) * QK_PAD)
        q = q_ref[sub * t:(sub + 1) * t, cols]
        return _dot_nt(q, k_ref[pl.ds(pl.multiple_of(start, t), tk), cols])

    def softmax(sub, s, diagonal):
        tk = s.shape[1]
        if diagonal:
            row = lax.broadcasted_iota(jnp.int32, s.shape, 0)
            col = lax.broadcasted_iota(jnp.int32, s.shape, 1)
            s = jnp.where(col <= row, s, NEG_INF)
        chunks = [s[:, c * LANES:(c + 1) * LANES] for c in range(tk // LANES)]
        m_cur = jnp.max(functools.reduce(jnp.maximum, chunks), axis=1, keepdims=True)
        m_prev = m_ref[sub]
        m_new = jnp.maximum(m_prev, jnp.broadcast_to(m_cur, (t, LANES)) * c2)
        alpha = jnp.exp2(m_prev - m_new)
        p = [jnp.exp2(c * c2 - m_new).astype(BF16) for c in chunks]
        m_ref[sub] = m_new
        return p, alpha

    def accumulate(sub, g, p, alpha, start):
        ones = jnp.ones((t, LANES), BF16)
        n = t // LANES
        cols = slice(g * V_DIM, (g + 1) * V_DIM)
        pv = [_dot(jnp.concatenate(p[c * n:(c + 1) * n], axis=1),
                   jnp.concatenate(
                       [v_ref[pl.ds(pl.multiple_of(start + c * t, t), t), cols], ones], axis=1))
              for c in range(len(p) // n)]
        acc_ref[sub] = jnp.concatenate([alpha, alpha], axis=1) * acc_ref[sub] + sum(pv)

    def update(g, sub, s, start, diagonal):
        chain = g * nsub + sub
        accumulate(chain, g, *softmax(chain, s, diagonal), start)

    def body(j, carry):
        start = j * 2 * t
        ss = [logits(g, sub, start, 2 * t) for g, sub in chains]
        for (g, sub), s in zip(chains, ss):
            update(g, sub, s, start, False)
        return carry

    lax.fori_loop(0, i * (nsub // 2), body, 0)
    for w_ref, wo_ref in zip(side_in, side_out):
        wo_ref[...] = w_ref[...].astype(BF16)
    first = i * nsub
    for kb in range(nsub):
        for g in range(hp):
            for sub in range(kb, nsub):
                start = (first + kb) * t
                update(g, sub, logits(g, sub, start, t), start, sub == kb)
    for g, sub in chains:
        acc = acc_ref[g * nsub + sub]
        o_ref[sub * t:(sub + 1) * t, g * V_DIM:(g + 1) * V_DIM] = acc[:, :V_DIM] / acc[:, V_DIM:]


def _side_block_rows(rows, steps):
    return next(rb for rb in range(SUBLANES_BF16, rows + 1, SUBLANES_BF16)
                if rows % rb == 0 and rows // rb <= steps)


def _attention(q, k, v, side_weights=(), *, t=512, nsub=4, hp=2):
    B, S, _ = q.shape
    tq = nsub * t
    assert S % tq == 0 and t % MOBA_BLOCK == 0 and nsub % 2 == 0
    assert MLA_HEADS % hp == 0 and MOBA_HEADS % hp == 0
    nq, ng = S // tq, HEADS // hp
    steps = B * ng * nq
    kern = functools.partial(_attn_kernel, t=t, nsub=nsub, hp=hp, nside=len(side_weights))
    side_specs, side_bytes = [], 0
    for w in side_weights:
        rb = _side_block_rows(w.shape[0], steps)
        last = w.shape[0] // rb - 1
        side_specs.append(pl.BlockSpec(
            (rb, w.shape[1]),
            lambda b, h, i, last=last: (jnp.minimum((b * ng + h) * nq + i, last), 0)))
        side_bytes += 2 * rb * w.shape[1] * (4 + 2)
    vmem = (hp * (2 * S * (QK_PAD + V_DIM) * 2 + 2 * tq * QK_PAD * 2 + 2 * tq * V_DIM * 4
                  + tq * (LANES + 2 * V_DIM) * 4 + nsub * 6 * t * t * 4)
            + side_bytes + SPILL_BYTES)
    vmem = min(vmem, V7X_VMEM_BYTES)
    out = pl.pallas_call(
        kern,
        grid=(B, ng, nq),
        in_specs=[
            pl.BlockSpec((None, tq, hp * QK_PAD), lambda b, h, i: (b, i, h)),
            pl.BlockSpec((None, S, hp * QK_PAD), lambda b, h, i: (b, 0, h)),
            pl.BlockSpec((None, S, hp * V_DIM), lambda b, h, i: (b, 0, h)),
        ] + side_specs,
        out_specs=[pl.BlockSpec((None, tq, hp * V_DIM), lambda b, h, i: (b, i, h))] + side_specs,
        out_shape=[jax.ShapeDtypeStruct((B, S, HEADS * V_DIM), F32)]
        + [jax.ShapeDtypeStruct(w.shape, BF16) for w in side_weights],
        scratch_shapes=[pltpu.VMEM((hp * nsub, t, LANES), F32),
                        pltpu.VMEM((hp * nsub, t, 2 * V_DIM), F32)],
        compiler_params=pltpu.CompilerParams(
            dimension_semantics=("arbitrary", "arbitrary", "arbitrary"), vmem_limit_bytes=vmem),
        name="attention",
    )(q, k, v, *side_weights)
    return out[0], tuple(out[1:])


def _out_kernel(x_ref, am_ref, ga_ref, gm_ref, w_ref, gp_ref, o_ref):
    an = _rms(am_ref[:, :MLA_WIDTH], ga_ref[...]).astype(BF16)
    mn = _rms(am_ref[:, MLA_WIDTH:], gm_ref[...]).astype(BF16)
    y = _dot(jnp.concatenate([an, mn], axis=1), w_ref[...])
    o_ref[...] = x_ref[...] + _rms(y, gp_ref[...])


def _out_proj(x, am, ga, gm, w, gp, *, tm=512):
    T, D = x.shape
    W = MLA_WIDTH + MOBA_WIDTH
    tok = lambda wd: pl.BlockSpec((tm, wd), lambda i: (i, 0))
    vmem = 2 * 2 * tm * D * 4 + 2 * tm * W * 4 + W * D * 2 + 4 * tm * D * 4 + SPILL_BYTES
    return pl.pallas_call(
        _out_kernel,
        grid=(T // tm,),
        in_specs=[tok(D), tok(W), _const_spec((1, MLA_WIDTH)), _const_spec((1, MOBA_WIDTH)),
                  _const_spec((W, D)), _const_spec((1, D))],
        out_specs=tok(D),
        out_shape=jax.ShapeDtypeStruct((T, D), F32),
        compiler_params=pltpu.CompilerParams(
            dimension_semantics=("parallel",), vmem_limit_bytes=vmem),
        name="out_proj",
    )(x, am, ga, gm, w, gp)


def _spread_rope_cols(w):
    half = MLA_ROPE_DIM // 2
    z = jnp.zeros(w.shape[:-1] + (LANES // 2 - half,), w.dtype)
    return jnp.concatenate([w[..., :half], z, w[..., half:], z], axis=-1)


def _rope_tables(seq):
    step = 64
    assert seq % step == 0
    lo = jnp.arange(step, dtype=F32)[:, None]
    hi = step * jnp.arange(seq // step, dtype=F32)[:, None]

    def inv_freq(dim):
        return 1.0 / (ROPE_THETA ** (jnp.arange(0, dim, 2, dtype=F32) / dim))

    def tables(inv, live):
        inv = jnp.concatenate([inv, inv])[None, :]
        sign = jnp.concatenate([-live, live])[None, :]
        ch, sh = jnp.cos(hi * inv)[:, None, :], jnp.sin(hi * inv)[:, None, :]
        cl, sl = jnp.cos(lo * inv)[None, :, :], jnp.sin(lo * inv)[None, :, :]
        cos = (ch * cl - sh * sl).reshape(seq, LANES)
        sin = (sh * cl + ch * sl).reshape(seq, LANES)
        return cos * jnp.abs(sign), sin * sign

    cm, sm = tables(inv_freq(MOBA_HEAD_DIM), jnp.ones((LANES // 2,), F32))
    pad = jnp.zeros((LANES // 2 - MLA_ROPE_DIM // 2,), F32)
    ca, sa = tables(jnp.concatenate([inv_freq(MLA_ROPE_DIM), pad]),
                    jnp.concatenate([jnp.ones((MLA_ROPE_DIM // 2,), F32), pad]))
    return ca, sa, cm, sm


def kernel(x, ffn1_pre_g, ffn1_w_gate, ffn1_w_up, ffn1_w_down, ffn1_post_g, mix_pre_g, w_in, mla_q_norm_g, mla_kv_norm_g, mla_w_uq, mla_w_ukv, mla_out_g, moba_out_g, w_out, mix_post_g, ffn2_pre_g, ffn2_w_gate, ffn2_w_up, ffn2_w_down, ffn2_post_g):
    B, S, D = x.shape
    depth = w_in.shape[0]
    H = MLA_HEADS
    ca, sa, cm, sm = _rope_tables(S)
    xt = x.reshape(B * S, D)
    for l in range(depth):
        o_kpe = MLA_Q_RANK + MLA_KV_RANK
        w_in_t = w_in[l].T
        w_lat = w_in_t[:o_kpe].astype(BF16)
        w_kpe = _spread_rope_cols(w_in_t[o_kpe:o_kpe + MLA_ROPE_DIM].T).T.astype(BF16)
        w_moba = w_in_t[o_kpe + MLA_ROPE_DIM:].astype(BF16)
        uq = mla_w_uq[l].reshape(MLA_Q_RANK, H, MLA_QK_DIM)
        w_uq = jnp.concatenate(
            [uq[..., :MLA_NOPE_DIM], _spread_rope_cols(uq[..., MLA_NOPE_DIM:])], axis=-1
        ).reshape(MLA_Q_RANK, H * QK_PAD).astype(BF16)
        ukv = mla_w_ukv[l].reshape(MLA_KV_RANK, H, MLA_NOPE_DIM + MLA_V_DIM)
        w_uk = ukv[..., :MLA_NOPE_DIM].reshape(MLA_KV_RANK, MLA_WIDTH).astype(BF16)
        w_uv = ukv[..., MLA_NOPE_DIM:].reshape(MLA_KV_RANK, MLA_WIDTH).astype(BF16)

        g1 = (ffn1_pre_g[l][None], ffn1_post_g[l][None])
        head, wg1, wu1, wd1 = _ffn(xt, g1[0], ffn1_w_gate[l], ffn1_w_up[l], ffn1_w_down[l], g1[1],
                                   tf=256, tiles=1, emit_weights=True)
        xt = _ffn(xt, g1[0], wg1, wu1, wd1, g1[1], tf=512, head=head)
        q, k, v = _proj(xt, mix_pre_g[l][None], w_lat, w_kpe, w_moba, mla_q_norm_g[l][None],
                        mla_kv_norm_g[l][None], w_uq, w_uk, w_uv, ca, sa, cm, sm, seq=S)
        am, (wg2, wu2, wd2, wo) = _attention(
            q.reshape(B, S, -1), k.reshape(B, S, -1), v.reshape(B, S, -1),
            side_weights=(ffn2_w_gate[l], ffn2_w_up[l], ffn2_w_down[l], w_out[l]))
        xt = _out_proj(xt, am.reshape(B * S, -1), mla_out_g[l][None], moba_out_g[l][None],
                       wo, mix_post_g[l][None])
        xt = _ffn_piped(xt, ffn2_pre_g[l][None], wg2, wu2, wd2, ffn2_post_g[l][None])
    return xt.reshape(B, S, D)
```

```python
import functools

import jax
import jax.numpy as jnp
from jax import lax
from jax.experimental import pallas as pl
from jax.experimental.pallas import tpu as pltpu

MLA_HEADS = 8
MLA_Q_RANK = 512
MLA_KV_RANK = 256
MLA_NOPE_DIM = 128
MLA_ROPE_DIM = 64
MLA_V_DIM = 128
MLA_QK_DIM = MLA_NOPE_DIM + MLA_ROPE_DIM
MOBA_HEADS = 8
MOBA_HEAD_DIM = 128
MOBA_BLOCK = 256
MOBA_TOPK = 3
MLA_WIDTH = MLA_HEADS * MLA_V_DIM
MOBA_WIDTH = MOBA_HEADS * MOBA_HEAD_DIM
ROPE_THETA = 10000.0
NORM_EPS = 1e-6
NEG_INF = -1e30
MASK_BIAS = -(2.0 ** 100)

LANES = 128
SUBLANES_BF16 = 16
HEADS = MLA_HEADS + MOBA_HEADS
QK_PAD = 2 * LANES
V_DIM = 128
V7X_VMEM_BYTES = 64 * 2 ** 20
SPILL_BYTES = 8 * 2 ** 20

BF16 = jnp.bfloat16
F32 = jnp.float32


def _rms(x, g):
    return x * lax.rsqrt(jnp.mean(x * x, axis=-1, keepdims=True) + NORM_EPS) * g


def _dot(a, b):
    return jnp.dot(a, b, preferred_element_type=F32)


def _dot_nt(a, b):
    return lax.dot_general(a, b, (((1,), (1,)), ((), ())), preferred_element_type=F32)


def _const_spec(shape):
    return pl.BlockSpec(shape, lambda *_: (0,) * len(shape), pipeline_mode=pl.Buffered(1))


def _ffn_kernel(*refs, emit_weights, has_head):
    x_ref, pre_g_ref, wg_ref, wu_ref, wd_ref, post_g_ref = refs[:6]
    rest = list(refs[6:])
    head_ref = rest.pop(0) if has_head else None
    o_ref = rest.pop(0)
    w16_refs = [rest.pop(0) for _ in range(3)] if emit_weights else None
    (xn_ref,) = rest

    i = pl.program_id(0)
    f = pl.program_id(1)
    last = pl.num_programs(1) - 1
    tm = x_ref.shape[0]
    halves = [slice(0, tm // 2), slice(tm // 2, tm)]

    def when_computing(cond):
        return pl.when(cond & (i > 0)) if has_head else pl.when(cond)

    def weights():
        w = [r[...].astype(BF16) for r in (wg_ref, wu_ref, wd_ref)]
        if emit_weights:
            for r16, w16 in zip(w16_refs, w):
                r16[...] = w16
        return w

    def swiglu_down(xn, w):
        h = _dot(xn, w[0])
        u = _dot(xn, w[1])
        a = (h * jax.nn.sigmoid(h)) * u
        return _dot(a.astype(BF16), w[2])

    @when_computing(f == 0)
    def _():
        w = weights()
        for rows in halves:
            xn = _rms(x_ref[rows, :], pre_g_ref[...]).astype(BF16)
            xn_ref[rows, :] = xn
            o_ref[rows, :] = swiglu_down(xn, w)

    @when_computing((f > 0) & (f < last))
    def _():
        o_ref[...] += swiglu_down(xn_ref[...], weights())

    @when_computing(f == last)
    def _():
        w = weights()
        for rows in halves:
            acc = o_ref[rows, :] + swiglu_down(xn_ref[rows, :], w)
            o_ref[rows, :] = x_ref[rows, :] + 0.5 * _rms(acc, post_g_ref[...])

    if has_head:
        @pl.when((i == 0) & (f == last))
        def _():
            o_ref[...] = head_ref[...]


def _ffn(x, pre_g, wg, wu, wd, post_g, *, tf, tm=1024, tiles=None, emit_weights=False,
         head=None):
    T, D = x.shape
    F = wg.shape[1]
    tiles = T // tm if tiles is None else tiles
    assert not emit_weights or tiles == 1
    has_head = head is not None
    wbytes = wg.dtype.itemsize
    vmem = (2 * 2 * tm * D * 4
            + tm * D * 2
            + 2 * 3 * D * tf * wbytes
            + 3 * D * tf * 2 * (3 if emit_weights else 1)
            + has_head * tm * D * 4
            + 4 * tm * tf * 4 + tm * D * 4)
    vmem = min(vmem, V7X_VMEM_BYTES)
    row = (lambda i: jnp.maximum(i, 1)) if has_head else (lambda i: i)
    col = (lambda i, f: jnp.where(i == 0, 0, f)) if has_head else (lambda i, f: f)
    w_specs = [pl.BlockSpec((D, tf), lambda i, f: (0, col(i, f))),
               pl.BlockSpec((D, tf), lambda i, f: (0, col(i, f))),
               pl.BlockSpec((tf, D), lambda i, f: (col(i, f), 0))]
    in_specs = [pl.BlockSpec((tm, D), lambda i, f: (row(i), 0)),
                pl.BlockSpec((1, D), lambda i, f: (0, 0))] + w_specs + [
                pl.BlockSpec((1, D), lambda i, f: (0, 0))]
    out_specs = [pl.BlockSpec((tm, D), lambda i, f: (i, 0))]
    out_shape = [jax.ShapeDtypeStruct((tiles * tm, D), F32)]
    args = [x, pre_g, wg, wu, wd, post_g]
    if has_head:
        in_specs.append(_const_spec((tm, D)))
        args.append(head)
    if emit_weights:
        out_specs += w_specs
        out_shape += [jax.ShapeDtypeStruct(w.shape, BF16) for w in (wg, wu, wd)]
    out = pl.pallas_call(
        functools.partial(_ffn_kernel, emit_weights=emit_weights, has_head=has_head),
        grid=(tiles, F // tf),
        in_specs=in_specs,
        out_specs=out_specs,
        out_shape=out_shape,
        scratch_shapes=[pltpu.VMEM((tm, D), BF16)],
        compiler_params=pltpu.CompilerParams(
            dimension_semantics=("arbitrary", "arbitrary"), vmem_limit_bytes=vmem),
        name="ffn",
    )(*args)
    return out if emit_weights else out[0]


def _rope(x, c, s):
    return x * c + pltpu.roll(x, LANES // 2, axis=1) * s


def _moba_bias(kmean, qh, blk):
    nb = kmean.shape[0]
    gate = _dot_nt(kmean.astype(BF16), qh)
    n = lax.broadcasted_iota(jnp.int32, gate.shape, 0)
    cand = jnp.where(n < blk, gate, -jnp.inf)
    allowed = n == blk
    for _ in range(min(MOBA_TOPK, nb - 1)):
        best = jnp.max(cand, axis=0, keepdims=True)
        first = jnp.min(jnp.where(cand == best, n, nb), axis=0, keepdims=True)
        pick = (n == first) & (best > -jnp.inf)
        allowed = allowed | pick
        cand = jnp.where(pick, -jnp.inf, cand)
    bias = jnp.where(allowed, 0.0, MASK_BIAS)
    bias = jnp.concatenate([bias, jnp.zeros((LANES - nb, bias.shape[1]), F32)], axis=0)
    return bias.T


def _proj_kernel(x_ref, g_ref, w_lat_ref, w_kpe_ref, w_moba_ref, gq_ref, gkv_ref,
                 w_uq_ref, w_uk_ref, w_uv_ref, ca_ref, sa_ref, cm_ref, sm_ref,
                 q_ref, k_ref, v_ref, kmean_ref):
    nb = kmean_ref.shape[0]
    L = MOBA_BLOCK

    @pl.when(pl.program_id(0) == 0)
    def _():
        kmean_ref[...] = jnp.zeros_like(kmean_ref)

    un = _rms(x_ref[...], g_ref[...]).astype(BF16)
    ca, sa, cm, sm = ca_ref[...], sa_ref[...], cm_ref[...], sm_ref[...]
    tm = un.shape[0]
    blk0 = (pl.program_id(0) * (tm // L)) % nb

    c_q = _dot_nt(un, w_lat_ref[:MLA_Q_RANK, :])
    c_kv = _dot_nt(un, w_lat_ref[MLA_Q_RANK:, :])
    k_pe = _dot_nt(un, w_kpe_ref[...])
    km = _dot_nt(un, w_moba_ref[MOBA_WIDTH:2 * MOBA_WIDTH, :])
    qm = _dot_nt(un, w_moba_ref[:MOBA_WIDTH, :])

    qn = _rms(c_q, gq_ref[...]).astype(BF16)
    kvn = _rms(c_kv, gkv_ref[...]).astype(BF16)
    q = _dot(qn, w_uq_ref[...])
    k_nope = _dot(kvn, w_uk_ref[...])
    v_ref[:, :MLA_WIDTH] = _dot(kvn, w_uv_ref[...]).astype(BF16)
    k_pe = _rope(k_pe, ca, sa).astype(BF16)
    for h in range(MLA_HEADS):
        lo = h * QK_PAD
        q_ref[:, lo:lo + LANES] = q[:, lo:lo + LANES].astype(BF16)
        q_ref[:, lo + LANES:lo + QK_PAD] = _rope(q[:, lo + LANES:lo + QK_PAD], ca, sa).astype(BF16)
        k_ref[:, lo:lo + LANES] = k_nope[:, h * LANES:(h + 1) * LANES].astype(BF16)
        k_ref[:, lo + LANES:lo + QK_PAD] = k_pe

    lane = lax.broadcasted_iota(jnp.int32, (L, LANES), 1)
    blk_row = lax.broadcasted_iota(jnp.int32, (nb, LANES), 0)
    for h in range(MOBA_HEADS):
        sl = slice(h * LANES, (h + 1) * LANES)
        lo = (MLA_HEADS + h) * QK_PAD
        qh = _rope(qm[:, sl], cm, sm).astype(BF16)
        kh = _rope(km[:, sl], cm, sm)
        q_ref[:, lo:lo + LANES] = qh
        k_ref[:, lo:lo + LANES] = kh.astype(BF16)
        kmean = kmean_ref[:, sl]
        for part in range(tm // L):
            rows = slice(part * L, (part + 1) * L)
            blk = blk0 + part
            kmean = jnp.where(blk_row == blk, jnp.mean(kh[rows], axis=0, keepdims=True), kmean)
            q_ref[rows, lo + LANES:lo + QK_PAD] = _moba_bias(kmean, qh[rows], blk).astype(BF16)
            k_ref[rows, lo + LANES:lo + QK_PAD] = jnp.where(lane == blk, 1.0, 0.0).astype(BF16)
        kmean_ref[:, sl] = kmean
    v_ref[:, MLA_WIDTH:] = _dot_nt(un, w_moba_ref[2 * MOBA_WIDTH:, :]).astype(BF16)


def _proj(x, g, w_lat, w_kpe, w_moba, gq, gkv, w_uq, w_uk, w_uv, ca, sa, cm, sm, *, seq, tm=512):
    T, D = x.shape
    n_in = w_lat.shape[0] + w_kpe.shape[0] + w_moba.shape[0]
    nb = seq // MOBA_BLOCK
    assert seq % tm == 0 and tm % MOBA_BLOCK == 0 and nb % SUBLANES_BF16 == 0 and nb <= LANES
    wq, wv = HEADS * QK_PAD, HEADS * V_DIM
    tok = lambda w: pl.BlockSpec((tm, w), lambda i: (i, 0))
    tab = pl.BlockSpec((tm, LANES), lambda i: (i % (seq // tm), 0))
    vmem = (2 * tm * D * 4
            + (D * n_in + MLA_Q_RANK * MLA_HEADS * QK_PAD + 2 * MLA_KV_RANK * MLA_WIDTH) * 2
            + 2 * tm * (2 * wq + wv) * 2
            + 8 * tm * MLA_HEADS * QK_PAD * 4
            + SPILL_BYTES)
    vmem = min(vmem, V7X_VMEM_BYTES)
    return pl.pallas_call(
        _proj_kernel,
        grid=(T // tm,),
        in_specs=[
            tok(D), _const_spec((1, D)),
            _const_spec(w_lat.shape), _const_spec(w_kpe.shape), _const_spec(w_moba.shape),
            _const_spec((1, MLA_Q_RANK)), _const_spec((1, MLA_KV_RANK)),
            _const_spec((MLA_Q_RANK, MLA_HEADS * QK_PAD)), _const_spec((MLA_KV_RANK, MLA_WIDTH)),
            _const_spec((MLA_KV_RANK, MLA_WIDTH)),
            tab, tab, tab, tab,
        ],
        out_specs=[tok(wq), tok(wq), tok(wv)],
        out_shape=[
            jax.ShapeDtypeStruct((T, wq), BF16),
            jax.ShapeDtypeStruct((T, wq), BF16),
            jax.ShapeDtypeStruct((T, wv), BF16),
        ],
        scratch_shapes=[pltpu.VMEM((nb, MOBA_WIDTH), F32)],
        compiler_params=pltpu.CompilerParams(
            dimension_semantics=("arbitrary",), vmem_limit_bytes=vmem),
        name="proj",
    )(x, g, w_lat, w_kpe, w_moba, gq, gkv, w_uq, w_uk, w_uv, ca, sa, cm, sm)


LOG2_E = 1.4426950408889634


def _attn_kernel(*refs, t, nsub, hp, nside):
    q_ref, k_ref, v_ref = refs[:3]
    side_in, (o_ref, *side_out) = refs[3:3 + nside], refs[3 + nside:4 + 2 * nside]
    m_ref, acc_ref = refs[4 + 2 * nside:]
    h = pl.program_id(1)
    i = pl.program_id(2)
    scale = jnp.where(h < MLA_HEADS // hp, MLA_QK_DIM ** -0.5, MOBA_HEAD_DIM ** -0.5)
    c2 = (scale * LOG2_E).astype(F32)
    m_ref[...] = jnp.full(m_ref.shape, -jnp.inf, F32)
    acc_ref[...] = jnp.zeros_like(acc_ref)
    chains = [(g, sub) for sub in range(nsub) for g in range(hp)]

    def logits(g, sub, start, tk):
        cols = slice(g * QK_PAD, (g + 1) * QK_PAD)
        q = q_ref[sub * t:(sub + 1) * t, cols]
        return _dot_nt(q, k_ref[pl.ds(pl.multiple_of(start, t), tk), cols])

    def softmax(sub, s, diagonal):
        tk = s.shape[1]
        if diagonal:
            row = lax.broadcasted_iota(jnp.int32, s.shape, 0)
            col = lax.broadcasted_iota(jnp.int32, s.shape, 1)
            s = jnp.where(col <= row, s, NEG_INF)
        chunks = [s[:, c * LANES:(c + 1) * LANES] for c in range(tk // LANES)]
        m_cur = jnp.max(functools.reduce(jnp.maximum, chunks), axis=1, keepdims=True)
        m_prev = m_ref[sub]
        m_new = jnp.maximum(m_prev, jnp.broadcast_to(m_cur, (t, LANES)) * c2)
        alpha = jnp.exp2(m_prev - m_new)
        p = [jnp.exp2(c * c2 - m_new).astype(BF16) for c in chunks]
        m_ref[sub] = m_new
        return p, alpha

    def accumulate(sub, g, p, alpha, start):
        ones = jnp.ones((t, LANES), BF16)
        n = t // LANES
        cols = slice(g * V_DIM, (g + 1) * V_DIM)
        pv = [_dot(jnp.concatenate(p[c * n:(c + 1) * n], axis=1),
                   jnp.concatenate(
                       [v_ref[pl.ds(pl.multiple_of(start + c * t, t), t), cols], ones], axis=1))
              for c in range(len(p) // n)]
        acc_ref[sub] = jnp.concatenate([alpha, alpha], axis=1) * acc_ref[sub] + sum(pv)

    def update(g, sub, s, start, diagonal):
        chain = g * nsub + sub
        accumulate(chain, g, *softmax(chain, s, diagonal), start)

    def body(j, carry):
        start = j * 2 * t
        ss = [logits(g, sub, start, 2 * t) for g, sub in chains]
        for (g, sub), s in zip(chains, ss):
            update(g, sub, s, start, False)
        return carry

    lax.fori_loop(0, i * (nsub // 2), body, 0)
    for w_ref, wo_ref in zip(side_in, side_out):
        wo_ref[...] = w_ref[...].astype(BF16)
    first = i * nsub
    for kb in range(nsub):
        for g in range(hp):
            for sub in range(kb, nsub):
                start = (first + kb) * t
                update(g, sub, logits(g, sub, start, t), start, sub == kb)
    for g, sub in chains:
        acc = acc_ref[g * nsub + sub]
        o_ref[sub * t:(sub + 1) * t, g * V_DIM:(g + 1) * V_DIM] = acc[:, :V_DIM] / acc[:, V_DIM:]


def _side_block_rows(rows, steps):
    return next(rb for rb in range(SUBLANES_BF16, rows + 1, SUBLANES_BF16)
                if rows % rb == 0 and rows // rb <= steps)


def _attention(q, k, v, side_weights=(), *, t=512, nsub=4, hp=2):
    B, S, _ = q.shape
    tq = nsub * t
    assert S % tq == 0 and t % MOBA_BLOCK == 0 and nsub % 2 == 0
    assert MLA_HEADS % hp == 0 and MOBA_HEADS % hp == 0
    nq, ng = S // tq, HEADS // hp
    steps = B * ng * nq
    kern = functools.partial(_attn_kernel, t=t, nsub=nsub, hp=hp, nside=len(side_weights))
    side_specs, side_bytes = [], 0
    for w in side_weights:
        rb = _side_block_rows(w.shape[0], steps)
        last = w.shape[0] // rb - 1
        side_specs.append(pl.BlockSpec(
            (rb, w.shape[1]),
            lambda b, h, i, last=last: (jnp.minimum((b * ng + h) * nq + i, last), 0)))
        side_bytes += 2 * rb * w.shape[1] * (4 + 2)
    vmem = (hp * (2 * S * (QK_PAD + V_DIM) * 2 + 2 * tq * QK_PAD * 2 + 2 * tq * V_DIM * 4
                  + tq * (LANES + 2 * V_DIM) * 4 + nsub * 6 * t * t * 4)
            + side_bytes + SPILL_BYTES)
    vmem = min(vmem, V7X_VMEM_BYTES)
    out = pl.pallas_call(
        kern,
        grid=(B, ng, nq),
        in_specs=[
            pl.BlockSpec((None, tq, hp * QK_PAD), lambda b, h, i: (b, i, h)),
            pl.BlockSpec((None, S, hp * QK_PAD), lambda b, h, i: (b, 0, h)),
            pl.BlockSpec((None, S, hp * V_DIM), lambda b, h, i: (b, 0, h)),
        ] + side_specs,
        out_specs=[pl.BlockSpec((None, tq, hp * V_DIM), lambda b, h, i: (b, i, h))] + side_specs,
        out_shape=[jax.ShapeDtypeStruct((B, S, HEADS * V_DIM), F32)]
        + [jax.ShapeDtypeStruct(w.shape, BF16) for w in side_weights],
        scratch_shapes=[pltpu.VMEM((hp * nsub, t, LANES), F32),
                        pltpu.VMEM((hp * nsub, t, 2 * V_DIM), F32)],
        compiler_params=pltpu.CompilerParams(
            dimension_semantics=("arbitrary", "arbitrary", "arbitrary"), vmem_limit_bytes=vmem),
        name="attention",
    )(q, k, v, *side_weights)
    return out[0], tuple(out[1:])


def _out_kernel(x_ref, am_ref, ga_ref, gm_ref, w_ref, gp_ref, side_ref, o_ref, side_out_ref):
    side_out_ref[...] = side_ref[...].astype(BF16)
    an = _rms(am_ref[:, :MLA_WIDTH], ga_ref[...]).astype(BF16)
    mn = _rms(am_ref[:, MLA_WIDTH:], gm_ref[...]).astype(BF16)
    y = _dot(jnp.concatenate([an, mn], axis=1), w_ref[...])
    o_ref[...] = x_ref[...] + _rms(y, gp_ref[...])


def _out_proj(x, am, ga, gm, w, gp, side_weight, *, tm=512):
    T, D = x.shape
    W = MLA_WIDTH + MOBA_WIDTH
    steps = T // tm
    rb = _side_block_rows(side_weight.shape[0], steps)
    assert side_weight.shape[0] // rb == steps
    side = pl.BlockSpec((rb, side_weight.shape[1]), lambda i: (i, 0))
    tok = lambda wd: pl.BlockSpec((tm, wd), lambda i: (i, 0))
    vmem = (2 * 2 * tm * D * 4 + 2 * tm * W * 4 + W * D * 2 + 4 * tm * D * 4
            + 2 * rb * side_weight.shape[1] * (4 + 2) + SPILL_BYTES)
    return pl.pallas_call(
        _out_kernel,
        grid=(steps,),
        in_specs=[tok(D), tok(W), _const_spec((1, MLA_WIDTH)), _const_spec((1, MOBA_WIDTH)),
                  _const_spec((W, D)), _const_spec((1, D)), side],
        out_specs=[tok(D), side],
        out_shape=[jax.ShapeDtypeStruct((T, D), F32),
                   jax.ShapeDtypeStruct(side_weight.shape, BF16)],
        compiler_params=pltpu.CompilerParams(
            dimension_semantics=("parallel",), vmem_limit_bytes=vmem),
        name="out_proj",
    )(x, am, ga, gm, w, gp, side_weight)


def _spread_rope_cols(w):
    half = MLA_ROPE_DIM // 2
    z = jnp.zeros(w.shape[:-1] + (LANES // 2 - half,), w.dtype)
    return jnp.concatenate([w[..., :half], z, w[..., half:], z], axis=-1)


def _rope_tables(seq):
    step = 64
    assert seq % step == 0
    lo = jnp.arange(step, dtype=F32)[:, None]
    hi = step * jnp.arange(seq // step, dtype=F32)[:, None]

    def inv_freq(dim):
        return 1.0 / (ROPE_THETA ** (jnp.arange(0, dim, 2, dtype=F32) / dim))

    def tables(inv, live):
        inv = jnp.concatenate([inv, inv])[None, :]
        sign = jnp.concatenate([-live, live])[None, :]
        ch, sh = jnp.cos(hi * inv)[:, None, :], jnp.sin(hi * inv)[:, None, :]
        cl, sl = jnp.cos(lo * inv)[None, :, :], jnp.sin(lo * inv)[None, :, :]
        cos = (ch * cl - sh * sl).reshape(seq, LANES)
        sin = (sh * cl + ch * sl).reshape(seq, LANES)
        return cos * jnp.abs(sign), sin * sign

    cm, sm = tables(inv_freq(MOBA_HEAD_DIM), jnp.ones((LANES // 2,), F32))
    pad = jnp.zeros((LANES // 2 - MLA_ROPE_DIM // 2,), F32)
    ca, sa = tables(jnp.concatenate([inv_freq(MLA_ROPE_DIM), pad]),
                    jnp.concatenate([jnp.ones((MLA_ROPE_DIM // 2,), F32), pad]))
    return ca, sa, cm, sm


def kernel(x, ffn1_pre_g, ffn1_w_gate, ffn1_w_up, ffn1_w_down, ffn1_post_g, mix_pre_g, w_in, mla_q_norm_g, mla_kv_norm_g, mla_w_uq, mla_w_ukv, mla_out_g, moba_out_g, w_out, mix_post_g, ffn2_pre_g, ffn2_w_gate, ffn2_w_up, ffn2_w_down, ffn2_post_g):
    B, S, D = x.shape
    depth = w_in.shape[0]
    H = MLA_HEADS
    ca, sa, cm, sm = _rope_tables(S)
    xt = x.reshape(B * S, D)
    for l in range(depth):
        o_kpe = MLA_Q_RANK + MLA_KV_RANK
        w_in_t = w_in[l].T
        w_lat = w_in_t[:o_kpe].astype(BF16)
        w_kpe = _spread_rope_cols(w_in_t[o_kpe:o_kpe + MLA_ROPE_DIM].T).T.astype(BF16)
        w_moba = w_in_t[o_kpe + MLA_ROPE_DIM:].astype(BF16)
        uq = mla_w_uq[l].reshape(MLA_Q_RANK, H, MLA_QK_DIM)
        w_uq = jnp.concatenate(
            [uq[..., :MLA_NOPE_DIM], _spread_rope_cols(uq[..., MLA_NOPE_DIM:])], axis=-1
        ).reshape(MLA_Q_RANK, H * QK_PAD).astype(BF16)
        ukv = mla_w_ukv[l].reshape(MLA_KV_RANK, H, MLA_NOPE_DIM + MLA_V_DIM)
        w_uk = ukv[..., :MLA_NOPE_DIM].reshape(MLA_KV_RANK, MLA_WIDTH).astype(BF16)
        w_uv = ukv[..., MLA_NOPE_DIM:].reshape(MLA_KV_RANK, MLA_WIDTH).astype(BF16)

        g1 = (ffn1_pre_g[l][None], ffn1_post_g[l][None])
        head, wg1, wu1, wd1 = _ffn(xt, g1[0], ffn1_w_gate[l], ffn1_w_up[l], ffn1_w_down[l], g1[1],
                                   tf=256, tiles=1, emit_weights=True)
        xt = _ffn(xt, g1[0], wg1, wu1, wd1, g1[1], tf=512, head=head)
        q, k, v = _proj(xt, mix_pre_g[l][None], w_lat, w_kpe, w_moba, mla_q_norm_g[l][None],
                        mla_kv_norm_g[l][None], w_uq, w_uk, w_uv, ca, sa, cm, sm, seq=S)
        am, (wg2, wu2, wo) = _attention(
            q.reshape(B, S, -1), k.reshape(B, S, -1), v.reshape(B, S, -1),
            side_weights=(ffn2_w_gate[l], ffn2_w_up[l], w_out[l]))
        xt, wd2 = _out_proj(xt, am.reshape(B * S, -1), mla_out_g[l][None], moba_out_g[l][None],
                            wo, mix_post_g[l][None], ffn2_w_down[l])
        xt = _ffn(xt, ffn2_pre_g[l][None], wg2, wu2, wd2, ffn2_post_g[l][None], tf=512)
    return xt.reshape(B, S, D)
```
